```python
import math
import jax
import jax.numpy as jnp
from jax import lax
import numpy as np

D_MODEL = 2048
BATCH = 2
SEQ = 8192
DEPTH = 4

GRID_W = 64
CTX_LEN = 256
EPS = 1e-6
N_MOD = 6

D_MIX = D_MODEL
D_LRU = 3 * D_MODEL // 8
D_HY = 3 * D_MODEL // 8
D_SG = D_MIX - D_LRU - D_HY

LRU_HEADS = 8
LRU_HEAD_DIM = D_LRU // LRU_HEADS
LRU_CONV = 4
LRU_C = 8.0

HY_ORDER = 2
HY_CONV = 3
HY_BANDS = 8
HY_EMB = 1 + 2 * HY_BANDS
HY_FILTER_HIDDEN = 64
HY_TARGET = 1e-2
HY_FAST_PCT = 0.3
HY_SLOW_PCT = 1.5
HY_MIN_DECAY = math.log(HY_TARGET) / HY_SLOW_PCT
HY_MAX_DECAY = math.log(HY_TARGET) / HY_FAST_PCT

SG_CHUNK = 2 * GRID_W
SG_HEADS = 4
SG_HEAD_DIM = D_SG // SG_HEADS

OFF_LRU_G = D_LRU
OFF_HY = 2 * D_LRU
OFF_SG = OFF_HY + 3 * D_HY
D_IN = OFF_SG + 2 * D_SG

N_EXPERTS = 32
TOP_K = 4
D_EXPERT = D_MODEL // 4
SWIGLU_LIMIT = 7.0
SWIGLU_ALPHA = 1.702
MOE_BLOCK = 256

kernel_name = "hybrid_lru_hyena_sgmlp_moe_dit"


def _rms(x):
    xf = x.astype(jnp.float32)
    return (xf * lax.rsqrt(jnp.mean(xf * xf, axis=-1, keepdims=True) + EPS)).astype(x.dtype)


def rmsnorm(x, g):
    return _rms(x) * g


def layernorm(x, g, b):
    xf = x.astype(jnp.float32)
    mu = jnp.mean(xf, axis=-1, keepdims=True)
    var = jnp.mean(jnp.square(xf - mu), axis=-1, keepdims=True)
    return ((xf - mu) * lax.rsqrt(var + EPS)).astype(x.dtype) * g + b


def depthwise_conv(x, w, b, left):
    n_taps = w.shape[0]
    length = x.shape[1]
    xp = jnp.pad(x, ((0, 0), (left, n_taps - 1 - left), (0, 0)))
    return sum(xp[:, k:k + length] * w[k] for k in range(n_taps)) + b


def _lin_combine(e1, e2):
    a1, b1 = e1
    a2, b2 = e2
    return a1 * a2, a2 * b1 + b2


def linear_scan(a, b, h0, reverse):
    if h0 is not None:
        idx = -1 if reverse else 0
        b = b.at[:, idx].add(a[:, idx] * h0)
    _, h = lax.associative_scan(_lin_combine, (a, b), reverse=reverse, axis=1)
    return h


def rglru_coeffs(x, w_a, b_a, w_x, b_x, lam):
    n_b, length, _ = x.shape
    xh = x.reshape(n_b, length, LRU_HEADS, LRU_HEAD_DIM)
    r = jax.nn.sigmoid(jnp.einsum('blhi,hij->blhj', xh, w_a).reshape(n_b, length, D_LRU) + b_a)
    i = jax.nn.sigmoid(jnp.einsum('blhi,hij->blhj', xh, w_x).reshape(n_b, length, D_LRU) + b_x)
    log_a = (-LRU_C * r * jax.nn.softplus(-lam)).astype(jnp.float32)
    a = jnp.exp(log_a)
    b = jnp.sqrt(-jnp.expm1(2.0 * log_a)) * (i * x).astype(jnp.float32)
    return a, b


def rglru_mixer(px_lat, pg_lat, px_ctx, pg_ctx, conv_w, conv_b, w_a, b_a, w_x, b_x, lam):
    dt = px_lat.dtype
    xl = depthwise_conv(px_lat, conv_w, conv_b, LRU_CONV // 2).astype(jnp.float32)
    xc = depthwise_conv(px_ctx, conv_w, conv_b, LRU_CONV // 2).astype(jnp.float32)
    h_lat, h_ctx = 0.0, 0.0
    for d, rev in enumerate((False, True)):
        a, b = rglru_coeffs(xc, w_a[d], b_a[d], w_x[d], b_x[d], lam[d])
        hc = linear_scan(a, b, None, rev)
        h_end = hc[:, 0] if rev else hc[:, -1]
        a, b = rglru_coeffs(xl, w_a[d], b_a[d], w_x[d], b_x[d], lam[d])
        h_lat = h_lat + linear_scan(a, b, h_end, rev)
        if pg_ctx is not None:
            h_ctx = h_ctx + hc
    y_lat = h_lat.astype(dt) * jax.nn.gelu(pg_lat)
    y_ctx = None if pg_ctx is None else h_ctx.astype(dt) * jax.nn.gelu(pg_ctx)
    return y_lat, y_ctx


def hyena_filters(length, w1, b1, f1, w2, b2, f2, w3, b3):
    t = jnp.linspace(0.0, 1.0, length, dtype=jnp.float32)[:, None]
    w = (2.0 * math.pi / length) * jnp.arange(length, dtype=jnp.float32)[:, None]
    f = jnp.linspace(1e-4, HY_BANDS - 1, HY_BANDS, dtype=jnp.float32)[None, :]
    z = jnp.concatenate([t, jnp.cos(f * w), -jnp.sin(f * w)], axis=-1)
    h = jnp.sin(f1 * (z @ w1 + b1))
    h = jnp.sin(f2 * (h @ w2 + b2))
    h = (h @ w3 + b3).astype(jnp.float32).reshape(length, 2, HY_ORDER, D_HY)
    deltas = jnp.abs(jnp.linspace(HY_MIN_DECAY, HY_MAX_DECAY, D_HY, dtype=jnp.float32))
    h = h * jnp.exp(-t * deltas)[:, None, None, :]
    h_f, h_b = h[:, 0], h[:, 1]
    k = jnp.concatenate([h_f.at[0].add(h_b[0]), jnp.zeros_like(h_f[:1]), h_b[:0:-1]], axis=0)
    return k * lax.rsqrt(jnp.sum(k * k, axis=0, keepdims=True) + EPS)


def fft_conv(z, k):
    length = z.shape[1]
    zf = jnp.fft.rfft(z, n=2 * length, axis=1)
    kf = jnp.fft.rfft(k, axis=0)
    return jnp.fft.irfft(zf * kf[None], n=2 * length, axis=1)[:, :length]


def hyena(p, conv_w, conv_b, w1, b1, f1, w2, b2, f2, w3, b3, skip):
    length = p.shape[1]
    v, x1, x2 = jnp.split(depthwise_conv(p, conv_w, conv_b, HY_CONV // 2), 3, axis=-1)
    k = hyena_filters(length, w1, b1, f1, w2, b2, f2, w3, b3)
    z = v.astype(jnp.float32)
    for o, gate in enumerate((x1, x2)):
        z = gate.astype(jnp.float32) * (fft_conv(z, k[:, o]) + skip[o] * z)
    return z.astype(p.dtype)


def spatial_gating(p, ln_g, ln_b, w_s, b_s):
    n_b, length, _ = p.shape
    u, v = jnp.split(jax.nn.gelu(p), 2, axis=-1)
    v = layernorm(v, ln_g, ln_b).reshape(n_b, length // SG_CHUNK, SG_CHUNK, SG_HEADS, SG_HEAD_DIM)
    v = jnp.einsum('gpq,bnqgc->bnpgc', w_s, v) + b_s.T[:, :, None]
    return u * v.reshape(n_b, length, D_SG)


def merge_groups(ys, g):
    return jnp.concatenate([_rms(y) for y in ys], axis=-1) * g


def moe_ffn(h, router_w, router_b, w1, b1, w2, b2):
    n_tok, d = h.shape
    logits = (h @ router_w + router_b).astype(jnp.float32)
    top_v, top_e = lax.top_k(logits, TOP_K)
    gate = jax.nn.softmax(top_v, axis=-1)
    n_assign = n_tok * TOP_K
    e_flat = top_e.reshape(-1)
    order = jnp.argsort(e_flat)
    e_sorted = e_flat[order]
    tok_sorted = order // TOP_K
    gate_sorted = gate.reshape(-1)[order]
    counts = jnp.bincount(e_flat, length=N_EXPERTS)
    starts = jnp.cumsum(counts) - counts
    padded = (counts + MOE_BLOCK - 1) // MOE_BLOCK * MOE_BLOCK
    pend = jnp.cumsum(padded)
    pstarts = pend - padded
    dest = pstarts[e_sorted] + (jnp.arange(n_assign) - starts[e_sorted])
    n_blocks = -(-(n_assign + N_EXPERTS * (MOE_BLOCK - 1)) // MOE_BLOCK)
    n_rows = n_blocks * MOE_BLOCK
    row_tok = jnp.full((n_rows,), n_tok, jnp.int32).at[dest].set(tok_sorted.astype(jnp.int32))
    row_gate = jnp.zeros((n_rows,), jnp.float32).at[dest].set(gate_sorted)
    block_e = jnp.minimum(jnp.searchsorted(pend, jnp.arange(n_blocks) * MOE_BLOCK, side='right'),
                          N_EXPERTS - 1)
    h_pad = jnp.concatenate([h, jnp.zeros((1, d), h.dtype)], axis=0)

    def expert_block(args):
        tok, g, e = args
        gu = h_pad[tok] @ w1[e] + b1[e]
        glu, lin = jnp.split(gu, 2, axis=-1)
        glu = jnp.minimum(glu, SWIGLU_LIMIT)
        lin = jnp.clip(lin, -SWIGLU_LIMIT, SWIGLU_LIMIT)
        act = glu * jax.nn.sigmoid(SWIGLU_ALPHA * glu) * (lin + 1.0)
        return ((act @ w2[e] + b2[e]) * g[:, None]).astype(h.dtype)

    y = lax.map(expert_block, (row_tok.reshape(n_blocks, MOE_BLOCK),
                               row_gate.reshape(n_blocks, MOE_BLOCK), block_e))
    return jax.ops.segment_sum(y.reshape(n_rows, d), row_tok, num_segments=n_tok + 1)[:n_tok]


def setup_inputs(seed: int = 0) -> dict:
    key = jax.random.key(seed)
    ks = iter(jax.random.split(key, 64))
    f32 = jnp.float32
    D = D_MODEL

    def nrm(shape, scale):
        return scale * jax.random.normal(next(ks), shape, f32)

    def gain(shape):
        return 1.0 + 0.02 * jax.random.normal(next(ks), shape, f32)

    u = jax.random.uniform(next(ks), (DEPTH, 2, D_LRU), f32, 0.9, 0.999)
    a0 = u ** (1.0 / LRU_C)
    lam = jnp.log(a0) - jnp.log1p(-a0)
    return {
        "x": nrm((BATCH, SEQ, D), 1.0),
        "c": nrm((BATCH, D), 1.0),
        "ctx": nrm((BATCH, CTX_LEN, D), 1.0),
        "c_ctx": nrm((D,), 1.0),
        "ada_w": nrm((DEPTH, D, N_MOD * D), 0.5 * D ** -0.5),
        "ada_b": nrm((DEPTH, N_MOD * D), 0.02),
        "norm_mix_g": gain((DEPTH, D)),
        "w_in": nrm((DEPTH, D, D_IN), D ** -0.5),
        "lru_conv_w": nrm((DEPTH, LRU_CONV, D_LRU), LRU_CONV ** -0.5),
        "lru_conv_b": nrm((DEPTH, D_LRU), 0.02),
        "lru_wa": nrm((DEPTH, 2, LRU_HEADS, LRU_HEAD_DIM, LRU_HEAD_DIM), LRU_HEAD_DIM ** -0.5),
        "lru_ba": nrm((DEPTH, 2, D_LRU), 0.02),
        "lru_wx": nrm((DEPTH, 2, LRU_HEADS, LRU_HEAD_DIM, LRU_HEAD_DIM), LRU_HEAD_DIM ** -0.5),
        "lru_bx": nrm((DEPTH, 2, D_LRU), 0.02),
        "lru_lam": lam,
        "hy_conv_w": nrm((DEPTH, HY_CONV, 3 * D_HY), HY_CONV ** -0.5),
        "hy_conv_b": nrm((DEPTH, 3 * D_HY), 0.02),
        "hf_w1": nrm((DEPTH, HY_EMB, HY_FILTER_HIDDEN), HY_EMB ** -0.5),
        "hf_b1": nrm((DEPTH, HY_FILTER_HIDDEN), 0.02),
        "hf_f1": 1.0 + 0.1 * jax.random.normal(next(ks), (DEPTH, HY_FILTER_HIDDEN), f32),
        "hf_w2": nrm((DEPTH, HY_FILTER_HIDDEN, HY_FILTER_HIDDEN), HY_FILTER_HIDDEN ** -0.5),
        "hf_b2": nrm((DEPTH, HY_FILTER_HIDDEN), 0.02),
        "hf_f2": 1.0 + 0.1 * jax.random.normal(next(ks), (DEPTH, HY_FILTER_HIDDEN), f32),
        "hf_w3": nrm((DEPTH, HY_FILTER_HIDDEN, 2 * HY_ORDER * D_HY), HY_FILTER_HIDDEN ** -0.5),
        "hf_b3": nrm((DEPTH, 2 * HY_ORDER * D_HY), 0.02),
        "hy_skip": nrm((DEPTH, HY_ORDER, D_HY), 0.5),
        "sg_ln_g": gain((DEPTH, D_SG)),
        "sg_ln_b": nrm((DEPTH, D_SG), 0.02),
        "sg_w": nrm((DEPTH, SG_HEADS, SG_CHUNK, SG_CHUNK), SG_CHUNK ** -0.5),
        "sg_b": gain((DEPTH, SG_HEADS, SG_CHUNK)),
        "grp_norm_g": gain((DEPTH, D_MIX)),
        "w_out": nrm((DEPTH, D_MIX, D), D_MIX ** -0.5),
        "norm_ffn_g": gain((DEPTH, D)),
        "router_w": nrm((DEPTH, D, N_EXPERTS), D ** -0.5),
        "router_b": nrm((DEPTH, N_EXPERTS), 0.01),
        "moe_w1": nrm((DEPTH, N_EXPERTS, D, 2 * D_EXPERT), D ** -0.5),
        "moe_b1": nrm((DEPTH, N_EXPERTS, 2 * D_EXPERT), 0.02),
        "moe_w2": nrm((DEPTH, N_EXPERTS, D_EXPERT, D), D_EXPERT ** -0.5),
        "moe_b2": nrm((DEPTH, N_EXPERTS, D), 0.02),
        "final_norm_g": gain((D,)),
    }


def reference(x, c, ctx, c_ctx, ada_w, ada_b, norm_mix_g, w_in, lru_conv_w, lru_conv_b,
              lru_wa, lru_ba, lru_wx, lru_bx, lru_lam, hy_conv_w, hy_conv_b,
              hf_w1, hf_b1, hf_f1, hf_w2, hf_b2, hf_f2, hf_w3, hf_b3, hy_skip,
              sg_ln_g, sg_ln_b, sg_w, sg_b, grp_norm_g, w_out, norm_ffn_g,
              router_w, router_b, moe_w1, moe_b1, moe_w2, moe_b2, final_norm_g):
    n_b, n_lat, d = x.shape
    n_ctx = ctx.shape[1]
    x_lat, x_ctx = x, ctx
    s_lat = jax.nn.silu(c)
    s_ctx = jax.nn.silu(c_ctx)
    for l in range(DEPTH):
        ctx_out = l < DEPTH - 1
        m_lat = jnp.split((s_lat @ ada_w[l] + ada_b[l])[:, None, :], N_MOD, axis=-1)
        n_mc = N_MOD if ctx_out else 2
        m_ctx = jnp.split(s_ctx @ ada_w[l][:, :n_mc * d] + ada_b[l][:n_mc * d], n_mc, axis=-1)

        h_lat = rmsnorm(x_lat, norm_mix_g[l]) * (1.0 + m_lat[1]) + m_lat[0]
        h_ctx = rmsnorm(x_ctx, norm_mix_g[l]) * (1.0 + m_ctx[1]) + m_ctx[0]
        p_lat = h_lat @ w_in[l]
        p_ctx = h_ctx @ (w_in[l] if ctx_out else w_in[l][:, :D_LRU])
        a_lat, a_ctx = rglru_mixer(
            p_lat[..., :D_LRU], p_lat[..., OFF_LRU_G:OFF_HY],
            p_ctx[..., :D_LRU], p_ctx[..., OFF_LRU_G:OFF_HY] if ctx_out else None,
            lru_conv_w[l], lru_conv_b[l], lru_wa[l], lru_ba[l], lru_wx[l], lru_bx[l], lru_lam[l])
        hy_params = (hy_conv_w[l], hy_conv_b[l], hf_w1[l], hf_b1[l], hf_f1[l], hf_w2[l],
                     hf_b2[l], hf_f2[l], hf_w3[l], hf_b3[l], hy_skip[l])
        sg_params = (sg_ln_g[l], sg_ln_b[l], sg_w[l], sg_b[l])
        b_lat = hyena(p_lat[..., OFF_HY:OFF_SG], *hy_params)
        g_lat = spatial_gating(p_lat[..., OFF_SG:], *sg_params)
        y_lat = merge_groups((a_lat, b_lat, g_lat), grp_norm_g[l]) @ w_out[l]
        x_lat = x_lat + m_lat[2] * y_lat
        if ctx_out:
            b_ctx = hyena(p_ctx[..., OFF_HY:OFF_SG], *hy_params)
            g_ctx = spatial_gating(p_ctx[..., OFF_SG:], *sg_params)
            y_ctx = merge_groups((a_ctx, b_ctx, g_ctx), grp_norm_g[l]) @ w_out[l]
            x_ctx = x_ctx + m_ctx[2] * y_ctx

        f_tok = (rmsnorm(x_lat, norm_ffn_g[l]) * (1.0 + m_lat[4]) + m_lat[3]).reshape(-1, d)
        if ctx_out:
            f_ctx = rmsnorm(x_ctx, norm_ffn_g[l]) * (1.0 + m_ctx[4]) + m_ctx[3]
            f_tok = jnp.concatenate([f_tok, f_ctx.reshape(-1, d)], axis=0)
        f_out = moe_ffn(f_tok, router_w[l], router_b[l], moe_w1[l], moe_b1[l], moe_w2[l], moe_b2[l])
        x_lat = x_lat + m_lat[5] * f_out[:n_b * n_lat].reshape(n_b, n_lat, d)
        if ctx_out:
            x_ctx = x_ctx + m_ctx[5] * f_out[n_b * n_lat:].reshape(n_b, n_ctx, d)
    return rmsnorm(x_lat, final_norm_g)
```

```python
import functools
import math

import numpy as np
import jax
import jax.numpy as jnp
from jax import lax
from jax.experimental import pallas as pl
from jax.experimental.pallas import tpu as pltpu

F32 = jnp.float32
BF16 = jnp.bfloat16

EPS = 1e-6
N_MOD = 6
LRU_HEADS = 8
LRU_CONV = 4
LRU_C = 8.0
HY_ORDER = 2
HY_CONV = 3
HY_BANDS = 8
HY_EMB = 1 + 2 * HY_BANDS
HY_EMB_PAD = 32
HY_MIN_DECAY = math.log(1e-2) / 1.5
HY_MAX_DECAY = math.log(1e-2) / 0.3
SG_CHUNK = 128
SG_HEADS = 4
N_EXPERTS = 32
TOP_K = 4
MOE_BLOCK = 256
SWIGLU_LIMIT = 7.0
SWIGLU_ALPHA = 1.702

LANES = 128
SUBLANES = 8
TM = 256
HALO = SUBLANES
DFT_INNER = 128
DISPATCH_TOK = 64
VMEM_LIMIT = 48 * 1024 * 1024


def _cparams(*sem, vmem=VMEM_LIMIT):
    return pltpu.CompilerParams(dimension_semantics=sem, vmem_limit_bytes=vmem)


def _bdot(a, b):
    return jnp.dot(a, b, preferred_element_type=F32)


def _hdot(a, b):
    return jnp.dot(a, b, preferred_element_type=F32, precision=lax.Precision.HIGHEST)


def _gelu(x):
    return 0.5 * x * (1.0 + jnp.tanh(math.sqrt(2.0 / math.pi) * (x + 0.044715 * (x * x * x))))


def _sigmoid(x):
    return 1.0 / (1.0 + jnp.exp(-x))


def _rms(x):
    return x * lax.rsqrt(jnp.mean(x * x, axis=-1, keepdims=True) + EPS)


def _ada_kernel(s_ref, w_ref, b_ref, o_ref):
    s = s_ref[...]
    s = s * _sigmoid(s)
    o_ref[0] = _bdot(s.astype(BF16), w_ref[0].astype(BF16)) + b_ref[0]


def _ada_call(cond, ada_w, ada_b):
    depth, d, n = ada_w.shape
    tn = 1536
    return pl.pallas_call(
        _ada_kernel,
        grid=(depth, n // tn),
        in_specs=[
            pl.BlockSpec((SUBLANES, d), lambda l, j: (0, 0)),
            pl.BlockSpec((1, d, tn), lambda l, j: (l, 0, j)),
            pl.BlockSpec((1, 1, tn), lambda l, j: (l, 0, j)),
        ],
        out_specs=pl.BlockSpec((1, SUBLANES, tn), lambda l, j: (l, 0, j)),
        out_shape=jax.ShapeDtypeStruct((depth, SUBLANES, n), F32),
        compiler_params=_cparams("arbitrary", "arbitrary"),
        name="ada_mod",
    )(cond, ada_w, ada_b.reshape(depth, 1, n))


def _inproj_kernel(x_ref, g_ref, sc_ref, sh_ref, w_ref, o_ref):
    h = _rms(x_ref[...]) * g_ref[...] * (1.0 + sc_ref[0]) + sh_ref[0]
    o_ref[...] = _bdot(h.astype(BF16), w_ref[...])


def _inproj_call(x, g, sc, sh, w, seg_of_tile):
    t, d = x.shape
    n = w.shape[1]
    tn = n // 2
    return pl.pallas_call(
        _inproj_kernel,
        grid=(2, t // TM),
        in_specs=[
            pl.BlockSpec((TM, d), lambda j, i: (i, 0)),
            pl.BlockSpec((1, d), lambda j, i: (0, 0)),
            pl.BlockSpec((1, 1, d), lambda j, i: (seg_of_tile(i), 0, 0)),
            pl.BlockSpec((1, 1, d), lambda j, i: (seg_of_tile(i), 0, 0)),
            pl.BlockSpec((d, tn), lambda j, i: (0, j)),
        ],
        out_specs=pl.BlockSpec((TM, tn), lambda j, i: (i, j)),
        out_shape=jax.ShapeDtypeStruct((t, n), F32),
        compiler_params=_cparams("arbitrary", "arbitrary"),
        name="in_proj",
    )(x, g, sc, sh, w)


def _lru_kernel(x_ref, xp_ref, xn_ref, cw_ref, cb_ref, wg_ref, bg_ref, lam_ref, o_ref,
                xe_s, a_s, b_s, h_s, *, reverse, n_chunks, d_lru):
    s = pl.program_id(1)
    j = (n_chunks - s) if reverse else (s - 1)
    has_prev = jnp.logical_and(s > 0, j > 0)
    has_next = jnp.logical_and(s > 0, j < n_chunks - 1)

    @pl.when(s == 0)
    def _():
        h_s[...] = jnp.zeros_like(h_s)

    xe_s[0:HALO, :] = jnp.where(has_prev, xp_ref[...], 0.0)
    xe_s[HALO:HALO + TM, :] = x_ref[...]
    xe_s[HALO + TM:HALO + TM + HALO, :] = jnp.where(has_next, xn_ref[...], 0.0)
    left = LRU_CONV // 2
    xc = cb_ref[...]
    for k in range(LRU_CONV):
        xc = xc + cw_ref[k:k + 1, :] * xe_s[pl.ds(HALO + k - left, TM), :]

    g = _bdot(xc.astype(BF16), wg_ref[...]) + bg_ref[...]
    r = _sigmoid(g[:, :d_lru])
    ig = _sigmoid(g[:, d_lru:])
    lam = lam_ref[...]
    sp = jnp.maximum(-lam, 0.0) + jnp.log1p(jnp.exp(-jnp.abs(lam)))
    log_a = (-LRU_C * r) * sp
    a = jnp.exp(log_a)
    a_s[...] = a
    b_s[...] = jnp.sqrt(-jnp.tanh(log_a) * (a * a + 1.0)) * (ig * xc)

    def step(t, h):
        tt = (TM - 1 - t) if reverse else t
        h = a_s[pl.ds(tt, 1), :] * h + b_s[pl.ds(tt, 1), :]
        o_ref[pl.ds(tt, 1), :] = h
        return h

    h_s[0:1, :] = lax.fori_loop(0, TM, step, h_s[0:1, :], unroll=8)


def _lru_call(p, cw, cb, wg, bg, lam, *, reverse, batch, n_lat, n_ctx):
    t = p.shape[0]
    d_lru = cw.shape[1]
    n_chunks = n_lat // TM
    assert n_ctx == TM
    ctx0 = batch * n_chunks
    per = TM // HALO
    last = t // HALO - 1

    def rb(b, s):
        lat = b * n_chunks + ((n_chunks - s) if reverse else (s - 1))
        return jnp.where(s == 0, ctx0 + b, lat)

    kern = functools.partial(_lru_kernel, reverse=reverse, n_chunks=n_chunks, d_lru=d_lru)
    return pl.pallas_call(
        kern,
        grid=(batch, n_chunks + 1),
        in_specs=[
            pl.BlockSpec((TM, d_lru), lambda b, s: (rb(b, s), 0)),
            pl.BlockSpec((HALO, d_lru), lambda b, s: (jnp.maximum(rb(b, s) * per - 1, 0), 0)),
            pl.BlockSpec((HALO, d_lru), lambda b, s: (jnp.minimum((rb(b, s) + 1) * per, last), 0)),
            pl.BlockSpec((LRU_CONV, d_lru), lambda b, s: (0, 0)),
            pl.BlockSpec((1, d_lru), lambda b, s: (0, 0)),
            pl.BlockSpec((d_lru, 2 * d_lru), lambda b, s: (0, 0)),
            pl.BlockSpec((1, 2 * d_lru), lambda b, s: (0, 0)),
            pl.BlockSpec((1, d_lru), lambda b, s: (0, 0)),
        ],
        out_specs=pl.BlockSpec((TM, d_lru), lambda b, s: (rb(b, s), 0)),
        out_shape=jax.ShapeDtypeStruct((t, d_lru), F32),
        scratch_shapes=[
            pltpu.VMEM((TM + 2 * HALO, d_lru), F32),
            pltpu.VMEM((TM, d_lru), F32),
            pltpu.VMEM((TM, d_lru), F32),
            pltpu.VMEM((SUBLANES, d_lru), F32),
        ],
        compiler_params=_cparams("arbitrary", "arbitrary"),
        name="rglru_rev" if reverse else "rglru_fwd",
    )(p, p, p, cw, cb, wg, bg, lam)


def _sg_kernel(u0_ref, u1_ref, v0_ref, v1_ref, lg_ref, lb_ref, ws_ref, bias_ref, o_ref):
    u = _gelu(jnp.concatenate([u0_ref[...], u1_ref[...]], axis=-1))
    v = _gelu(jnp.concatenate([v0_ref[...], v1_ref[...]], axis=-1))
    mu = jnp.mean(v, axis=-1, keepdims=True)
    dv = v - mu
    var = jnp.mean(dv * dv, axis=-1, keepdims=True)
    vb = ((dv * lax.rsqrt(var + EPS)) * lg_ref[...] + lb_ref[...]).astype(BF16)
    hd = vb.shape[1] // SG_HEADS
    for ch in range(TM // SG_CHUNK):
        r0 = ch * SG_CHUNK
        ys = [_bdot(ws_ref[g], vb[r0:r0 + SG_CHUNK, g * hd:(g + 1) * hd]) for g in range(SG_HEADS)]
        y = jnp.concatenate(ys, axis=-1) + bias_ref[...]
        o_ref[r0:r0 + SG_CHUNK, :] = u[r0:r0 + SG_CHUNK, :] * y


def _sg_call(p, off_sg, lg, lb, ws, bias):
    t = p.shape[0]
    d_sg = lg.shape[1]
    half = d_sg // 2
    c0 = off_sg // half
    specs = [pl.BlockSpec((TM, half), (lambda i, c=c0 + k: (i, c))) for k in range(4)]
    return pl.pallas_call(
        _sg_kernel,
        grid=(t // TM,),
        in_specs=specs + [
            pl.BlockSpec((1, d_sg), lambda i: (0, 0)),
            pl.BlockSpec((1, d_sg), lambda i: (0, 0)),
            pl.BlockSpec((SG_HEADS, SG_CHUNK, SG_CHUNK), lambda i: (0, 0, 0)),
            pl.BlockSpec((SG_CHUNK, d_sg), lambda i: (0, 0)),
        ],
        out_specs=pl.BlockSpec((TM, d_sg), lambda i: (i, 0)),
        out_shape=jax.ShapeDtypeStruct((t, d_sg), F32),
        compiler_params=_cparams("arbitrary"),
        name="spatial_gating",
    )(p, p, p, p, lg, lb, ws, bias)


def _hyconv_kernel(x_ref, xp_ref, xn_ref, cw_ref, cb_ref, o_ref, xe_s, *, n_chunks):
    j = pl.program_id(1) % n_chunks
    xe_s[0:HALO, :] = jnp.where(j > 0, xp_ref[...], 0.0)
    xe_s[HALO:HALO + TM, :] = x_ref[...]
    xe_s[HALO + TM:HALO + TM + HALO, :] = jnp.where(j < n_chunks - 1, xn_ref[...], 0.0)
    left = HY_CONV // 2
    y = cb_ref[...]
    for k in range(HY_CONV):
        y = y + cw_ref[k:k + 1, :] * xe_s[pl.ds(HALO + k - left, TM), :]
    o_ref[0] = y


def _hyconv_call(p, off_hy, cw, cb, *, batch, n_lat):
    t = p.shape[0]
    d_hy = cw.shape[1] // 3
    n_chunks = n_lat // TM
    c0 = off_hy // d_hy
    per = TM // HALO
    last = t // HALO - 1
    kern = functools.partial(_hyconv_kernel, n_chunks=n_chunks)
    return pl.pallas_call(
        kern,
        grid=(3, batch * n_chunks),
        in_specs=[
            pl.BlockSpec((TM, d_hy), lambda c, i: (i, c0 + c)),
            pl.BlockSpec((HALO, d_hy), lambda c, i: (jnp.maximum(i * per - 1, 0), c0 + c)),
            pl.BlockSpec((HALO, d_hy), lambda c, i: (jnp.minimum((i + 1) * per, last), c0 + c)),
            pl.BlockSpec((HY_CONV, d_hy), lambda c, i: (0, c)),
            pl.BlockSpec((1, d_hy), lambda c, i: (0, c)),
        ],
        out_specs=pl.BlockSpec((1, TM, d_hy), lambda c, i: (c, i, 0)),
        out_shape=jax.ShapeDtypeStruct((3, batch * n_lat, d_hy), F32),
        scratch_shapes=[pltpu.VMEM((TM + 2 * HALO, d_hy), F32)],
        compiler_params=_cparams("arbitrary", "arbitrary"),
        name="hyena_shortconv",
    )(p, p, p, cw, cb)


def _filter_kernel(z_ref, z0_ref, w1_ref, b1_ref, f1_ref, w2_ref, b2_ref, f2_ref, w3_ref, b3_ref,
                   w3b_ref, b3b_ref, dl_ref, kt_ref, ss_ref, *, length, rows):
    i = pl.program_id(0)

    def mlp(z, w3, b3):
        h = jnp.sin(f1_ref[...] * (_hdot(z, w1_ref[...]) + b1_ref[...]))
        h = jnp.sin(f2_ref[...] * (_hdot(h, w2_ref[...]) + b2_ref[...]))
        return _hdot(h, w3) + b3

    z = z_ref[...]
    win = jnp.exp(-z[:, 0:1] * dl_ref[...])
    k = mlp(z, w3_ref[...], b3_ref[...]) * jnp.concatenate([win] * HY_ORDER, axis=-1)
    lag0_back = mlp(z0_ref[...], w3b_ref[...], b3b_ref[...])[0:1, :]
    n = i * rows + lax.broadcasted_iota(jnp.int32, (rows, 1), 0)
    k = k + jnp.where(n == 0, lag0_back, 0.0)
    k = jnp.where(n == length, 0.0, k)
    kt_ref[...] = k

    @pl.when(i == 0)
    def _():
        ss_ref[...] = jnp.zeros_like(ss_ref)

    ss_ref[0:1, :] += jnp.sum(k * k, axis=0, keepdims=True)


def _filter_features(length):
    n = np.arange(2 * length)
    j = np.where(n < length, n, 2 * length - n).astype(np.float64)
    t = j / (length - 1)
    w = (2.0 * math.pi / length) * j
    f = np.linspace(1e-4, HY_BANDS - 1, HY_BANDS)
    z = np.zeros((2 * length, HY_EMB_PAD), np.float64)
    z[:, 0] = t
    z[:, 1:1 + HY_BANDS] = np.cos(w[:, None] * f[None, :])
    z[:, 1 + HY_BANDS:HY_EMB] = -np.sin(w[:, None] * f[None, :])
    return jnp.asarray(z, F32)


def _filter_call(length, w1, b1, f1, w2, b2, f2, w3, b3, d_hy):
    rows = TM
    hid = w1.shape[1]
    nc = HY_ORDER * d_hy
    ztab = _filter_features(length)
    w1p = jnp.zeros((HY_EMB_PAD, hid), F32).at[:HY_EMB].set(w1)
    deltas = jnp.asarray(np.abs(np.linspace(HY_MIN_DECAY, HY_MAX_DECAY, d_hy))[None, :], F32)
    n_fwd = length // rows
    kern = functools.partial(_filter_kernel, length=length, rows=rows)
    full = lambda shape: pl.BlockSpec(shape, lambda i: (0,) * len(shape))
    return pl.pallas_call(
        kern,
        grid=(2 * length // rows,),
        in_specs=[
            pl.BlockSpec((rows, HY_EMB_PAD), lambda i: (i, 0)),
            pl.BlockSpec((SUBLANES, HY_EMB_PAD), lambda i: (0, 0)),
            full((HY_EMB_PAD, hid)), full((1, hid)), full((1, hid)),
            full((hid, hid)), full((1, hid)), full((1, hid)),
            pl.BlockSpec((hid, nc), lambda i: (0, (i >= n_fwd).astype(jnp.int32))),
            pl.BlockSpec((1, nc), lambda i: (0, (i >= n_fwd).astype(jnp.int32))),
            pl.BlockSpec((hid, nc), lambda i: (0, 1)),
            pl.BlockSpec((1, nc), lambda i: (0, 1)),
            full((1, d_hy)),
        ],
        out_specs=[
            pl.BlockSpec((rows, nc), lambda i: (i, 0)),
            pl.BlockSpec((SUBLANES, nc), lambda i: (0, 0)),
        ],
        out_shape=[
            jax.ShapeDtypeStruct((2 * length, nc), F32),
            jax.ShapeDtypeStruct((SUBLANES, nc), F32),
        ],
        compiler_params=_cparams("arbitrary"),
        name="hyena_filter",
    )(ztab, ztab, w1p, b1[None], f1[None], w2, b2[None], f2[None], w3, b3[None], w3, b3[None], deltas)


def _dft_mats(length):
    n_fft = 2 * length
    n1 = n_fft // DFT_INNER
    n1h = n1 // 2
    k = np.arange(n1)
    th = 2.0 * math.pi * np.outer(k, np.arange(n1h)) / n1
    c, s = np.cos(th), np.sin(th)
    w_outer = np.block([[c, s], [-s, c]])
    th = 2.0 * math.pi * np.outer(k, k) / n1
    w_outer_real = np.concatenate([np.cos(th), -np.sin(th)], axis=0)
    m = np.arange(DFT_INNER)
    th = 2.0 * math.pi * np.outer(m, m) / DFT_INNER
    c, s = np.cos(th), np.sin(th)
    w_inner = np.block([[c, s], [-s, c]])
    w_inner_inv = np.block([[c, -s], [s, c]])
    th = 2.0 * math.pi * np.outer(np.arange(n1h), k) / n1
    c, s = np.cos(th) / n_fft, np.sin(th) / n_fft
    w_outer_inv = np.block([[c, -s], [s, c]])
    th = 2.0 * math.pi * np.outer(k, m) / n_fft
    tw = np.stack([np.cos(th), -np.sin(th)], axis=0)
    as_bf = lambda a: jnp.asarray(a, F32).astype(BF16)
    tw_b = jnp.broadcast_to(jnp.asarray(tw, F32)[..., None], (2, n1, DFT_INNER, LANES))
    return dict(outer=as_bf(w_outer), outer_real=as_bf(w_outer_real), inner=as_bf(w_inner),
                inner_inv=as_bf(w_inner_inv), outer_inv=as_bf(w_outer_inv), tw=tw_b, n1=n1)


def _dft_outer_kernel(x_ref, w_ref, o_ref):
    x = x_ref[...]
    if x.ndim == 3:
        x = x.reshape(x.shape[0] * x.shape[1], x.shape[2])
    o_ref[...] = _bdot(w_ref[...], x.astype(BF16))


def _dft_outer_call(x, w, nt):
    ncol = x.shape[-1]
    m, r = w.shape
    if x.ndim == 3:
        xspec = pl.BlockSpec((x.shape[0], x.shape[1], nt), lambda i: (0, 0, i))
    else:
        xspec = pl.BlockSpec((x.shape[0], nt), lambda i: (0, i))
    return pl.pallas_call(
        _dft_outer_kernel,
        grid=(ncol // nt,),
        in_specs=[xspec, pl.BlockSpec((m, r), lambda i: (0, 0))],
        out_specs=pl.BlockSpec((m, nt), lambda i: (0, i)),
        out_shape=jax.ShapeDtypeStruct((m, ncol), F32),
        compiler_params=_cparams("arbitrary"),
        name="hyena_dft_outer",
    )(x, w)


def _twiddled(a_ref, tw_ref, reps):
    ar, ai = a_ref[0, 0], a_ref[1, 0]
    twr = jnp.concatenate([tw_ref[0, 0]] * reps, axis=-1)
    twi = jnp.concatenate([tw_ref[1, 0]] * reps, axis=-1)
    return ar * twr - ai * twi, ar * twi + ai * twr, twr, twi


def _filter_spectrum_kernel(a_ref, tw_ref, w_ref, o_ref, *, reps):
    br, bi, _, _ = _twiddled(a_ref, tw_ref, reps)
    x = _bdot(w_ref[...], jnp.concatenate([br, bi], axis=0).astype(BF16))
    o_ref[0, 0] = x[:DFT_INNER]
    o_ref[1, 0] = x[DFT_INNER:]


def _filter_spectrum_call(a, tw, w_inner):
    _, n1, _, c = a.shape
    blk = pl.BlockSpec((2, 1, DFT_INNER, c), lambda k: (0, k, 0, 0))
    return pl.pallas_call(
        functools.partial(_filter_spectrum_kernel, reps=c // LANES),
        grid=(n1,),
        in_specs=[blk, pl.BlockSpec((2, 1, DFT_INNER, LANES), lambda k: (0, k, 0, 0)),
                  pl.BlockSpec((2 * DFT_INNER, 2 * DFT_INNER), lambda k: (0, 0))],
        out_specs=blk,
        out_shape=jax.ShapeDtypeStruct(a.shape, F32),
        compiler_params=_cparams("arbitrary"),
        name="hyena_filter_spectrum",
    )(a, tw, w_inner)


def _spectral_kernel(a_ref, tw_ref, kf_ref, w_ref, wi_ref, o_ref, *, reps):
    br, bi, twr, twi = _twiddled(a_ref, tw_ref, reps)
    x = _bdot(w_ref[...], jnp.concatenate([br, bi], axis=0).astype(BF16))
    xr, xi = x[:DFT_INNER], x[DFT_INNER:]
    kr, ki = kf_ref[0, 0], kf_ref[1, 0]
    yr = xr * kr - xi * ki
    yi = xr * ki + xi * kr
    c = _bdot(wi_ref[...], jnp.concatenate([yr, yi], axis=0).astype(BF16))
    cr, ci = c[:DFT_INNER], c[DFT_INNER:]
    o_ref[0, 0] = cr * twr + ci * twi
    o_ref[1, 0] = ci * twr - cr * twi


def _spectral_call(a, tw, kf, order, w_inner, w_inner_inv):
    _, n1, _, c = a.shape
    blk = pl.BlockSpec((2, 1, DFT_INNER, c), lambda k: (0, k, 0, 0))
    sq = pl.BlockSpec((2 * DFT_INNER, 2 * DFT_INNER), lambda k: (0, 0))
    return pl.pallas_call(
        functools.partial(_spectral_kernel, reps=c // LANES),
        grid=(n1,),
        in_specs=[blk, pl.BlockSpec((2, 1, DFT_INNER, LANES), lambda k: (0, k, 0, 0)),
                  pl.BlockSpec((2, 1, DFT_INNER, c), lambda k: (0, k, 0, order)), sq, sq],
        out_specs=blk,
        out_shape=jax.ShapeDtypeStruct(a.shape, F32),
        compiler_params=_cparams("arbitrary"),
        name="hyena_spectral",
    )(a, tw, kf, w_inner, w_inner_inv)


def _dft_outer_inv_kernel(c_ref, w_ref, z_ref, gate_ref, skip_ref, ss_ref, o_ref):
    y = _bdot(w_ref[...], c_ref[...].astype(BF16))
    y = y.reshape(z_ref.shape)
    scale = lax.rsqrt(ss_ref[...] + EPS)
    o_ref[...] = gate_ref[...] * (scale * y + skip_ref[...] * z_ref[...])


def _dft_outer_inv_call(cp, w, z, gate, skip_t, ss_t, nt):
    b, n1h, ncol = z.shape
    m, r = w.shape
    zspec = pl.BlockSpec((b, n1h, nt), lambda i: (0, 0, i))
    vspec = pl.BlockSpec((1, nt), lambda i: (0, i))
    return pl.pallas_call(
        _dft_outer_inv_kernel,
        grid=(ncol // nt,),
        in_specs=[pl.BlockSpec((r, nt), lambda i: (0, i)), pl.BlockSpec((m, r), lambda i: (0, 0)),
                  zspec, zspec, vspec, vspec],
        out_specs=zspec,
        out_shape=jax.ShapeDtypeStruct(z.shape, F32),
        compiler_params=_cparams("arbitrary"),
        name="hyena_dft_outer_inv",
    )(cp, w, z, gate, skip_t, ss_t)


def _ctx_dft_mats(n_ctx):
    n_fft = 2 * n_ctx
    f = np.arange(n_fft)
    th = 2.0 * math.pi * np.outer(f, np.arange(n_ctx)) / n_fft
    fz = np.concatenate([np.cos(th), -np.sin(th)], axis=0)
    th = 2.0 * math.pi * np.outer(f, f) / n_fft
    fk = np.concatenate([np.cos(th), -np.sin(th)], axis=0)
    th = 2.0 * math.pi * np.outer(np.arange(n_ctx), f) / n_fft
    fi = np.concatenate([np.cos(th), -np.sin(th)], axis=1) / n_fft
    as_bf = lambda a: jnp.asarray(a, F32).astype(BF16)
    return as_bf(fz), as_bf(fk), as_bf(fi)


def _hyctx_kernel(p0_ref, p1_ref, p2_ref, cw_ref, cb_ref, kt_ref, ss_ref, skip_ref,
                  fz_ref, fk_ref, fi_ref, o_ref, xe_s, *, d_hy):
    n_fft = fk_ref.shape[1]
    left = HY_CONV // 2

    def conv(p_ref, c):
        xe_s[0:HALO, :] = jnp.zeros((HALO, d_hy), F32)
        xe_s[HALO:HALO + TM, :] = p_ref[...]
        xe_s[HALO + TM:HALO + TM + HALO, :] = jnp.zeros((HALO, d_hy), F32)
        y = cb_ref[:, c * d_hy:(c + 1) * d_hy]
        for k in range(HY_CONV):
            y = y + cw_ref[k:k + 1, c * d_hy:(c + 1) * d_hy] * xe_s[pl.ds(HALO + k - left, TM), :]
        return y

    z = conv(p0_ref, 0)
    gates = (conv(p1_ref, 1), conv(p2_ref, 2))
    for o in range(HY_ORDER):
        cols = slice(o * d_hy, (o + 1) * d_hy)
        kf = _bdot(fk_ref[...], kt_ref[:, cols].astype(BF16))
        zf = _bdot(fz_ref[...], z.astype(BF16))
        zr, zi = zf[:n_fft], zf[n_fft:]
        kr, ki = kf[:n_fft], kf[n_fft:]
        yf = jnp.concatenate([zr * kr - zi * ki, zr * ki + zi * kr], axis=0)
        y = _bdot(fi_ref[...], yf.astype(BF16))
        scale = lax.rsqrt(ss_ref[0:1, cols] + EPS)
        z = gates[o] * (scale * y + skip_ref[o:o + 1, :] * z)
    o_ref[...] = z


def _hyctx_call(p, off_hy, cw, cb, kt, ss, skip, mats, *, batch, n_lat, n_ctx):
    d_hy = skip.shape[1]
    assert n_ctx == TM
    c0 = off_hy // d_hy
    r0 = batch * n_lat // TM
    fz, fk, fi = mats
    full = lambda a: pl.BlockSpec(a.shape, lambda b: (0,) * a.ndim)
    return pl.pallas_call(
        functools.partial(_hyctx_kernel, d_hy=d_hy),
        grid=(batch,),
        in_specs=[pl.BlockSpec((TM, d_hy), (lambda b, c=c0 + k: (r0 + b, c))) for k in range(3)]
        + [full(cw), full(cb), full(kt), full(ss), full(skip), full(fz), full(fk), full(fi)],
        out_specs=pl.BlockSpec((TM, d_hy), lambda b: (b, 0)),
        out_shape=jax.ShapeDtypeStruct((batch * n_ctx, d_hy), F32),
        scratch_shapes=[pltpu.VMEM((TM + 2 * HALO, d_hy), F32)],
        compiler_params=_cparams("arbitrary"),
        name="hyena_ctx",
    )(p, p, p, cw, cb, kt, ss, skip, fz, fk, fi)


def _merge_kernel(x_ref, hf_ref, hb_ref, pg_ref, hy_ref, sg_ref, gg_ref, w_ref, gate_ref, o_ref):
    a = (hf_ref[...] + hb_ref[...]) * _gelu(pg_ref[...])
    y = jnp.concatenate([_rms(a), _rms(hy_ref[...]), _rms(sg_ref[...])], axis=-1) * gg_ref[...]
    o_ref[...] = x_ref[...] + gate_ref[0] * _bdot(y.astype(BF16), w_ref[...])


def _merge_call(x, hf, hb, p, hy, sg, gg, w, gate, seg_of_tile, n_rows):
    d = x.shape[1]
    d_lru, d_hy, d_sg = hf.shape[1], hy.shape[1], sg.shape[1]
    row = lambda width, col=0: pl.BlockSpec((TM, width), lambda i: (i, col))
    return pl.pallas_call(
        _merge_kernel,
        grid=(n_rows // TM,),
        in_specs=[row(d), row(d_lru), row(d_lru), row(d_lru, 1), row(d_hy), row(d_sg),
                  pl.BlockSpec((1, d), lambda i: (0, 0)),
                  pl.BlockSpec((d, d), lambda i: (0, 0)),
                  pl.BlockSpec((1, 1, d), lambda i: (seg_of_tile(i), 0, 0))],
        out_specs=row(d),
        out_shape=jax.ShapeDtypeStruct((n_rows, d), F32),
        compiler_params=_cparams("arbitrary"),
        name="merge_out_proj",
    )(x, hf, hb, p, hy, sg, gg, w, gate)


def _router_kernel(x_ref, g_ref, sc_ref, sh_ref, rw_ref, rb_ref, f_ref, e_ref, gt_ref, m_ref):
    f = _rms(x_ref[...]) * g_ref[...] * (1.0 + sc_ref[0]) + sh_ref[0]
    f_ref[...] = f
    logits = _hdot(f, rw_ref[...]) + rb_ref[...]
    lane = lax.broadcasted_iota(jnp.int32, logits.shape, 1).astype(F32)
    work = logits
    e_out = jnp.zeros_like(logits)
    mask = jnp.zeros_like(logits)
    vals = []
    for k in range(TOP_K):
        mx = jnp.max(work, axis=-1, keepdims=True)
        idx = jnp.min(jnp.where(work == mx, lane, float(LANES)), axis=-1, keepdims=True)
        sel = lane == idx
        e_out = jnp.where(lane == float(k), idx, e_out)
        mask = jnp.where(sel, 1.0, mask)
        work = jnp.where(sel, -jnp.inf, work)
        vals.append(mx)
    ex = [jnp.exp(v - vals[0]) for v in vals]
    den = ex[0]
    for v in ex[1:]:
        den = den + v
    g_out = jnp.zeros_like(logits)
    for k in range(TOP_K):
        g_out = jnp.where(lane == float(k), ex[k] / den, g_out)
    e_ref[...] = e_out.astype(jnp.int32)
    gt_ref[...] = g_out
    m_ref[...] = mask


def _router_call(x, g, sc, sh, rw, rb, seg_of_tile, n_rows):
    d = x.shape[1]
    row = lambda width: pl.BlockSpec((TM, width), lambda i: (i, 0))
    mod = pl.BlockSpec((1, 1, d), lambda i: (seg_of_tile(i), 0, 0))
    return pl.pallas_call(
        _router_kernel,
        grid=(n_rows // TM,),
        in_specs=[row(d), pl.BlockSpec((1, d), lambda i: (0, 0)), mod, mod,
                  pl.BlockSpec((d, LANES), lambda i: (0, 0)), pl.BlockSpec((1, LANES), lambda i: (0, 0))],
        out_specs=[row(d), row(LANES), row(LANES), row(LANES)],
        out_shape=[jax.ShapeDtypeStruct((n_rows, d), F32), jax.ShapeDtypeStruct((n_rows, LANES), jnp.int32),
                   jax.ShapeDtypeStruct((n_rows, LANES), F32), jax.ShapeDtypeStruct((n_rows, LANES), F32)],
        compiler_params=_cparams("arbitrary"),
        name="moe_router",
    )(x, g, sc, sh, rw, rb)


def _rank_kernel(m_ref, tri_ref, pos_ref, cnt_ref, carry_s):
    i = pl.program_id(0)

    @pl.when(i == 0)
    def _():
        carry_s[...] = jnp.zeros_like(carry_s)

    m = m_ref[...]
    pos_ref[...] = _bdot(tri_ref[...], m.astype(BF16)) + carry_s[0:1, :]
    carry_s[0:1, :] += jnp.sum(m, axis=0, keepdims=True)
    cnt_ref[...] = jnp.broadcast_to(carry_s[0:1, :], cnt_ref.shape)


def _rank_call(mask):
    n = mask.shape[0]
    tri = jnp.asarray(np.tril(np.ones((TM, TM)), -1), BF16)
    return pl.pallas_call(
        _rank_kernel,
        grid=(n // TM,),
        in_specs=[pl.BlockSpec((TM, LANES), lambda i: (i, 0)), pl.BlockSpec((TM, TM), lambda i: (0, 0))],
        out_specs=[pl.BlockSpec((TM, LANES), lambda i: (i, 0)), pl.BlockSpec((SUBLANES, LANES), lambda i: (0, 0))],
        out_shape=[jax.ShapeDtypeStruct((n, LANES), F32), jax.ShapeDtypeStruct((SUBLANES, LANES), F32)],
        scratch_shapes=[pltpu.VMEM((SUBLANES, LANES), F32)],
        compiler_params=_cparams("arbitrary"),
        name="moe_rank",
    )(mask, tri)


def _slot_kernel(e_ref, pos_ref, start_ref, o_ref):
    sd = pos_ref[...] + start_ref[...]
    e = e_ref[...]
    lane = lax.broadcasted_iota(jnp.int32, sd.shape, 1)
    out = jnp.zeros_like(sd)
    for k in range(TOP_K):
        sk = jnp.sum(jnp.where(lane == e[:, k:k + 1], sd, 0.0), axis=-1, keepdims=True)
        out = jnp.where(lane == k, sk, out)
    o_ref[...] = out.astype(jnp.int32)


def _slot_call(e4, pos, start):
    n = e4.shape[0]
    row = pl.BlockSpec((TM, LANES), lambda i: (i, 0))
    return pl.pallas_call(
        _slot_kernel,
        grid=(n // TM,),
        in_specs=[row, row, pl.BlockSpec((1, LANES), lambda i: (0, 0))],
        out_specs=row,
        out_shape=jax.ShapeDtypeStruct((n, LANES), jnp.int32),
        compiler_params=_cparams("arbitrary"),
        name="moe_slot",
    )(e4, pos, start)


def _row_copy(src, dst, src_row, dst_row, sem):
    return pltpu.make_async_copy(src.at[pl.ds(src_row, 1)], dst.at[pl.ds(dst_row, 1)], sem)


def _dispatch_kernel(slot_ref, f_hbm, init_hbm, xs_hbm, sem):
    del init_hbm
    i = pl.program_id(0)
    per_step = DISPATCH_TOK * TOP_K

    def issue(tl, c):
        t = i * DISPATCH_TOK + tl
        for k in range(TOP_K):
            _row_copy(f_hbm, xs_hbm, t, slot_ref[t * TOP_K + k], sem).start()
        return c

    lax.fori_loop(0, DISPATCH_TOK, issue, 0)

    def drain(r, c):
        _row_copy(f_hbm, xs_hbm, 0, 0, sem).wait()
        return c

    lax.fori_loop(0, per_step, drain, 0)


def _dispatch_call(slot_flat, f, n_rows):
    n, d = f.shape
    return pl.pallas_call(
        _dispatch_kernel,
        grid_spec=pltpu.PrefetchScalarGridSpec(
            num_scalar_prefetch=1,
            grid=(n // DISPATCH_TOK,),
            in_specs=[pl.BlockSpec(memory_space=pl.ANY), pl.BlockSpec(memory_space=pl.ANY)],
            out_specs=pl.BlockSpec(memory_space=pl.ANY),
            scratch_shapes=[pltpu.SemaphoreType.DMA(())],
        ),
        out_shape=jax.ShapeDtypeStruct((n_rows, d), F32),
        input_output_aliases={2: 0},
        compiler_params=pltpu.CompilerParams(dimension_semantics=("arbitrary",), has_side_effects=True),
        name="moe_dispatch",
    )(slot_flat, f, jnp.zeros((n_rows, d), F32))


def _expert_kernel(be_ref, nu_ref, x_ref, w1_ref, b1_ref, w2_ref, b2_ref, o_ref):
    i = pl.program_id(0)
    d_e = w2_ref.shape[1]

    @pl.when(i < nu_ref[0])
    def _():
        gu = _bdot(x_ref[...].astype(BF16), w1_ref[0].astype(BF16)) + b1_ref[0]
        glu = jnp.minimum(gu[:, :d_e], SWIGLU_LIMIT)
        lin = jnp.clip(gu[:, d_e:], -SWIGLU_LIMIT, SWIGLU_LIMIT)
        act = glu * _sigmoid(SWIGLU_ALPHA * glu) * (lin + 1.0)
        o_ref[...] = _bdot(act.astype(BF16), w2_ref[0].astype(BF16)) + b2_ref[0]

    @pl.when(i >= nu_ref[0])
    def _():
        o_ref[...] = jnp.zeros_like(o_ref)


def _expert_call(block_e, n_used, xs, w1, b1, w2, b2):
    n_rows, d = xs.shape
    n_e, _, d_gu = w1.shape
    d_e = w2.shape[1]
    return pl.pallas_call(
        _expert_kernel,
        grid_spec=pltpu.PrefetchScalarGridSpec(
            num_scalar_prefetch=2,
            grid=(n_rows // MOE_BLOCK,),
            in_specs=[
                pl.BlockSpec((MOE_BLOCK, d), lambda i, be, nu: (i, 0)),
                pl.BlockSpec((1, d, d_gu), lambda i, be, nu: (be[i], 0, 0)),
                pl.BlockSpec((1, 1, d_gu), lambda i, be, nu: (be[i], 0, 0)),
                pl.BlockSpec((1, d_e, d), lambda i, be, nu: (be[i], 0, 0)),
                pl.BlockSpec((1, 1, d), lambda i, be, nu: (be[i], 0, 0)),
            ],
            out_specs=pl.BlockSpec((MOE_BLOCK, d), lambda i, be, nu: (i, 0)),
        ),
        out_shape=jax.ShapeDtypeStruct((n_rows, d), F32),
        compiler_params=_cparams("arbitrary"),
        name="moe_experts",
    )(block_e, n_used, xs, w1, b1.reshape(n_e, 1, d_gu), w2, b2.reshape(n_e, 1, d))


def _combine_kernel(slot_ref, x_ref, g_ref, gate_ref, ys_hbm, o_ref, buf, sem):
    i = pl.program_id(0)
    n = pl.num_programs(0)
    per_step = DISPATCH_TOK * TOP_K

    def issue(step, b):
        def body(tl, c):
            t = step * DISPATCH_TOK + tl
            for k in range(TOP_K):
                _row_copy(ys_hbm, buf.at[b], slot_ref[t * TOP_K + k], k * DISPATCH_TOK + tl, sem.at[b]).start()
            return c
        lax.fori_loop(0, DISPATCH_TOK, body, 0)

    @pl.when(i == 0)
    def _():
        issue(0, 0)

    @pl.when(i + 1 < n)
    def _():
        issue(i + 1, (i + 1) % 2)

    cur = i % 2

    def drain(r, c):
        _row_copy(ys_hbm, buf.at[cur], 0, 0, sem.at[cur]).wait()
        return c

    lax.fori_loop(0, per_step, drain, 0)
    g = g_ref[...]
    acc = jnp.zeros(o_ref.shape, F32)
    for k in range(TOP_K):
        acc = acc + g[:, k:k + 1] * buf[cur, pl.ds(k * DISPATCH_TOK, DISPATCH_TOK), :]
    o_ref[...] = x_ref[...] + gate_ref[0] * acc


def _combine_call(slot_flat, x, g4, gate, ys, seg_of_tile, n_tok):
    d = x.shape[1]
    return pl.pallas_call(
        _combine_kernel,
        grid_spec=pltpu.PrefetchScalarGridSpec(
            num_scalar_prefetch=1,
            grid=(n_tok // DISPATCH_TOK,),
            in_specs=[
                pl.BlockSpec((DISPATCH_TOK, d), lambda i, s: (i, 0)),
                pl.BlockSpec((DISPATCH_TOK, LANES), lambda i, s: (i, 0)),
                pl.BlockSpec((1, 1, d), lambda i, s: (seg_of_tile(i), 0, 0)),
                pl.BlockSpec(memory_space=pl.ANY),
            ],
            out_specs=pl.BlockSpec((DISPATCH_TOK, d), lambda i, s: (i, 0)),
            scratch_shapes=[pltpu.VMEM((2, DISPATCH_TOK * TOP_K, d), F32), pltpu.SemaphoreType.DMA((2,))],
        ),
        out_shape=jax.ShapeDtypeStruct((n_tok, d), F32),
        compiler_params=_cparams("arbitrary"),
        name="moe_combine",
    )(slot_flat, x, g4, gate, ys)


def _moe(x, norm_g, sc, sh, gate, rw, rb, w1, b1, w2, b2, seg_of_row_tile, n_tok, n_lat_total, batch):
    d = x.shape[1]
    f, e4, g4, mask = _router_call(x, norm_g, sc, sh, rw, rb, seg_of_row_tile(TM), n_tok)
    pos, cnt = _rank_call(mask)
    counts = cnt[0, :N_EXPERTS].astype(jnp.int32)
    padded = (counts + MOE_BLOCK - 1) // MOE_BLOCK * MOE_BLOCK
    pend = jnp.cumsum(padded)
    pstart = pend - padded
    n_blocks = -(-(n_tok * TOP_K + N_EXPERTS * (MOE_BLOCK - 1)) // MOE_BLOCK)
    block_e = jnp.minimum(jnp.searchsorted(pend, jnp.arange(n_blocks, dtype=jnp.int32) * MOE_BLOCK, side='right'),
                          N_EXPERTS - 1).astype(jnp.int32)
    n_used = (pend[-1:] // MOE_BLOCK).astype(jnp.int32)
    start = jnp.zeros((1, LANES), F32).at[0, :N_EXPERTS].set(pstart.astype(F32))
    slot = _slot_call(e4, pos, start)
    slot_flat = slot[:, :TOP_K].reshape(-1)
    xs = _dispatch_call(slot_flat, f, n_blocks * MOE_BLOCK)
    ys = _expert_call(block_e, n_used, xs, w1, b1, w2, b2)
    return _combine_call(slot_flat, x, g4, gate, ys, seg_of_row_tile(DISPATCH_TOK), n_tok)


def _final_kernel(x_ref, g_ref, o_ref):
    o_ref[...] = _rms(x_ref[...]) * g_ref[...]


def _final_call(x, g, n_rows):
    d = x.shape[1]
    return pl.pallas_call(
        _final_kernel,
        grid=(n_rows // TM,),
        in_specs=[pl.BlockSpec((TM, d), lambda i: (i, 0)), pl.BlockSpec((1, d), lambda i: (0, 0))],
        out_specs=pl.BlockSpec((TM, d), lambda i: (i, 0)),
        out_shape=jax.ShapeDtypeStruct((n_rows, d), F32),
        compiler_params=_cparams("arbitrary"),
        name="final_norm",
    )(x, g)


def _blockdiag(w):
    h, hd, _ = w.shape
    return jnp.einsum('hij,hg->higj', w, jnp.eye(h, dtype=w.dtype)).reshape(h * hd, h * hd)


def kernel(x, c, ctx, c_ctx, ada_w, ada_b, norm_mix_g, w_in, lru_conv_w, lru_conv_b, lru_wa, lru_ba, lru_wx, lru_bx, lru_lam, hy_conv_w, hy_conv_b, hf_w1, hf_b1, hf_f1, hf_w2, hf_b2, hf_f2, hf_w3, hf_b3, hy_skip, sg_ln_g, sg_ln_b, sg_w, sg_b, grp_norm_g, w_out, norm_ffn_g, router_w, router_b, moe_w1, moe_b1, moe_w2, moe_b2, final_norm_g):
    batch, n_lat, d = x.shape
    n_ctx = ctx.shape[1]
    depth = ada_w.shape[0]
    d_lru = lru_conv_w.shape[2]
    d_hy = hy_skip.shape[2]
    d_sg = sg_ln_g.shape[1]
    off_hy = 2 * d_lru
    off_sg = off_hy + 3 * d_hy
    assert batch == 2 and n_ctx == TM and n_lat % (TM * 2) == 0 and batch + 1 <= SUBLANES
    n_lat_total = batch * n_lat
    n_all = n_lat_total + batch * n_ctx

    def seg_of_row_tile(rows):
        per_seq = n_lat // rows
        return lambda i: jnp.minimum(i // per_seq, batch)

    cond = jnp.zeros((SUBLANES, d), F32).at[:batch].set(c).at[batch].set(c_ctx)
    mods = _ada_call(cond, ada_w, ada_b)

    mats = _dft_mats(n_lat)
    ctx_mats = _ctx_dft_mats(n_ctx)
    n1 = mats["n1"]
    ncol = DFT_INNER * d_hy
    nt = 4096
    sg_bias_shape = (SG_CHUNK, d_sg)

    xt = jnp.concatenate([x.reshape(n_lat_total, d), ctx.reshape(batch * n_ctx, d)], axis=0)
    for l in range(depth):
        ctx_out = l < depth - 1
        m = mods[l, :batch + 1].reshape(batch + 1, N_MOD, d)
        mod = [m[:, j][:, None, :] for j in range(N_MOD)]

        p = _inproj_call(xt, norm_mix_g[l][None], mod[1], mod[0], w_in[l].astype(BF16), seg_of_row_tile(TM))

        hs = []
        for dr, rev in enumerate((False, True)):
            wg = jnp.concatenate([_blockdiag(lru_wa[l, dr]), _blockdiag(lru_wx[l, dr])], axis=1).astype(BF16)
            bg = jnp.concatenate([lru_ba[l, dr], lru_bx[l, dr]])[None]
            hs.append(_lru_call(p, lru_conv_w[l], lru_conv_b[l][None], wg, bg, lru_lam[l, dr][None],
                                reverse=rev, batch=batch, n_lat=n_lat, n_ctx=n_ctx))

        sg_bias = jnp.broadcast_to(sg_b[l].T[:, :, None], (SG_CHUNK, SG_HEADS, d_sg // SG_HEADS)).reshape(sg_bias_shape)
        sg = _sg_call(p, off_sg, sg_ln_g[l][None], sg_ln_b[l][None], sg_w[l].astype(BF16), sg_bias)

        filt = (hf_w1[l], hf_b1[l], hf_f1[l], hf_w2[l], hf_b2[l], hf_f2[l], hf_w3[l], hf_b3[l])
        kt, ss = _filter_call(n_lat, *filt, d_hy)
        ka = _dft_outer_call(kt.reshape(n1, DFT_INNER * HY_ORDER * d_hy), mats["outer_real"], nt)
        kf = _filter_spectrum_call(ka.reshape(2, n1, DFT_INNER, HY_ORDER * d_hy), mats["tw"], mats["inner"])
        vxx = _hyconv_call(p, off_hy, hy_conv_w[l], hy_conv_b[l][None], batch=batch, n_lat=n_lat)
        vxx = vxx.reshape(3, batch, n1 // 2, ncol)
        z = vxx[0]
        for o in range(HY_ORDER):
            a = _dft_outer_call(z, mats["outer"], nt)
            cp = _spectral_call(a.reshape(2, n1, DFT_INNER, d_hy), mats["tw"], kf, o, mats["inner"], mats["inner_inv"])
            skip_t = jnp.tile(hy_skip[l, o], DFT_INNER)[None]
            ss_t = jnp.tile(ss[0, o * d_hy:(o + 1) * d_hy], DFT_INNER)[None]
            z = _dft_outer_inv_call(cp.reshape(2 * n1, ncol), mats["outer_inv"], z, vxx[1 + o], skip_t, ss_t, nt)
        hy = z.reshape(n_lat_total, d_hy)
        if ctx_out:
            ktc, ssc = _filter_call(n_ctx, *filt, d_hy)
            hyc = _hyctx_call(p, off_hy, hy_conv_w[l], hy_conv_b[l][None], ktc, ssc, hy_skip[l], ctx_mats,
                              batch=batch, n_lat=n_lat, n_ctx=n_ctx)
            hy = jnp.concatenate([hy, hyc], axis=0)

        n_tok = n_all if ctx_out else n_lat_total
        xt = _merge_call(xt, hs[0], hs[1], p, hy, sg, grp_norm_g[l][None], w_out[l].astype(BF16), mod[2],
                         seg_of_row_tile(TM), n_tok)

        rw = jnp.zeros((d, LANES), F32).at[:, :N_EXPERTS].set(router_w[l])
        rb = jnp.full((1, LANES), -1e30, F32).at[0, :N_EXPERTS].set(router_b[l])
        xt = _moe(xt, norm_ffn_g[l][None], mod[4], mod[3], mod[5], rw, rb, moe_w1[l], moe_b1[l], moe_w2[l], moe_b2[l],
                  seg_of_row_tile, n_tok, n_lat_total, batch)

    return _final_call(xt, final_norm_g[None], n_lat_total).reshape(batch, n_lat, d)
```

```python
import functools
import math

import numpy as np
import jax
import jax.numpy as jnp
from jax import lax
from jax.experimental import pallas as pl
from jax.experimental.pallas import tpu as pltpu

F32 = jnp.float32
BF16 = jnp.bfloat16

EPS = 1e-6
N_MOD = 6
LRU_HEADS = 8
LRU_CONV = 4
LRU_C = 8.0
HY_ORDER = 2
HY_CONV = 3
HY_BANDS = 8
HY_EMB = 1 + 2 * HY_BANDS
HY_EMB_PAD = 32
HY_MIN_DECAY = math.log(1e-2) / 1.5
HY_MAX_DECAY = math.log(1e-2) / 0.3
SG_CHUNK = 128
SG_HEADS = 4
N_EXPERTS = 32
TOP_K = 4
MOE_BLOCK = 256
SWIGLU_LIMIT = 7.0
SWIGLU_ALPHA = 1.702

LANES = 128
SUBLANES = 8
TM = 256
HALO = SUBLANES
DFT_INNER = 128
DISPATCH_TOK = 64
VMEM_LIMIT = 48 * 1024 * 1024


def _cparams(*sem, vmem=VMEM_LIMIT, row_dma=False):
    return pltpu.CompilerParams(dimension_semantics=sem, vmem_limit_bytes=vmem, disable_bounds_checks=row_dma)


def _bdot(a, b):
    return jnp.dot(a, b, preferred_element_type=F32)


def _hdot(a, b):
    return jnp.dot(a, b, preferred_element_type=F32, precision=lax.Precision.HIGHEST)


def _gelu(x):
    return 0.5 * x * (1.0 + jnp.tanh(math.sqrt(2.0 / math.pi) * (x + 0.044715 * (x * x * x))))


def _sigmoid(x):
    return 1.0 / (1.0 + jnp.exp(-x))


def _rms(x):
    return x * lax.rsqrt(jnp.mean(x * x, axis=-1, keepdims=True) + EPS)


def _ada_kernel(s_ref, w_ref, b_ref, o_ref):
    s = s_ref[...]
    s = s * _sigmoid(s)
    o_ref[0] = _bdot(s.astype(BF16), w_ref[0].astype(BF16)) + b_ref[0]


def _ada_call(cond, ada_w, ada_b):
    depth, d, n = ada_w.shape
    tn = 1536
    return pl.pallas_call(
        _ada_kernel,
        grid=(depth, n // tn),
        in_specs=[
            pl.BlockSpec((SUBLANES, d), lambda l, j: (0, 0)),
            pl.BlockSpec((1, d, tn), lambda l, j: (l, 0, j)),
            pl.BlockSpec((1, 1, tn), lambda l, j: (l, 0, j)),
        ],
        out_specs=pl.BlockSpec((1, SUBLANES, tn), lambda l, j: (l, 0, j)),
        out_shape=jax.ShapeDtypeStruct((depth, SUBLANES, n), F32),
        compiler_params=_cparams("arbitrary", "arbitrary"),
        name="ada_mod",
    )(cond, ada_w, ada_b.reshape(depth, 1, n))


def _inproj_kernel(x_ref, g_ref, sc_ref, sh_ref, w_ref, o_ref):
    h = _rms(x_ref[...]) * g_ref[...] * (1.0 + sc_ref[0]) + sh_ref[0]
    o_ref[...] = _bdot(h.astype(BF16), w_ref[...])


def _inproj_call(x, g, sc, sh, w, seg_of_tile):
    t, d = x.shape
    n = w.shape[1]
    tn = n // 2
    return pl.pallas_call(
        _inproj_kernel,
        grid=(2, t // TM),
        in_specs=[
            pl.BlockSpec((TM, d), lambda j, i: (i, 0)),
            pl.BlockSpec((1, d), lambda j, i: (0, 0)),
            pl.BlockSpec((1, 1, d), lambda j, i: (seg_of_tile(i), 0, 0)),
            pl.BlockSpec((1, 1, d), lambda j, i: (seg_of_tile(i), 0, 0)),
            pl.BlockSpec((d, tn), lambda j, i: (0, j)),
        ],
        out_specs=pl.BlockSpec((TM, tn), lambda j, i: (i, j)),
        out_shape=jax.ShapeDtypeStruct((t, n), F32),
        compiler_params=_cparams("arbitrary", "arbitrary"),
        name="in_proj",
    )(x, g, sc, sh, w)


def _lru_kernel(x_ref, xp_ref, xn_ref, cw_ref, cb_ref, wg_ref, bg_ref, lam_ref, o_ref,
                xe_s, a_s, b_s, h_s, *, reverse, n_chunks, d_lru):
    s = pl.program_id(1)
    j = (n_chunks - s) if reverse else (s - 1)
    has_prev = jnp.logical_and(s > 0, j > 0)
    has_next = jnp.logical_and(s > 0, j < n_chunks - 1)

    @pl.when(s == 0)
    def _():
        h_s[...] = jnp.zeros_like(h_s)

    xe_s[0:HALO, :] = jnp.where(has_prev, xp_ref[...], 0.0)
    xe_s[HALO:HALO + TM, :] = x_ref[...]
    xe_s[HALO + TM:HALO + TM + HALO, :] = jnp.where(has_next, xn_ref[...], 0.0)
    left = LRU_CONV // 2
    xc = cb_ref[...]
    for k in range(LRU_CONV):
        xc = xc + cw_ref[k:k + 1, :] * xe_s[pl.ds(HALO + k - left, TM), :]

    g = _bdot(xc.astype(BF16), wg_ref[...]) + bg_ref[...]
    r = _sigmoid(g[:, :d_lru])
    ig = _sigmoid(g[:, d_lru:])
    lam = lam_ref[...]
    sp = jnp.maximum(-lam, 0.0) + jnp.log1p(jnp.exp(-jnp.abs(lam)))
    log_a = (-LRU_C * r) * sp
    a = jnp.exp(log_a)
    a_s[...] = a
    b_s[...] = jnp.sqrt(-jnp.tanh(log_a) * (a * a + 1.0)) * (ig * xc)

    def step(t, h):
        tt = (TM - 1 - t) if reverse else t
        h = a_s[pl.ds(tt, 1), :] * h + b_s[pl.ds(tt, 1), :]
        o_ref[pl.ds(tt, 1), :] = h
        return h

    h_s[0:1, :] = lax.fori_loop(0, TM, step, h_s[0:1, :], unroll=8)


def _lru_call(p, cw, cb, wg, bg, lam, *, reverse, batch, n_lat, n_ctx):
    t = p.shape[0]
    d_lru = cw.shape[1]
    n_chunks = n_lat // TM
    assert n_ctx == TM
    ctx0 = batch * n_chunks
    per = TM // HALO
    last = t // HALO - 1

    def rb(b, s):
        lat = b * n_chunks + ((n_chunks - s) if reverse else (s - 1))
        return jnp.where(s == 0, ctx0 + b, lat)

    kern = functools.partial(_lru_kernel, reverse=reverse, n_chunks=n_chunks, d_lru=d_lru)
    return pl.pallas_call(
        kern,
        grid=(batch, n_chunks + 1),
        in_specs=[
            pl.BlockSpec((TM, d_lru), lambda b, s: (rb(b, s), 0)),
            pl.BlockSpec((HALO, d_lru), lambda b, s: (jnp.maximum(rb(b, s) * per - 1, 0), 0)),
            pl.BlockSpec((HALO, d_lru), lambda b, s: (jnp.minimum((rb(b, s) + 1) * per, last), 0)),
            pl.BlockSpec((LRU_CONV, d_lru), lambda b, s: (0, 0)),
            pl.BlockSpec((1, d_lru), lambda b, s: (0, 0)),
            pl.BlockSpec((d_lru, 2 * d_lru), lambda b, s: (0, 0)),
            pl.BlockSpec((1, 2 * d_lru), lambda b, s: (0, 0)),
            pl.BlockSpec((1, d_lru), lambda b, s: (0, 0)),
        ],
        out_specs=pl.BlockSpec((TM, d_lru), lambda b, s: (rb(b, s), 0)),
        out_shape=jax.ShapeDtypeStruct((t, d_lru), F32),
        scratch_shapes=[
            pltpu.VMEM((TM + 2 * HALO, d_lru), F32),
            pltpu.VMEM((TM, d_lru), F32),
            pltpu.VMEM((TM, d_lru), F32),
            pltpu.VMEM((SUBLANES, d_lru), F32),
        ],
        compiler_params=_cparams("arbitrary", "arbitrary"),
        name="rglru_rev" if reverse else "rglru_fwd",
    )(p, p, p, cw, cb, wg, bg, lam)


def _sg_kernel(u0_ref, u1_ref, v0_ref, v1_ref, lg_ref, lb_ref, ws_ref, bias_ref, o_ref):
    u = _gelu(jnp.concatenate([u0_ref[...], u1_ref[...]], axis=-1))
    v = _gelu(jnp.concatenate([v0_ref[...], v1_ref[...]], axis=-1))
    mu = jnp.mean(v, axis=-1, keepdims=True)
    dv = v - mu
    var = jnp.mean(dv * dv, axis=-1, keepdims=True)
    vb = ((dv * lax.rsqrt(var + EPS)) * lg_ref[...] + lb_ref[...]).astype(BF16)
    hd = vb.shape[1] // SG_HEADS
    for ch in range(TM // SG_CHUNK):
        r0 = ch * SG_CHUNK
        ys = [_bdot(ws_ref[g], vb[r0:r0 + SG_CHUNK, g * hd:(g + 1) * hd]) for g in range(SG_HEADS)]
        y = jnp.concatenate(ys, axis=-1) + bias_ref[...]
        o_ref[r0:r0 + SG_CHUNK, :] = u[r0:r0 + SG_CHUNK, :] * y


def _sg_call(p, off_sg, lg, lb, ws, bias):
    t = p.shape[0]
    d_sg = lg.shape[1]
    half = d_sg // 2
    c0 = off_sg // half
    specs = [pl.BlockSpec((TM, half), (lambda i, c=c0 + k: (i, c))) for k in range(4)]
    return pl.pallas_call(
        _sg_kernel,
        grid=(t // TM,),
        in_specs=specs + [
            pl.BlockSpec((1, d_sg), lambda i: (0, 0)),
            pl.BlockSpec((1, d_sg), lambda i: (0, 0)),
            pl.BlockSpec((SG_HEADS, SG_CHUNK, SG_CHUNK), lambda i: (0, 0, 0)),
            pl.BlockSpec((SG_CHUNK, d_sg), lambda i: (0, 0)),
        ],
        out_specs=pl.BlockSpec((TM, d_sg), lambda i: (i, 0)),
        out_shape=jax.ShapeDtypeStruct((t, d_sg), F32),
        compiler_params=_cparams("arbitrary"),
        name="spatial_gating",
    )(p, p, p, p, lg, lb, ws, bias)


def _hyconv_kernel(x_ref, xp_ref, xn_ref, cw_ref, cb_ref, o_ref, xe_s, *, n_chunks):
    j = pl.program_id(1) % n_chunks
    xe_s[0:HALO, :] = jnp.where(j > 0, xp_ref[...], 0.0)
    xe_s[HALO:HALO + TM, :] = x_ref[...]
    xe_s[HALO + TM:HALO + TM + HALO, :] = jnp.where(j < n_chunks - 1, xn_ref[...], 0.0)
    left = HY_CONV // 2
    y = cb_ref[...]
    for k in range(HY_CONV):
        y = y + cw_ref[k:k + 1, :] * xe_s[pl.ds(HALO + k - left, TM), :]
    o_ref[0] = y


def _hyconv_call(p, off_hy, cw, cb, *, batch, n_lat):
    t = p.shape[0]
    d_hy = cw.shape[1] // 3
    n_chunks = n_lat // TM
    c0 = off_hy // d_hy
    per = TM // HALO
    last = t // HALO - 1
    kern = functools.partial(_hyconv_kernel, n_chunks=n_chunks)
    return pl.pallas_call(
        kern,
        grid=(3, batch * n_chunks),
        in_specs=[
            pl.BlockSpec((TM, d_hy), lambda c, i: (i, c0 + c)),
            pl.BlockSpec((HALO, d_hy), lambda c, i: (jnp.maximum(i * per - 1, 0), c0 + c)),
            pl.BlockSpec((HALO, d_hy), lambda c, i: (jnp.minimum((i + 1) * per, last), c0 + c)),
            pl.BlockSpec((HY_CONV, d_hy), lambda c, i: (0, c)),
            pl.BlockSpec((1, d_hy), lambda c, i: (0, c)),
        ],
        out_specs=pl.BlockSpec((1, TM, d_hy), lambda c, i: (c, i, 0)),
        out_shape=jax.ShapeDtypeStruct((3, batch * n_lat, d_hy), F32),
        scratch_shapes=[pltpu.VMEM((TM + 2 * HALO, d_hy), F32)],
        compiler_params=_cparams("arbitrary", "arbitrary"),
        name="hyena_shortconv",
    )(p, p, p, cw, cb)


def _filter_kernel(z_ref, z0_ref, w1_ref, b1_ref, f1_ref, w2_ref, b2_ref, f2_ref, w3_ref, b3_ref,
                   w3b_ref, b3b_ref, dl_ref, kt_ref, ss_ref, *, length, rows):
    i = pl.program_id(0)

    def mlp(z, w3, b3):
        h = jnp.sin(f1_ref[...] * (_hdot(z, w1_ref[...]) + b1_ref[...]))
        h = jnp.sin(f2_ref[...] * (_hdot(h, w2_ref[...]) + b2_ref[...]))
        return _hdot(h, w3) + b3

    z = z_ref[...]
    win = jnp.exp(-z[:, 0:1] * dl_ref[...])
    k = mlp(z, w3_ref[...], b3_ref[...]) * jnp.concatenate([win] * HY_ORDER, axis=-1)
    lag0_back = mlp(z0_ref[...], w3b_ref[...], b3b_ref[...])[0:1, :]
    n = i * rows + lax.broadcasted_iota(jnp.int32, (rows, 1), 0)
    k = k + jnp.where(n == 0, lag0_back, 0.0)
    k = jnp.where(n == length, 0.0, k)
    kt_ref[...] = k

    @pl.when(i == 0)
    def _():
        ss_ref[...] = jnp.zeros_like(ss_ref)

    ss_ref[0:1, :] += jnp.sum(k * k, axis=0, keepdims=True)


def _filter_features(length):
    n = np.arange(2 * length)
    j = np.where(n < length, n, 2 * length - n).astype(np.float64)
    t = j / (length - 1)
    w = (2.0 * math.pi / length) * j
    f = np.linspace(1e-4, HY_BANDS - 1, HY_BANDS)
    z = np.zeros((2 * length, HY_EMB_PAD), np.float64)
    z[:, 0] = t
    z[:, 1:1 + HY_BANDS] = np.cos(w[:, None] * f[None, :])
    z[:, 1 + HY_BANDS:HY_EMB] = -np.sin(w[:, None] * f[None, :])
    return jnp.asarray(z, F32)


def _filter_call(length, w1, b1, f1, w2, b2, f2, w3, b3, d_hy):
    rows = TM
    hid = w1.shape[1]
    nc = HY_ORDER * d_hy
    ztab = _filter_features(length)
    w1p = jnp.zeros((HY_EMB_PAD, hid), F32).at[:HY_EMB].set(w1)
    deltas = jnp.asarray(np.abs(np.linspace(HY_MIN_DECAY, HY_MAX_DECAY, d_hy))[None, :], F32)
    n_fwd = length // rows
    kern = functools.partial(_filter_kernel, length=length, rows=rows)
    full = lambda shape: pl.BlockSpec(shape, lambda i: (0,) * len(shape))
    return pl.pallas_call(
        kern,
        grid=(2 * length // rows,),
        in_specs=[
            pl.BlockSpec((rows, HY_EMB_PAD), lambda i: (i, 0)),
            pl.BlockSpec((SUBLANES, HY_EMB_PAD), lambda i: (0, 0)),
            full((HY_EMB_PAD, hid)), full((1, hid)), full((1, hid)),
            full((hid, hid)), full((1, hid)), full((1, hid)),
            pl.BlockSpec((hid, nc), lambda i: (0, (i >= n_fwd).astype(jnp.int32))),
            pl.BlockSpec((1, nc), lambda i: (0, (i >= n_fwd).astype(jnp.int32))),
            pl.BlockSpec((hid, nc), lambda i: (0, 1)),
            pl.BlockSpec((1, nc), lambda i: (0, 1)),
            full((1, d_hy)),
        ],
        out_specs=[
            pl.BlockSpec((rows, nc), lambda i: (i, 0)),
            pl.BlockSpec((SUBLANES, nc), lambda i: (0, 0)),
        ],
        out_shape=[
            jax.ShapeDtypeStruct((2 * length, nc), F32),
            jax.ShapeDtypeStruct((SUBLANES, nc), F32),
        ],
        compiler_params=_cparams("arbitrary"),
        name="hyena_filter",
    )(ztab, ztab, w1p, b1[None], f1[None], w2, b2[None], f2[None], w3, b3[None], w3, b3[None], deltas)


def _dft_mats(length):
    n_fft = 2 * length
    n1 = n_fft // DFT_INNER
    n1h = n1 // 2
    k = np.arange(n1)
    th = 2.0 * math.pi * np.outer(k, np.arange(n1h)) / n1
    c, s = np.cos(th), np.sin(th)
    w_outer = np.block([[c, s], [-s, c]])
    th = 2.0 * math.pi * np.outer(k, k) / n1
    w_outer_real = np.concatenate([np.cos(th), -np.sin(th)], axis=0)
    m = np.arange(DFT_INNER)
    th = 2.0 * math.pi * np.outer(m, m) / DFT_INNER
    c, s = np.cos(th), np.sin(th)
    w_inner = np.block([[c, s], [-s, c]])
    w_inner_inv = np.block([[c, -s], [s, c]])
    th = 2.0 * math.pi * np.outer(np.arange(n1h), k) / n1
    c, s = np.cos(th) / n_fft, np.sin(th) / n_fft
    w_outer_inv = np.block([[c, -s], [s, c]])
    th = 2.0 * math.pi * np.outer(k, m) / n_fft
    tw = np.stack([np.cos(th), -np.sin(th)], axis=0)
    as_bf = lambda a: jnp.asarray(a, F32).astype(BF16)
    tw_b = jnp.broadcast_to(jnp.asarray(tw, F32)[..., None], (2, n1, DFT_INNER, LANES))
    return dict(outer=as_bf(w_outer), outer_real=as_bf(w_outer_real), inner=as_bf(w_inner),
                inner_inv=as_bf(w_inner_inv), outer_inv=as_bf(w_outer_inv), tw=tw_b, n1=n1)


def _dft_outer_kernel(x_ref, w_ref, o_ref):
    x = x_ref[...]
    if x.ndim == 3:
        x = x.reshape(x.shape[0] * x.shape[1], x.shape[2])
    o_ref[...] = _bdot(w_ref[...], x.astype(BF16))


def _dft_outer_call(x, w, nt):
    ncol = x.shape[-1]
    m, r = w.shape
    if x.ndim == 3:
        xspec = pl.BlockSpec((x.shape[0], x.shape[1], nt), lambda i: (0, 0, i))
    else:
        xspec = pl.BlockSpec((x.shape[0], nt), lambda i: (0, i))
    return pl.pallas_call(
        _dft_outer_kernel,
        grid=(ncol // nt,),
        in_specs=[xspec, pl.BlockSpec((m, r), lambda i: (0, 0))],
        out_specs=pl.BlockSpec((m, nt), lambda i: (0, i)),
        out_shape=jax.ShapeDtypeStruct((m, ncol), F32),
        compiler_params=_cparams("arbitrary"),
        name="hyena_dft_outer",
    )(x, w)


def _twiddled(a_ref, tw_ref, reps):
    ar, ai = a_ref[0, 0], a_ref[1, 0]
    twr = jnp.concatenate([tw_ref[0, 0]] * reps, axis=-1)
    twi = jnp.concatenate([tw_ref[1, 0]] * reps, axis=-1)
    return ar * twr - ai * twi, ar * twi + ai * twr, twr, twi


def _filter_spectrum_kernel(a_ref, tw_ref, w_ref, o_ref, *, reps):
    br, bi, _, _ = _twiddled(a_ref, tw_ref, reps)
    x = _bdot(w_ref[...], jnp.concatenate([br, bi], axis=0).astype(BF16))
    o_ref[0, 0] = x[:DFT_INNER]
    o_ref[1, 0] = x[DFT_INNER:]


def _filter_spectrum_call(a, tw, w_inner):
    _, n1, _, c = a.shape
    blk = pl.BlockSpec((2, 1, DFT_INNER, c), lambda k: (0, k, 0, 0))
    return pl.pallas_call(
        functools.partial(_filter_spectrum_kernel, reps=c // LANES),
        grid=(n1,),
        in_specs=[blk, pl.BlockSpec((2, 1, DFT_INNER, LANES), lambda k: (0, k, 0, 0)),
                  pl.BlockSpec((2 * DFT_INNER, 2 * DFT_INNER), lambda k: (0, 0))],
        out_specs=blk,
        out_shape=jax.ShapeDtypeStruct(a.shape, F32),
        compiler_params=_cparams("arbitrary"),
        name="hyena_filter_spectrum",
    )(a, tw, w_inner)


def _spectral_kernel(a_ref, tw_ref, kf_ref, w_ref, wi_ref, o_ref, *, reps):
    br, bi, twr, twi = _twiddled(a_ref, tw_ref, reps)
    x = _bdot(w_ref[...], jnp.concatenate([br, bi], axis=0).astype(BF16))
    xr, xi = x[:DFT_INNER], x[DFT_INNER:]
    kr, ki = kf_ref[0, 0], kf_ref[1, 0]
    yr = xr * kr - xi * ki
    yi = xr * ki + xi * kr
    c = _bdot(wi_ref[...], jnp.concatenate([yr, yi], axis=0).astype(BF16))
    cr, ci = c[:DFT_INNER], c[DFT_INNER:]
    o_ref[0, 0] = cr * twr + ci * twi
    o_ref[1, 0] = ci * twr - cr * twi


def _spectral_call(a, tw, kf, order, w_inner, w_inner_inv):
    _, n1, _, c = a.shape
    blk = pl.BlockSpec((2, 1, DFT_INNER, c), lambda k: (0, k, 0, 0))
    sq = pl.BlockSpec((2 * DFT_INNER, 2 * DFT_INNER), lambda k: (0, 0))
    return pl.pallas_call(
        functools.partial(_spectral_kernel, reps=c // LANES),
        grid=(n1,),
        in_specs=[blk, pl.BlockSpec((2, 1, DFT_INNER, LANES), lambda k: (0, k, 0, 0)),
                  pl.BlockSpec((2, 1, DFT_INNER, c), lambda k: (0, k, 0, order)), sq, sq],
        out_specs=blk,
        out_shape=jax.ShapeDtypeStruct(a.shape, F32),
        compiler_params=_cparams("arbitrary"),
        name="hyena_spectral",
    )(a, tw, kf, w_inner, w_inner_inv)


def _dft_outer_inv_kernel(c_ref, w_ref, z_ref, gate_ref, skip_ref, ss_ref, o_ref):
    y = _bdot(w_ref[...], c_ref[...].astype(BF16))
    y = y.reshape(z_ref.shape)
    scale = lax.rsqrt(ss_ref[...] + EPS)
    o_ref[...] = gate_ref[...] * (scale * y + skip_ref[...] * z_ref[...])


def _dft_outer_inv_call(cp, w, z, gate, skip_t, ss_t, nt):
    b, n1h, ncol = z.shape
    m, r = w.shape
    zspec = pl.BlockSpec((b, n1h, nt), lambda i: (0, 0, i))
    vspec = pl.BlockSpec((1, nt), lambda i: (0, i))
    return pl.pallas_call(
        _dft_outer_inv_kernel,
        grid=(ncol // nt,),
        in_specs=[pl.BlockSpec((r, nt), lambda i: (0, i)), pl.BlockSpec((m, r), lambda i: (0, 0)),
                  zspec, zspec, vspec, vspec],
        out_specs=zspec,
        out_shape=jax.ShapeDtypeStruct(z.shape, F32),
        compiler_params=_cparams("arbitrary"),
        name="hyena_dft_outer_inv",
    )(cp, w, z, gate, skip_t, ss_t)


def _ctx_dft_mats(n_ctx):
    n_fft = 2 * n_ctx
    f = np.arange(n_fft)
    th = 2.0 * math.pi * np.outer(f, np.arange(n_ctx)) / n_fft
    fz = np.concatenate([np.cos(th), -np.sin(th)], axis=0)
    th = 2.0 * math.pi * np.outer(f, f) / n_fft
    fk = np.concatenate([np.cos(th), -np.sin(th)], axis=0)
    th = 2.0 * math.pi * np.outer(np.arange(n_ctx), f) / n_fft
    fi = np.concatenate([np.cos(th), -np.sin(th)], axis=1) / n_fft
    as_bf = lambda a: jnp.asarray(a, F32).astype(BF16)
    return as_bf(fz), as_bf(fk), as_bf(fi)


def _hyctx_kernel(p0_ref, p1_ref, p2_ref, cw_ref, cb_ref, kt_ref, ss_ref, skip_ref,
                  fz_ref, fk_ref, fi_ref, o_ref, xe_s, *, d_hy):
    n_fft = fk_ref.shape[1]
    left = HY_CONV // 2

    def conv(p_ref, c):
        xe_s[0:HALO, :] = jnp.zeros((HALO, d_hy), F32)
        xe_s[HALO:HALO + TM, :] = p_ref[...]
        xe_s[HALO + TM:HALO + TM + HALO, :] = jnp.zeros((HALO, d_hy), F32)
        y = cb_ref[:, c * d_hy:(c + 1) * d_hy]
        for k in range(HY_CONV):
            y = y + cw_ref[k:k + 1, c * d_hy:(c + 1) * d_hy] * xe_s[pl.ds(HALO + k - left, TM), :]
        return y

    z = conv(p0_ref, 0)
    gates = (conv(p1_ref, 1), conv(p2_ref, 2))
    for o in range(HY_ORDER):
        cols = slice(o * d_hy, (o + 1) * d_hy)
        kf = _bdot(fk_ref[...], kt_ref[:, cols].astype(BF16))
        zf = _bdot(fz_ref[...], z.astype(BF16))
        zr, zi = zf[:n_fft], zf[n_fft:]
        kr, ki = kf[:n_fft], kf[n_fft:]
        yf = jnp.concatenate([zr * kr - zi * ki, zr * ki + zi * kr], axis=0)
        y = _bdot(fi_ref[...], yf.astype(BF16))
        scale = lax.rsqrt(ss_ref[0:1, cols] + EPS)
        z = gates[o] * (scale * y + skip_ref[o:o + 1, :] * z)
    o_ref[...] = z


def _hyctx_call(p, off_hy, cw, cb, kt, ss, skip, mats, *, batch, n_lat, n_ctx):
    d_hy = skip.shape[1]
    assert n_ctx == TM
    c0 = off_hy // d_hy
    r0 = batch * n_lat // TM
    fz, fk, fi = mats
    full = lambda a: pl.BlockSpec(a.shape, lambda b: (0,) * a.ndim)
    return pl.pallas_call(
        functools.partial(_hyctx_kernel, d_hy=d_hy),
        grid=(batch,),
        in_specs=[pl.BlockSpec((TM, d_hy), (lambda b, c=c0 + k: (r0 + b, c))) for k in range(3)]
        + [full(cw), full(cb), full(kt), full(ss), full(skip), full(fz), full(fk), full(fi)],
        out_specs=pl.BlockSpec((TM, d_hy), lambda b: (b, 0)),
        out_shape=jax.ShapeDtypeStruct((batch * n_ctx, d_hy), F32),
        scratch_shapes=[pltpu.VMEM((TM + 2 * HALO, d_hy), F32)],
        compiler_params=_cparams("arbitrary"),
        name="hyena_ctx",
    )(p, p, p, cw, cb, kt, ss, skip, fz, fk, fi)


def _merge_kernel(x_ref, hf_ref, hb_ref, pg_ref, hy_ref, sg_ref, gg_ref, w_ref, gate_ref, o_ref):
    a = (hf_ref[...] + hb_ref[...]) * _gelu(pg_ref[...])
    y = jnp.concatenate([_rms(a), _rms(hy_ref[...]), _rms(sg_ref[...])], axis=-1) * gg_ref[...]
    o_ref[...] = x_ref[...] + gate_ref[0] * _bdot(y.astype(BF16), w_ref[...])


def _merge_call(x, hf, hb, p, hy, sg, gg, w, gate, seg_of_tile, n_rows):
    d = x.shape[1]
    d_lru, d_hy, d_sg = hf.shape[1], hy.shape[1], sg.shape[1]
    row = lambda width, col=0: pl.BlockSpec((TM, width), lambda i: (i, col))
    return pl.pallas_call(
        _merge_kernel,
        grid=(n_rows // TM,),
        in_specs=[row(d), row(d_lru), row(d_lru), row(d_lru, 1), row(d_hy), row(d_sg),
                  pl.BlockSpec((1, d), lambda i: (0, 0)),
                  pl.BlockSpec((d, d), lambda i: (0, 0)),
                  pl.BlockSpec((1, 1, d), lambda i: (seg_of_tile(i), 0, 0))],
        out_specs=row(d),
        out_shape=jax.ShapeDtypeStruct((n_rows, d), F32),
        compiler_params=_cparams("arbitrary"),
        name="merge_out_proj",
    )(x, hf, hb, p, hy, sg, gg, w, gate)


def _router_kernel(x_ref, g_ref, sc_ref, sh_ref, rw_ref, rb_ref, f_ref, e_ref, gt_ref, m_ref):
    f = _rms(x_ref[...]) * g_ref[...] * (1.0 + sc_ref[0]) + sh_ref[0]
    f_ref[...] = f
    logits = _hdot(f, rw_ref[...]) + rb_ref[...]
    lane = lax.broadcasted_iota(jnp.int32, logits.shape, 1).astype(F32)
    work = logits
    e_out = jnp.zeros_like(logits)
    mask = jnp.zeros_like(logits)
    vals = []
    for k in range(TOP_K):
        mx = jnp.max(work, axis=-1, keepdims=True)
        idx = jnp.min(jnp.where(work == mx, lane, float(LANES)), axis=-1, keepdims=True)
        sel = lane == idx
        e_out = jnp.where(lane == float(k), idx, e_out)
        mask = jnp.where(sel, 1.0, mask)
        work = jnp.where(sel, -jnp.inf, work)
        vals.append(mx)
    ex = [jnp.exp(v - vals[0]) for v in vals]
    den = ex[0]
    for v in ex[1:]:
        den = den + v
    g_out = jnp.zeros_like(logits)
    for k in range(TOP_K):
        g_out = jnp.where(lane == float(k), ex[k] / den, g_out)
    e_ref[...] = e_out.astype(jnp.int32)
    gt_ref[...] = g_out
    m_ref[...] = mask


def _router_call(x, g, sc, sh, rw, rb, seg_of_tile, n_rows):
    d = x.shape[1]
    row = lambda width: pl.BlockSpec((TM, width), lambda i: (i, 0))
    mod = pl.BlockSpec((1, 1, d), lambda i: (seg_of_tile(i), 0, 0))
    return pl.pallas_call(
        _router_kernel,
        grid=(n_rows // TM,),
        in_specs=[row(d), pl.BlockSpec((1, d), lambda i: (0, 0)), mod, mod,
                  pl.BlockSpec((d, LANES), lambda i: (0, 0)), pl.BlockSpec((1, LANES), lambda i: (0, 0))],
        out_specs=[row(d), row(LANES), row(LANES), row(LANES)],
        out_shape=[jax.ShapeDtypeStruct((n_rows, d), F32), jax.ShapeDtypeStruct((n_rows, LANES), jnp.int32),
                   jax.ShapeDtypeStruct((n_rows, LANES), F32), jax.ShapeDtypeStruct((n_rows, LANES), F32)],
        compiler_params=_cparams("arbitrary"),
        name="moe_router",
    )(x, g, sc, sh, rw, rb)


def _rank_kernel(m_ref, tri_ref, pos_ref, cnt_ref, carry_s):
    i = pl.program_id(0)

    @pl.when(i == 0)
    def _():
        carry_s[...] = jnp.zeros_like(carry_s)

    m = m_ref[...]
    pos_ref[...] = _bdot(tri_ref[...], m.astype(BF16)) + carry_s[0:1, :]
    carry_s[0:1, :] += jnp.sum(m, axis=0, keepdims=True)
    cnt_ref[...] = jnp.broadcast_to(carry_s[0:1, :], cnt_ref.shape)


def _rank_call(mask):
    n = mask.shape[0]
    tri = jnp.asarray(np.tril(np.ones((TM, TM)), -1), BF16)
    return pl.pallas_call(
        _rank_kernel,
        grid=(n // TM,),
        in_specs=[pl.BlockSpec((TM, LANES), lambda i: (i, 0)), pl.BlockSpec((TM, TM), lambda i: (0, 0))],
        out_specs=[pl.BlockSpec((TM, LANES), lambda i: (i, 0)), pl.BlockSpec((SUBLANES, LANES), lambda i: (0, 0))],
        out_shape=[jax.ShapeDtypeStruct((n, LANES), F32), jax.ShapeDtypeStruct((SUBLANES, LANES), F32)],
        scratch_shapes=[pltpu.VMEM((SUBLANES, LANES), F32)],
        compiler_params=_cparams("arbitrary"),
        name="moe_rank",
    )(mask, tri)


def _slot_kernel(e_ref, pos_ref, start_ref, o_ref):
    sd = pos_ref[...] + start_ref[...]
    e = e_ref[...]
    lane = lax.broadcasted_iota(jnp.int32, sd.shape, 1)
    out = jnp.zeros_like(sd)
    for k in range(TOP_K):
        sk = jnp.sum(jnp.where(lane == e[:, k:k + 1], sd, 0.0), axis=-1, keepdims=True)
        out = jnp.where(lane == k, sk, out)
    o_ref[...] = out.astype(jnp.int32)


def _slot_call(e4, pos, start):
    n = e4.shape[0]
    row = pl.BlockSpec((TM, LANES), lambda i: (i, 0))
    return pl.pallas_call(
        _slot_kernel,
        grid=(n // TM,),
        in_specs=[row, row, pl.BlockSpec((1, LANES), lambda i: (0, 0))],
        out_specs=row,
        out_shape=jax.ShapeDtypeStruct((n, LANES), jnp.int32),
        compiler_params=_cparams("arbitrary"),
        name="moe_slot",
    )(e4, pos, start)


def _row_copy(src, dst, src_row, dst_row, sem):
    return pltpu.make_async_copy(src.at[pl.ds(src_row, 1)], dst.at[pl.ds(dst_row, 1)], sem)


def _drain(copy, count):
    group = 16

    def body(r, c):
        for _ in range(group):
            copy.wait()
        return c

    lax.fori_loop(0, count // group, body, 0)


def _dispatch_kernel(slot_ref, f_ref, init_hbm, xs_hbm, sem):
    del init_hbm
    i = pl.program_id(0)

    def issue(tl, c):
        t = i * TM + tl
        for k in range(TOP_K):
            _row_copy(f_ref, xs_hbm, tl, slot_ref[t * TOP_K + k], sem).start()
        return c

    lax.fori_loop(0, TM, issue, 0)
    _drain(_row_copy(f_ref, xs_hbm, 0, 0, sem), TM * TOP_K)


def _dispatch_call(slot_flat, f, n_rows):
    n, d = f.shape
    return pl.pallas_call(
        _dispatch_kernel,
        grid_spec=pltpu.PrefetchScalarGridSpec(
            num_scalar_prefetch=1,
            grid=(n // TM,),
            in_specs=[pl.BlockSpec((TM, d), lambda i, s: (i, 0)), pl.BlockSpec(memory_space=pl.ANY)],
            out_specs=pl.BlockSpec(memory_space=pl.ANY),
            scratch_shapes=[pltpu.SemaphoreType.DMA(())],
        ),
        out_shape=jax.ShapeDtypeStruct((n_rows, d), F32),
        input_output_aliases={2: 0},
        compiler_params=_cparams("arbitrary", row_dma=True),
        name="moe_dispatch",
    )(slot_flat, f, jnp.zeros((n_rows, d), F32))


def _expert_kernel(be_ref, nu_ref, x_ref, w1_ref, b1_ref, w2_ref, b2_ref, o_ref):
    i = pl.program_id(0)
    d_e = w2_ref.shape[2]

    @pl.when(i < nu_ref[0])
    def _():
        gu = _bdot(x_ref[...].astype(BF16), w1_ref[0, 0].astype(BF16)) + b1_ref[0, 0]
        glu = jnp.minimum(gu[:, :d_e], SWIGLU_LIMIT)
        lin = jnp.clip(gu[:, d_e:], -SWIGLU_LIMIT, SWIGLU_LIMIT)
        act = glu * _sigmoid(SWIGLU_ALPHA * glu) * (lin + 1.0)
        o_ref[...] = _bdot(act.astype(BF16), w2_ref[0, 0].astype(BF16)) + b2_ref[0, 0]

    @pl.when(i >= nu_ref[0])
    def _():
        o_ref[...] = jnp.zeros_like(o_ref)


def _expert_call(block_e, n_used, xs, w1, b1, w2, b2, layer):
    n_rows, d = xs.shape
    depth, n_e, _, d_gu = w1.shape
    d_e = w2.shape[2]
    return pl.pallas_call(
        _expert_kernel,
        grid_spec=pltpu.PrefetchScalarGridSpec(
            num_scalar_prefetch=2,
            grid=(n_rows // MOE_BLOCK,),
            in_specs=[
                pl.BlockSpec((MOE_BLOCK, d), lambda i, be, nu: (i, 0)),
                pl.BlockSpec((1, 1, d, d_gu), lambda i, be, nu: (layer, be[i], 0, 0)),
                pl.BlockSpec((1, 1, 1, d_gu), lambda i, be, nu: (layer, be[i], 0, 0)),
                pl.BlockSpec((1, 1, d_e, d), lambda i, be, nu: (layer, be[i], 0, 0)),
                pl.BlockSpec((1, 1, 1, d), lambda i, be, nu: (layer, be[i], 0, 0)),
            ],
            out_specs=pl.BlockSpec((MOE_BLOCK, d), lambda i, be, nu: (i, 0)),
        ),
        out_shape=jax.ShapeDtypeStruct((n_rows, d), F32),
        compiler_params=_cparams("arbitrary"),
        name="moe_experts",
    )(block_e, n_used, xs, w1, b1.reshape(depth, n_e, 1, d_gu), w2, b2.reshape(depth, n_e, 1, d))


def _combine_kernel(slot_ref, x_ref, g_ref, gate_ref, ys_hbm, o_ref, buf, sem):
    i = pl.program_id(0)
    n = pl.num_programs(0)
    per_step = DISPATCH_TOK * TOP_K

    def issue(step, b):
        def body(tl, c):
            t = step * DISPATCH_TOK + tl
            for k in range(TOP_K):
                _row_copy(ys_hbm, buf.at[b], slot_ref[t * TOP_K + k], k * DISPATCH_TOK + tl, sem.at[b]).start()
            return c
        lax.fori_loop(0, DISPATCH_TOK, body, 0)

    @pl.when(i == 0)
    def _():
        issue(0, 0)

    @pl.when(i + 1 < n)
    def _():
        issue(i + 1, (i + 1) % 2)

    cur = i % 2

    _drain(_row_copy(ys_hbm, buf.at[cur], 0, 0, sem.at[cur]), per_step)
    g = g_ref[...]
    acc = jnp.zeros(o_ref.shape, F32)
    for k in range(TOP_K):
        acc = acc + g[:, k:k + 1] * buf[cur, pl.ds(k * DISPATCH_TOK, DISPATCH_TOK), :]
    o_ref[...] = x_ref[...] + gate_ref[0] * acc


def _combine_call(slot_flat, x, g4, gate, ys, seg_of_tile, n_tok):
    d = x.shape[1]
    return pl.pallas_call(
        _combine_kernel,
        grid_spec=pltpu.PrefetchScalarGridSpec(
            num_scalar_prefetch=1,
            grid=(n_tok // DISPATCH_TOK,),
            in_specs=[
                pl.BlockSpec((DISPATCH_TOK, d), lambda i, s: (i, 0)),
                pl.BlockSpec((DISPATCH_TOK, LANES), lambda i, s: (i, 0)),
                pl.BlockSpec((1, 1, d), lambda i, s: (seg_of_tile(i), 0, 0)),
                pl.BlockSpec(memory_space=pl.ANY),
            ],
            out_specs=pl.BlockSpec((DISPATCH_TOK, d), lambda i, s: (i, 0)),
            scratch_shapes=[pltpu.VMEM((2, DISPATCH_TOK * TOP_K, d), F32), pltpu.SemaphoreType.DMA((2,))],
        ),
        out_shape=jax.ShapeDtypeStruct((n_tok, d), F32),
        compiler_params=_cparams("arbitrary", row_dma=True),
        name="moe_combine",
    )(slot_flat, x, g4, gate, ys)


def _moe(x, norm_g, sc, sh, gate, rw, rb, w1, b1, w2, b2, layer, seg_of_row_tile, n_tok):
    f, e4, g4, mask = _router_call(x, norm_g, sc, sh, rw, rb, seg_of_row_tile(TM), n_tok)
    pos, cnt = _rank_call(mask)
    counts = cnt[0, :N_EXPERTS].astype(jnp.int32)
    padded = (counts + MOE_BLOCK - 1) // MOE_BLOCK * MOE_BLOCK
    pend = jnp.cumsum(padded)
    pstart = pend - padded
    n_blocks = -(-(n_tok * TOP_K + N_EXPERTS * (MOE_BLOCK - 1)) // MOE_BLOCK)
    first_row = jnp.arange(n_blocks, dtype=jnp.int32) * MOE_BLOCK
    block_e = jnp.minimum(jnp.sum((pend[None, :] <= first_row[:, None]).astype(jnp.int32), axis=1), N_EXPERTS - 1)
    n_used = (pend[-1:] // MOE_BLOCK).astype(jnp.int32)
    start = jnp.zeros((1, LANES), F32).at[0, :N_EXPERTS].set(pstart.astype(F32))
    slot = _slot_call(e4, pos, start)
    slot_flat = slot[:, :TOP_K].reshape(-1)
    xs = _dispatch_call(slot_flat, f, n_blocks * MOE_BLOCK)
    ys = _expert_call(block_e, n_used, xs, w1, b1, w2, b2, layer)
    return _combine_call(slot_flat, x, g4, gate, ys, seg_of_row_tile(DISPATCH_TOK), n_tok)


def _final_kernel(x_ref, g_ref, o_ref):
    o_ref[...] = _rms(x_ref[...]) * g_ref[...]


def _final_call(x, g, n_rows):
    d = x.shape[1]
    return pl.pallas_call(
        _final_kernel,
        grid=(n_rows // TM,),
        in_specs=[pl.BlockSpec((TM, d), lambda i: (i, 0)), pl.BlockSpec((1, d), lambda i: (0, 0))],
        out_specs=pl.BlockSpec((TM, d), lambda i: (i, 0)),
        out_shape=jax.ShapeDtypeStruct((n_rows, d), F32),
        compiler_params=_cparams("arbitrary"),
        name="final_norm",
    )(x, g)


def _blockdiag(w):
    h, hd, _ = w.shape
    return jnp.einsum('hij,hg->higj', w, jnp.eye(h, dtype=w.dtype)).reshape(h * hd, h * hd)


def kernel(x, c, ctx, c_ctx, ada_w, ada_b, norm_mix_g, w_in, lru_conv_w, lru_conv_b, lru_wa, lru_ba, lru_wx, lru_bx, lru_lam, hy_conv_w, hy_conv_b, hf_w1, hf_b1, hf_f1, hf_w2, hf_b2, hf_f2, hf_w3, hf_b3, hy_skip, sg_ln_g, sg_ln_b, sg_w, sg_b, grp_norm_g, w_out, norm_ffn_g, router_w, router_b, moe_w1, moe_b1, moe_w2, moe_b2, final_norm_g):
    batch, n_lat, d = x.shape
    n_ctx = ctx.shape[1]
    depth = ada_w.shape[0]
    d_lru = lru_conv_w.shape[2]
    d_hy = hy_skip.shape[2]
    d_sg = sg_ln_g.shape[1]
    off_hy = 2 * d_lru
    off_sg = off_hy + 3 * d_hy
    assert batch == 2 and n_ctx == TM and n_lat % (TM * 2) == 0 and batch + 1 <= SUBLANES
    n_lat_total = batch * n_lat
    n_all = n_lat_total + batch * n_ctx

    def seg_of_row_tile(rows):
        per_seq = n_lat // rows
        return lambda i: jnp.minimum(i // per_seq, batch)

    cond = jnp.zeros((SUBLANES, d), F32).at[:batch].set(c).at[batch].set(c_ctx)
    mods = _ada_call(cond, ada_w, ada_b)

    mats = _dft_mats(n_lat)
    ctx_mats = _ctx_dft_mats(n_ctx)
    n1 = mats["n1"]
    ncol = DFT_INNER * d_hy
    nt = 4096
    sg_bias_shape = (SG_CHUNK, d_sg)

    xt = jnp.concatenate([x.reshape(n_lat_total, d), ctx.reshape(batch * n_ctx, d)], axis=0)
    for l in range(depth):
        ctx_out = l < depth - 1
        m = mods[l, :batch + 1].reshape(batch + 1, N_MOD, d)
        mod = [m[:, j][:, None, :] for j in range(N_MOD)]

        p = _inproj_call(xt, norm_mix_g[l][None], mod[1], mod[0], w_in[l].astype(BF16), seg_of_row_tile(TM))

        hs = []
        for dr, rev in enumerate((False, True)):
            wg = jnp.concatenate([_blockdiag(lru_wa[l, dr]), _blockdiag(lru_wx[l, dr])], axis=1).astype(BF16)
            bg = jnp.concatenate([lru_ba[l, dr], lru_bx[l, dr]])[None]
            hs.append(_lru_call(p, lru_conv_w[l], lru_conv_b[l][None], wg, bg, lru_lam[l, dr][None],
                                reverse=rev, batch=batch, n_lat=n_lat, n_ctx=n_ctx))

        sg_bias = jnp.broadcast_to(sg_b[l].T[:, :, None], (SG_CHUNK, SG_HEADS, d_sg // SG_HEADS)).reshape(sg_bias_shape)
        sg = _sg_call(p, off_sg, sg_ln_g[l][None], sg_ln_b[l][None], sg_w[l].astype(BF16), sg_bias)

        filt = (hf_w1[l], hf_b1[l], hf_f1[l], hf_w2[l], hf_b2[l], hf_f2[l], hf_w3[l], hf_b3[l])
        kt, ss = _filter_call(n_lat, *filt, d_hy)
        ka = _dft_outer_call(kt.reshape(n1, DFT_INNER * HY_ORDER * d_hy), mats["outer_real"], nt)
        kf = _filter_spectrum_call(ka.reshape(2, n1, DFT_INNER, HY_ORDER * d_hy), mats["tw"], mats["inner"])
        vxx = _hyconv_call(p, off_hy, hy_conv_w[l], hy_conv_b[l][None], batch=batch, n_lat=n_lat)
        vxx = vxx.reshape(3, batch, n1 // 2, ncol)
        z = vxx[0]
        for o in range(HY_ORDER):
            a = _dft_outer_call(z, mats["outer"], nt)
            cp = _spectral_call(a.reshape(2, n1, DFT_INNER, d_hy), mats["tw"], kf, o, mats["inner"], mats["inner_inv"])
            skip_t = jnp.tile(hy_skip[l, o], DFT_INNER)[None]
            ss_t = jnp.tile(ss[0, o * d_hy:(o + 1) * d_hy], DFT_INNER)[None]
            z = _dft_outer_inv_call(cp.reshape(2 * n1, ncol), mats["outer_inv"], z, vxx[1 + o], skip_t, ss_t, nt)
        hy = z.reshape(n_lat_total, d_hy)
        if ctx_out:
            ktc, ssc = _filter_call(n_ctx, *filt, d_hy)
            hyc = _hyctx_call(p, off_hy, hy_conv_w[l], hy_conv_b[l][None], ktc, ssc, hy_skip[l], ctx_mats,
                              batch=batch, n_lat=n_lat, n_ctx=n_ctx)
            hy = jnp.concatenate([hy, hyc], axis=0)

        n_tok = n_all if ctx_out else n_lat_total
        xt = _merge_call(xt, hs[0], hs[1], p, hy, sg, grp_norm_g[l][None], w_out[l].astype(BF16), mod[2],
                         seg_of_row_tile(TM), n_tok)

        rw = jnp.zeros((d, LANES), F32).at[:, :N_EXPERTS].set(router_w[l])
        rb = jnp.full((1, LANES), -1e30, F32).at[0, :N_EXPERTS].set(router_b[l])
        xt = _moe(xt, norm_ffn_g[l][None], mod[4], mod[3], mod[5], rw, rb, moe_w1, moe_b1, moe_w2, moe_b2,
                  l, seg_of_row_tile, n_tok)

    return _final_call(xt, final_norm_g[None], n_lat_total).reshape(batch, n_lat, d)
```

```python
import functools
import math

import numpy as np
import jax
import jax.numpy as jnp
from jax import lax
from jax.experimental import pallas as pl
from jax.experimental.pallas import tpu as pltpu

F32 = jnp.float32
BF16 = jnp.bfloat16

EPS = 1e-6
N_MOD = 6
LRU_HEADS = 8
LRU_CONV = 4
LRU_C = 8.0
HY_ORDER = 2
HY_CONV = 3
HY_BANDS = 8
HY_EMB = 1 + 2 * HY_BANDS
HY_EMB_PAD = 32
HY_MIN_DECAY = math.log(1e-2) / 1.5
HY_MAX_DECAY = math.log(1e-2) / 0.3
SG_CHUNK = 128
SG_HEADS = 4
N_EXPERTS = 32
TOP_K = 4
MOE_BLOCK = 256
SWIGLU_LIMIT = 7.0
SWIGLU_ALPHA = 1.702

LANES = 128
SUBLANES = 8
TM = 256
HALO = SUBLANES
DFT_INNER = 128
SEG_ROWS = SUBLANES
LOCAL_ROWS = TM * TOP_K + N_EXPERTS * SEG_ROWS
VMEM_LIMIT = 48 * 1024 * 1024


def _cparams(*sem, vmem=VMEM_LIMIT, row_dma=False):
    return pltpu.CompilerParams(dimension_semantics=sem, vmem_limit_bytes=vmem, disable_bounds_checks=row_dma)


def _bdot(a, b):
    return jnp.dot(a, b, preferred_element_type=F32)


def _hdot(a, b):
    return jnp.dot(a, b, preferred_element_type=F32, precision=lax.Precision.HIGHEST)


def _gelu(x):
    return 0.5 * x * (1.0 + jnp.tanh(math.sqrt(2.0 / math.pi) * (x + 0.044715 * (x * x * x))))


def _sigmoid(x):
    return 0.5 * jnp.tanh(0.5 * x) + 0.5


def _rms(x):
    return x * lax.rsqrt(jnp.mean(x * x, axis=-1, keepdims=True) + EPS)


def _ada_kernel(s_ref, w_ref, b_ref, o_ref):
    s = s_ref[...]
    s = s * _sigmoid(s)
    o_ref[0] = _bdot(s.astype(BF16), w_ref[0].astype(BF16)) + b_ref[0]


def _ada_call(cond, ada_w, ada_b):
    depth, d, n = ada_w.shape
    tn = 1536
    return pl.pallas_call(
        _ada_kernel,
        grid=(depth, n // tn),
        in_specs=[
            pl.BlockSpec((SUBLANES, d), lambda l, j: (0, 0)),
            pl.BlockSpec((1, d, tn), lambda l, j: (l, 0, j)),
            pl.BlockSpec((1, 1, tn), lambda l, j: (l, 0, j)),
        ],
        out_specs=pl.BlockSpec((1, SUBLANES, tn), lambda l, j: (l, 0, j)),
        out_shape=jax.ShapeDtypeStruct((depth, SUBLANES, n), F32),
        compiler_params=_cparams("arbitrary", "arbitrary"),
        name="ada_mod",
    )(cond, ada_w, ada_b.reshape(depth, 1, n))


def _inproj_kernel(x_ref, g_ref, sc_ref, sh_ref, w_ref, o_ref):
    h = _rms(x_ref[...]) * g_ref[...] * (1.0 + sc_ref[0]) + sh_ref[0]
    o_ref[...] = _bdot(h.astype(BF16), w_ref[...])


def _inproj_call(x, g, sc, sh, w, seg_of_tile):
    t, d = x.shape
    n = w.shape[1]
    tn = n // 2
    return pl.pallas_call(
        _inproj_kernel,
        grid=(2, t // TM),
        in_specs=[
            pl.BlockSpec((TM, d), lambda j, i: (i, 0)),
            pl.BlockSpec((1, d), lambda j, i: (0, 0)),
            pl.BlockSpec((1, 1, d), lambda j, i: (seg_of_tile(i), 0, 0)),
            pl.BlockSpec((1, 1, d), lambda j, i: (seg_of_tile(i), 0, 0)),
            pl.BlockSpec((d, tn), lambda j, i: (0, j)),
        ],
        out_specs=pl.BlockSpec((TM, tn), lambda j, i: (i, j)),
        out_shape=jax.ShapeDtypeStruct((t, n), F32),
        compiler_params=_cparams("arbitrary", "arbitrary"),
        name="in_proj",
    )(x, g, sc, sh, w)


def _lru_kernel(x_ref, xp_ref, xn_ref, cw_ref, cb_ref, wg_ref, bg_ref, lam_ref, o_ref,
                xe_s, a_s, b_s, h_s, *, reverse, n_chunks, d_lru):
    s = pl.program_id(1)
    j = (n_chunks - s) if reverse else (s - 1)
    has_prev = jnp.logical_and(s > 0, j > 0)
    has_next = jnp.logical_and(s > 0, j < n_chunks - 1)

    @pl.when(s == 0)
    def _():
        h_s[...] = jnp.zeros_like(h_s)

    xe_s[0:HALO, :] = jnp.where(has_prev, xp_ref[...], 0.0)
    xe_s[HALO:HALO + TM, :] = x_ref[...]
    xe_s[HALO + TM:HALO + TM + HALO, :] = jnp.where(has_next, xn_ref[...], 0.0)
    left = LRU_CONV // 2
    xc = cb_ref[...]
    for k in range(LRU_CONV):
        xc = xc + cw_ref[k:k + 1, :] * xe_s[pl.ds(HALO + k - left, TM), :]

    g = _bdot(xc.astype(BF16), wg_ref[...]) + bg_ref[...]
    r = _sigmoid(g[:, :d_lru])
    ig = _sigmoid(g[:, d_lru:])
    lam = lam_ref[...]
    sp = jnp.maximum(-lam, 0.0) + jnp.log1p(jnp.exp(-jnp.abs(lam)))
    log_a = (-LRU_C * r) * sp
    a = jnp.exp(log_a)
    a_s[...] = a
    b_s[...] = jnp.sqrt(-jnp.tanh(log_a) * (a * a + 1.0)) * (ig * xc)

    def step(t, h):
        tt = (TM - 1 - t) if reverse else t
        h = a_s[pl.ds(tt, 1), :] * h + b_s[pl.ds(tt, 1), :]
        o_ref[pl.ds(tt, 1), :] = h
        return h

    h_s[0:1, :] = lax.fori_loop(0, TM, step, h_s[0:1, :], unroll=8)


def _lru_call(p, cw, cb, wg, bg, lam, *, reverse, batch, n_lat, n_ctx):
    t = p.shape[0]
    d_lru = cw.shape[1]
    n_chunks = n_lat // TM
    assert n_ctx == TM
    ctx0 = batch * n_chunks
    per = TM // HALO
    last = t // HALO - 1

    def rb(b, s):
        lat = b * n_chunks + ((n_chunks - s) if reverse else (s - 1))
        return jnp.where(s == 0, ctx0 + b, lat)

    kern = functools.partial(_lru_kernel, reverse=reverse, n_chunks=n_chunks, d_lru=d_lru)
    return pl.pallas_call(
        kern,
        grid=(batch, n_chunks + 1),
        in_specs=[
            pl.BlockSpec((TM, d_lru), lambda b, s: (rb(b, s), 0)),
            pl.BlockSpec((HALO, d_lru), lambda b, s: (jnp.maximum(rb(b, s) * per - 1, 0), 0)),
            pl.BlockSpec((HALO, d_lru), lambda b, s: (jnp.minimum((rb(b, s) + 1) * per, last), 0)),
            pl.BlockSpec((LRU_CONV, d_lru), lambda b, s: (0, 0)),
            pl.BlockSpec((1, d_lru), lambda b, s: (0, 0)),
            pl.BlockSpec((d_lru, 2 * d_lru), lambda b, s: (0, 0)),
            pl.BlockSpec((1, 2 * d_lru), lambda b, s: (0, 0)),
            pl.BlockSpec((1, d_lru), lambda b, s: (0, 0)),
        ],
        out_specs=pl.BlockSpec((TM, d_lru), lambda b, s: (rb(b, s), 0)),
        out_shape=jax.ShapeDtypeStruct((t, d_lru), F32),
        scratch_shapes=[
            pltpu.VMEM((TM + 2 * HALO, d_lru), F32),
            pltpu.VMEM((TM, d_lru), F32),
            pltpu.VMEM((TM, d_lru), F32),
            pltpu.VMEM((SUBLANES, d_lru), F32),
        ],
        compiler_params=_cparams("arbitrary", "arbitrary"),
        name="rglru_rev" if reverse else "rglru_fwd",
    )(p, p, p, cw, cb, wg, bg, lam)


def _sg_kernel(u0_ref, u1_ref, v0_ref, v1_ref, lg_ref, lb_ref, ws_ref, bias_ref, o_ref):
    u = _gelu(jnp.concatenate([u0_ref[...], u1_ref[...]], axis=-1))
    v = _gelu(jnp.concatenate([v0_ref[...], v1_ref[...]], axis=-1))
    mu = jnp.mean(v, axis=-1, keepdims=True)
    dv = v - mu
    var = jnp.mean(dv * dv, axis=-1, keepdims=True)
    vb = ((dv * lax.rsqrt(var + EPS)) * lg_ref[...] + lb_ref[...]).astype(BF16)
    hd = vb.shape[1] // SG_HEADS
    for ch in range(TM // SG_CHUNK):
        r0 = ch * SG_CHUNK
        ys = [_bdot(ws_ref[g], vb[r0:r0 + SG_CHUNK, g * hd:(g + 1) * hd]) for g in range(SG_HEADS)]
        y = jnp.concatenate(ys, axis=-1) + bias_ref[...]
        o_ref[r0:r0 + SG_CHUNK, :] = u[r0:r0 + SG_CHUNK, :] * y


def _sg_call(p, off_sg, lg, lb, ws, bias):
    t = p.shape[0]
    d_sg = lg.shape[1]
    half = d_sg // 2
    c0 = off_sg // half
    specs = [pl.BlockSpec((TM, half), (lambda i, c=c0 + k: (i, c))) for k in range(4)]
    return pl.pallas_call(
        _sg_kernel,
        grid=(t // TM,),
        in_specs=specs + [
            pl.BlockSpec((1, d_sg), lambda i: (0, 0)),
            pl.BlockSpec((1, d_sg), lambda i: (0, 0)),
            pl.BlockSpec((SG_HEADS, SG_CHUNK, SG_CHUNK), lambda i: (0, 0, 0)),
            pl.BlockSpec((SG_CHUNK, d_sg), lambda i: (0, 0)),
        ],
        out_specs=pl.BlockSpec((TM, d_sg), lambda i: (i, 0)),
        out_shape=jax.ShapeDtypeStruct((t, d_sg), F32),
        compiler_params=_cparams("arbitrary"),
        name="spatial_gating",
    )(p, p, p, p, lg, lb, ws, bias)


def _hyconv_kernel(x_ref, xp_ref, xn_ref, cw_ref, cb_ref, o_ref, xe_s, *, n_chunks):
    j = pl.program_id(1) % n_chunks
    xe_s[0:HALO, :] = jnp.where(j > 0, xp_ref[...], 0.0)
    xe_s[HALO:HALO + TM, :] = x_ref[...]
    xe_s[HALO + TM:HALO + TM + HALO, :] = jnp.where(j < n_chunks - 1, xn_ref[...], 0.0)
    left = HY_CONV // 2
    y = cb_ref[...]
    for k in range(HY_CONV):
        y = y + cw_ref[k:k + 1, :] * xe_s[pl.ds(HALO + k - left, TM), :]
    o_ref[0] = y


def _hyconv_call(p, off_hy, cw, cb, *, batch, n_lat):
    t = p.shape[0]
    d_hy = cw.shape[1] // 3
    n_chunks = n_lat // TM
    c0 = off_hy // d_hy
    per = TM // HALO
    last = t // HALO - 1
    kern = functools.partial(_hyconv_kernel, n_chunks=n_chunks)
    return pl.pallas_call(
        kern,
        grid=(3, batch * n_chunks),
        in_specs=[
            pl.BlockSpec((TM, d_hy), lambda c, i: (i, c0 + c)),
            pl.BlockSpec((HALO, d_hy), lambda c, i: (jnp.maximum(i * per - 1, 0), c0 + c)),
            pl.BlockSpec((HALO, d_hy), lambda c, i: (jnp.minimum((i + 1) * per, last), c0 + c)),
            pl.BlockSpec((HY_CONV, d_hy), lambda c, i: (0, c)),
            pl.BlockSpec((1, d_hy), lambda c, i: (0, c)),
        ],
        out_specs=pl.BlockSpec((1, TM, d_hy), lambda c, i: (c, i, 0)),
        out_shape=jax.ShapeDtypeStruct((3, batch * n_lat, d_hy), F32),
        scratch_shapes=[pltpu.VMEM((TM + 2 * HALO, d_hy), F32)],
        compiler_params=_cparams("arbitrary", "arbitrary"),
        name="hyena_shortconv",
    )(p, p, p, cw, cb)


def _filter_kernel(z_ref, z0_ref, w1_ref, b1_ref, f1_ref, w2_ref, b2_ref, f2_ref, w3_ref, b3_ref,
                   w3b_ref, b3b_ref, dl_ref, kt_ref, ss_ref, *, length, rows):
    i = pl.program_id(0)

    def mlp(z, w3, b3):
        h = jnp.sin(f1_ref[...] * (_hdot(z, w1_ref[...]) + b1_ref[...]))
        h = jnp.sin(f2_ref[...] * (_hdot(h, w2_ref[...]) + b2_ref[...]))
        return _hdot(h, w3) + b3

    z = z_ref[...]
    win = jnp.exp(-z[:, 0:1] * dl_ref[...])
    k = mlp(z, w3_ref[...], b3_ref[...]) * jnp.concatenate([win] * HY_ORDER, axis=-1)
    lag0_back = mlp(z0_ref[...], w3b_ref[...], b3b_ref[...])[0:1, :]
    n = i * rows + lax.broadcasted_iota(jnp.int32, (rows, 1), 0)
    k = k + jnp.where(n == 0, lag0_back, 0.0)
    k = jnp.where(n == length, 0.0, k)
    kt_ref[...] = k

    @pl.when(i == 0)
    def _():
        ss_ref[...] = jnp.zeros_like(ss_ref)

    ss_ref[0:1, :] += jnp.sum(k * k, axis=0, keepdims=True)


def _filter_features(length):
    n = np.arange(2 * length)
    j = np.where(n < length, n, 2 * length - n).astype(np.float64)
    t = j / (length - 1)
    w = (2.0 * math.pi / length) * j
    f = np.linspace(1e-4, HY_BANDS - 1, HY_BANDS)
    z = np.zeros((2 * length, HY_EMB_PAD), np.float64)
    z[:, 0] = t
    z[:, 1:1 + HY_BANDS] = np.cos(w[:, None] * f[None, :])
    z[:, 1 + HY_BANDS:HY_EMB] = -np.sin(w[:, None] * f[None, :])
    return jnp.asarray(z, F32)


def _filter_call(length, w1, b1, f1, w2, b2, f2, w3, b3, d_hy):
    rows = TM
    hid = w1.shape[1]
    nc = HY_ORDER * d_hy
    ztab = _filter_features(length)
    w1p = jnp.zeros((HY_EMB_PAD, hid), F32).at[:HY_EMB].set(w1)
    deltas = jnp.asarray(np.abs(np.linspace(HY_MIN_DECAY, HY_MAX_DECAY, d_hy))[None, :], F32)
    n_fwd = length // rows
    kern = functools.partial(_filter_kernel, length=length, rows=rows)
    full = lambda shape: pl.BlockSpec(shape, lambda i: (0,) * len(shape))
    return pl.pallas_call(
        kern,
        grid=(2 * length // rows,),
        in_specs=[
            pl.BlockSpec((rows, HY_EMB_PAD), lambda i: (i, 0)),
            pl.BlockSpec((SUBLANES, HY_EMB_PAD), lambda i: (0, 0)),
            full((HY_EMB_PAD, hid)), full((1, hid)), full((1, hid)),
            full((hid, hid)), full((1, hid)), full((1, hid)),
            pl.BlockSpec((hid, nc), lambda i: (0, (i >= n_fwd).astype(jnp.int32))),
            pl.BlockSpec((1, nc), lambda i: (0, (i >= n_fwd).astype(jnp.int32))),
            pl.BlockSpec((hid, nc), lambda i: (0, 1)),
            pl.BlockSpec((1, nc), lambda i: (0, 1)),
            full((1, d_hy)),
        ],
        out_specs=[
            pl.BlockSpec((rows, nc), lambda i: (i, 0)),
            pl.BlockSpec((SUBLANES, nc), lambda i: (0, 0)),
        ],
        out_shape=[
            jax.ShapeDtypeStruct((2 * length, nc), F32),
            jax.ShapeDtypeStruct((SUBLANES, nc), F32),
        ],
        compiler_params=_cparams("arbitrary"),
        name="hyena_filter",
    )(ztab, ztab, w1p, b1[None], f1[None], w2, b2[None], f2[None], w3, b3[None], w3, b3[None], deltas)


def _dft_tables(length):
    n_fft = 2 * length
    n1 = n_fft // DFT_INNER
    n1h = n1 // 2
    unit = 2.0 * math.pi / n_fft
    k1 = jnp.arange(n1, dtype=jnp.int32)
    n2 = jnp.arange(DFT_INNER, dtype=jnp.int32)

    def cs(n1_count):
        n = DFT_INNER * jnp.arange(n1_count, dtype=jnp.int32)[None, None, :] + n2[:, None, None]
        th = ((k1[None, :, None] * n) % n_fft).astype(F32) * unit
        return jnp.cos(th), jnp.sin(th)

    c, s = cs(n1h)
    outer = jnp.concatenate([jnp.concatenate([c, s], -1), jnp.concatenate([-s, c], -1)], 1)
    ct, st = jnp.swapaxes(c, 1, 2) / n_fft, jnp.swapaxes(s, 1, 2) / n_fft
    outer_inv = jnp.concatenate([jnp.concatenate([ct, -st], -1), jnp.concatenate([st, ct], -1)], 1)
    c, s = cs(n1)
    outer_real = jnp.concatenate([c, -s], 1)
    m = np.arange(DFT_INNER)
    th = 2.0 * math.pi * np.outer(m, m) / DFT_INNER
    c, s = np.cos(th), np.sin(th)
    as_bf = lambda a: jnp.asarray(a, F32).astype(BF16)
    return dict(outer=outer.astype(BF16), outer_real=outer_real.astype(BF16), outer_inv=outer_inv.astype(BF16),
                inner=as_bf(np.block([[c, s], [-s, c]])), inner_inv=as_bf(np.block([[c, -s], [s, c]])), n1=n1)


def _dft_outer_kernel(x_ref, w_ref, o_ref):
    n1 = o_ref.shape[1]
    for j in range(SUBLANES):
        x = x_ref[..., j, :]
        x = x.reshape(-1, x.shape[-1])
        r = _bdot(w_ref[j], x.astype(BF16))
        o_ref[0, :, j, :] = r[:n1]
        o_ref[1, :, j, :] = r[n1:]


def _dft_outer_call(x, lead, w, ct):
    c = x.shape[-1]
    _, m, r = w.shape
    if lead is None:
        xspec = pl.BlockSpec((x.shape[0], SUBLANES, ct), lambda cc, i: (0, i, cc))
    else:
        xspec = pl.BlockSpec((None, 2, x.shape[2], SUBLANES, ct), lambda cc, i: (lead, 0, 0, i, cc))
    return pl.pallas_call(
        _dft_outer_kernel,
        grid=(c // ct, DFT_INNER // SUBLANES),
        in_specs=[xspec, pl.BlockSpec((SUBLANES, m, r), lambda cc, i: (i, 0, 0))],
        out_specs=pl.BlockSpec((2, m // 2, SUBLANES, ct), lambda cc, i: (0, 0, i, cc)),
        out_shape=jax.ShapeDtypeStruct((2, m // 2, DFT_INNER, c), F32),
        compiler_params=_cparams("arbitrary", "arbitrary"),
        name="hyena_dft_outer",
    )(x, w)


def _filter_spectrum_kernel(a_ref, w_ref, o_ref):
    x = _bdot(w_ref[...], jnp.concatenate([a_ref[0, 0], a_ref[1, 0]], axis=0).astype(BF16))
    o_ref[0, 0] = x[:DFT_INNER].astype(BF16)
    o_ref[1, 0] = x[DFT_INNER:].astype(BF16)


def _filter_spectrum_call(a, w_inner):
    _, n1, _, c = a.shape
    blk = pl.BlockSpec((2, 1, DFT_INNER, c), lambda k: (0, k, 0, 0))
    return pl.pallas_call(
        _filter_spectrum_kernel,
        grid=(n1,),
        in_specs=[blk, pl.BlockSpec((2 * DFT_INNER, 2 * DFT_INNER), lambda k: (0, 0))],
        out_specs=blk,
        out_shape=jax.ShapeDtypeStruct(a.shape, BF16),
        compiler_params=_cparams("arbitrary"),
        name="hyena_filter_spectrum",
    )(a, w_inner)


def _spectral_kernel(a_ref, kf_ref, w_ref, wi_ref, o_ref):
    x = _bdot(w_ref[...], jnp.concatenate([a_ref[0, 0], a_ref[1, 0]], axis=0).astype(BF16))
    xr, xi = x[:DFT_INNER], x[DFT_INNER:]
    kr, ki = kf_ref[0, 0].astype(F32), kf_ref[1, 0].astype(F32)
    yr = xr * kr - xi * ki
    yi = xr * ki + xi * kr
    c = _bdot(wi_ref[...], jnp.concatenate([yr, yi], axis=0).astype(BF16))
    o_ref[0, 0] = c[:DFT_INNER]
    o_ref[1, 0] = c[DFT_INNER:]


def _spectral_call(a, kf, order, w_inner, w_inner_inv):
    _, n1, _, c = a.shape
    blk = pl.BlockSpec((2, 1, DFT_INNER, c), lambda k: (0, k, 0, 0))
    sq = pl.BlockSpec((2 * DFT_INNER, 2 * DFT_INNER), lambda k: (0, 0))
    return pl.pallas_call(
        _spectral_kernel,
        grid=(n1,),
        in_specs=[blk, pl.BlockSpec((2, 1, DFT_INNER, c), lambda k: (0, k, 0, order)), sq, sq],
        out_specs=blk,
        out_shape=jax.ShapeDtypeStruct(a.shape, F32),
        compiler_params=_cparams("arbitrary"),
        name="hyena_spectral",
    )(a, kf, w_inner, w_inner_inv)


def _dft_outer_inv_kernel(c_ref, w_ref, z_ref, gate_ref, skip_ref, ss_ref, o_ref):
    scale = lax.rsqrt(ss_ref[...] + EPS)
    skip = skip_ref[...]
    for j in range(SUBLANES):
        cc = c_ref[:, :, j, :]
        y = _bdot(w_ref[j], cc.reshape(-1, cc.shape[-1]).astype(BF16))
        y = y.reshape(o_ref.shape[0], o_ref.shape[1], o_ref.shape[3])
        o_ref[:, :, j, :] = gate_ref[:, :, j, :] * (scale * y + skip * z_ref[:, :, j, :])


def _dft_outer_inv_call(cp, w, zs, z_lead, gates, gate_lead, skip, ss):
    _, _, n1h, _, c = zs.shape
    n1 = cp.shape[1]
    zspec = lambda lead: pl.BlockSpec((None, 2, n1h, SUBLANES, c), lambda i: (lead, 0, 0, i, 0))
    vspec = pl.BlockSpec((1, c), lambda i: (0, 0))
    return pl.pallas_call(
        _dft_outer_inv_kernel,
        grid=(DFT_INNER // SUBLANES,),
        in_specs=[pl.BlockSpec((2, n1, SUBLANES, c), lambda i: (0, 0, i, 0)),
                  pl.BlockSpec((SUBLANES, 2 * n1h, 2 * n1), lambda i: (i, 0, 0)),
                  zspec(z_lead), zspec(gate_lead), vspec, vspec],
        out_specs=zspec(0),
        out_shape=jax.ShapeDtypeStruct((1, 2, n1h, DFT_INNER, c), F32),
        compiler_params=_cparams("arbitrary"),
        name="hyena_dft_outer_inv",
    )(cp, w, zs, gates, skip, ss)


def _ctx_dft_mats(n_ctx):
    n_fft = 2 * n_ctx
    f = np.arange(n_fft)
    th = 2.0 * math.pi * np.outer(f, np.arange(n_ctx)) / n_fft
    fz = np.concatenate([np.cos(th), -np.sin(th)], axis=0)
    th = 2.0 * math.pi * np.outer(f, f) / n_fft
    fk = np.concatenate([np.cos(th), -np.sin(th)], axis=0)
    th = 2.0 * math.pi * np.outer(np.arange(n_ctx), f) / n_fft
    fi = np.concatenate([np.cos(th), -np.sin(th)], axis=1) / n_fft
    as_bf = lambda a: jnp.asarray(a, F32).astype(BF16)
    return as_bf(fz), as_bf(fk), as_bf(fi)


def _hyctx_kernel(p0_ref, p1_ref, p2_ref, cw_ref, cb_ref, kt_ref, ss_ref, skip_ref,
                  fz_ref, fk_ref, fi_ref, o_ref, xe_s, *, d_hy):
    n_fft = fk_ref.shape[1]
    left = HY_CONV // 2

    def conv(p_ref, c):
        xe_s[0:HALO, :] = jnp.zeros((HALO, d_hy), F32)
        xe_s[HALO:HALO + TM, :] = p_ref[...]
        xe_s[HALO + TM:HALO + TM + HALO, :] = jnp.zeros((HALO, d_hy), F32)
        y = cb_ref[:, c * d_hy:(c + 1) * d_hy]
        for k in range(HY_CONV):
            y = y + cw_ref[k:k + 1, c * d_hy:(c + 1) * d_hy] * xe_s[pl.ds(HALO + k - left, TM), :]
        return y

    z = conv(p0_ref, 0)
    gates = (conv(p1_ref, 1), conv(p2_ref, 2))
    for o in range(HY_ORDER):
        cols = slice(o * d_hy, (o + 1) * d_hy)
        kf = _bdot(fk_ref[...], kt_ref[:, cols].astype(BF16))
        zf = _bdot(fz_ref[...], z.astype(BF16))
        zr, zi = zf[:n_fft], zf[n_fft:]
        kr, ki = kf[:n_fft], kf[n_fft:]
        yf = jnp.concatenate([zr * kr - zi * ki, zr * ki + zi * kr], axis=0)
        y = _bdot(fi_ref[...], yf.astype(BF16))
        scale = lax.rsqrt(ss_ref[0:1, cols] + EPS)
        z = gates[o] * (scale * y + skip_ref[o:o + 1, :] * z)
    o_ref[...] = z


def _hyctx_call(p, off_hy, cw, cb, kt, ss, skip, mats, *, batch, n_lat, n_ctx):
    d_hy = skip.shape[1]
    assert n_ctx == TM
    c0 = off_hy // d_hy
    r0 = batch * n_lat // TM
    fz, fk, fi = mats
    full = lambda a: pl.BlockSpec(a.shape, lambda b: (0,) * a.ndim)
    return pl.pallas_call(
        functools.partial(_hyctx_kernel, d_hy=d_hy),
        grid=(batch,),
        in_specs=[pl.BlockSpec((TM, d_hy), (lambda b, c=c0 + k: (r0 + b, c))) for k in range(3)]
        + [full(cw), full(cb), full(kt), full(ss), full(skip), full(fz), full(fk), full(fi)],
        out_specs=pl.BlockSpec((TM, d_hy), lambda b: (b, 0)),
        out_shape=jax.ShapeDtypeStruct((batch * n_ctx, d_hy), F32),
        scratch_shapes=[pltpu.VMEM((TM + 2 * HALO, d_hy), F32)],
        compiler_params=_cparams("arbitrary"),
        name="hyena_ctx",
    )(p, p, p, cw, cb, kt, ss, skip, fz, fk, fi)


def _merge_kernel(x_ref, hf_ref, hb_ref, pg_ref, hy_ref, sg_ref, gg_ref, w_ref, gate_ref, o_ref):
    a = (hf_ref[...] + hb_ref[...]) * _gelu(pg_ref[...])
    y = jnp.concatenate([_rms(a), _rms(hy_ref[...]), _rms(sg_ref[...])], axis=-1) * gg_ref[...]
    o_ref[...] = x_ref[...] + gate_ref[0] * _bdot(y.astype(BF16), w_ref[...])


def _merge_call(x, hf, hb, p, hy, sg, gg, w, gate, seg_of_tile, n_rows):
    d = x.shape[1]
    d_lru, d_hy, d_sg = hf.shape[1], hy.shape[1], sg.shape[1]
    row = lambda width, col=0: pl.BlockSpec((TM, width), lambda i: (i, col))
    return pl.pallas_call(
        _merge_kernel,
        grid=(n_rows // TM,),
        in_specs=[row(d), row(d_lru), row(d_lru), row(d_lru, 1), row(d_hy), row(d_sg),
                  pl.BlockSpec((1, d), lambda i: (0, 0)),
                  pl.BlockSpec((d, d), lambda i: (0, 0)),
                  pl.BlockSpec((1, 1, d), lambda i: (seg_of_tile(i), 0, 0))],
        out_specs=row(d),
        out_shape=jax.ShapeDtypeStruct((n_rows, d), F32),
        compiler_params=_cparams("arbitrary"),
        name="merge_out_proj",
    )(x, hf, hb, p, hy, sg, gg, w, gate)


def _pack_bf16_pairs(x):
    half = x.shape[1] // 2
    lo = pltpu.bitcast(x[:, :half].astype(BF16).astype(F32), jnp.uint32)
    hi = pltpu.bitcast(x[:, half:].astype(BF16).astype(F32), jnp.uint32)
    return (lo >> 16) | (hi & jnp.uint32(0xFFFF0000))


def _unpack_bf16_pairs(u):
    lo = pltpu.bitcast(u << 16, F32).astype(BF16)
    hi = pltpu.bitcast(u & jnp.uint32(0xFFFF0000), F32).astype(BF16)
    return lo, hi


def _router_kernel(x_ref, g_ref, sc_ref, sh_ref, rw_ref, rb_ref, f_ref, e_ref, gt_ref, m_ref):
    f = _rms(x_ref[...]) * g_ref[...] * (1.0 + sc_ref[0]) + sh_ref[0]
    f_ref[...] = f.astype(BF16)
    logits = _hdot(f, rw_ref[...]) + rb_ref[...]
    lane = lax.broadcasted_iota(jnp.int32, logits.shape, 1).astype(F32)
    work = logits
    e_out = jnp.zeros_like(logits)
    mask = jnp.zeros_like(logits)
    vals = []
    for k in range(TOP_K):
        mx = jnp.max(work, axis=-1, keepdims=True)
        idx = jnp.min(jnp.where(work == mx, lane, float(LANES)), axis=-1, keepdims=True)
        sel = lane == idx
        e_out = jnp.where(lane == float(k), idx, e_out)
        mask = jnp.where(sel, 1.0, mask)
        work = jnp.where(sel, -jnp.inf, work)
        vals.append(mx)
    ex = [jnp.exp(v - vals[0]) for v in vals]
    den = ex[0]
    for v in ex[1:]:
        den = den + v
    g_out = jnp.zeros_like(logits)
    for k in range(TOP_K):
        g_out = jnp.where(lane == float(k), ex[k] / den, g_out)
    e_ref[...] = e_out.astype(jnp.int32)
    gt_ref[...] = g_out
    m_ref[...] = mask


def _router_call(x, g, sc, sh, rw, rb, seg_of_tile, n_rows):
    d = x.shape[1]
    row = lambda width: pl.BlockSpec((TM, width), lambda i: (i, 0))
    mod = pl.BlockSpec((1, 1, d), lambda i: (seg_of_tile(i), 0, 0))
    return pl.pallas_call(
        _router_kernel,
        grid=(n_rows // TM,),
        in_specs=[row(d), pl.BlockSpec((1, d), lambda i: (0, 0)), mod, mod,
                  pl.BlockSpec((d, LANES), lambda i: (0, 0)), pl.BlockSpec((1, LANES), lambda i: (0, 0))],
        out_specs=[row(d), row(LANES), row(LANES), row(LANES)],
        out_shape=[jax.ShapeDtypeStruct((n_rows, d), BF16), jax.ShapeDtypeStruct((n_rows, LANES), jnp.int32),
                   jax.ShapeDtypeStruct((n_rows, LANES), F32), jax.ShapeDtypeStruct((n_rows, LANES), F32)],
        compiler_params=_cparams("arbitrary"),
        name="moe_router",
    )(x, g, sc, sh, rw, rb)


def _rank_kernel(m_ref, e_ref, tri_ref, utri_ref, ls_ref, meta_ref, tot_ref, carry_s):
    i = pl.program_id(0)

    @pl.when(i == 0)
    def _():
        carry_s[...] = jnp.zeros_like(carry_s)

    m = m_ref[...]
    rank = _bdot(tri_ref[...], m.astype(BF16))
    cnt = jnp.sum(m, axis=0, keepdims=True)
    seg = jnp.floor((cnt + (SEG_ROWS - 1)) * (1.0 / SEG_ROWS)) * SEG_ROWS
    seg8 = jnp.broadcast_to(seg, (SUBLANES, LANES))
    loc = _bdot(seg8.astype(BF16), utri_ref[...])[0:1, :]
    sd = rank + loc
    e = e_ref[...]
    lane = lax.broadcasted_iota(jnp.int32, sd.shape, 1)
    out = jnp.zeros_like(sd)
    for k in range(TOP_K):
        sk = jnp.sum(jnp.where(lane == e[:, k:k + 1], sd, 0.0), axis=-1, keepdims=True)
        out = jnp.where(lane == k, sk, out)
    ls_ref[...] = out.astype(jnp.int32)
    row = lax.broadcasted_iota(jnp.int32, (SUBLANES, LANES), 0)
    meta_ref[0] = jnp.where(row == 0, carry_s[...], jnp.where(row == 1, seg8, 0.0))
    carry_s[...] = carry_s[...] + seg8
    tot_ref[...] = carry_s[...]


def _rank_call(mask, e4):
    n = mask.shape[0]
    tri = jnp.asarray(np.tril(np.ones((TM, TM)), -1), BF16)
    utri = jnp.asarray(np.triu(np.ones((LANES, LANES)), 1), BF16)
    row = pl.BlockSpec((TM, LANES), lambda i: (i, 0))
    return pl.pallas_call(
        _rank_kernel,
        grid=(n // TM,),
        in_specs=[row, row, pl.BlockSpec((TM, TM), lambda i: (0, 0)), pl.BlockSpec((LANES, LANES), lambda i: (0, 0))],
        out_specs=[row, pl.BlockSpec((1, SUBLANES, LANES), lambda i: (i, 0, 0)),
                   pl.BlockSpec((SUBLANES, LANES), lambda i: (0, 0))],
        out_shape=[jax.ShapeDtypeStruct((n, LANES), jnp.int32), jax.ShapeDtypeStruct((n // TM, SUBLANES, LANES), F32),
                   jax.ShapeDtypeStruct((SUBLANES, LANES), F32)],
        scratch_shapes=[pltpu.VMEM((SUBLANES, LANES), F32)],
        compiler_params=_cparams("arbitrary"),
        name="moe_rank",
    )(mask, e4, tri, utri)


def _seg_copy(src, dst, src_row, dst_row, sem):
    return pltpu.make_async_copy(src.at[pl.ds(src_row, SEG_ROWS)], dst.at[pl.ds(dst_row, SEG_ROWS)], sem)


def _drain(copy, count):
    def body(r, c):
        copy.wait()
        return c

    lax.fori_loop(0, count, body, 0)


def _tile_segments(seg_ref, nch_ref, tile, move):
    off = jnp.int32(0)
    for e in range(N_EXPERTS):
        nc = nch_ref[tile * N_EXPERTS + e]
        g0 = seg_ref[tile * N_EXPERTS + e]

        def body(c, carry, off=off, g0=g0):
            move(pl.multiple_of(off + c * SEG_ROWS, SEG_ROWS), pl.multiple_of(g0 + c * SEG_ROWS, SEG_ROWS))
            return carry

        lax.fori_loop(0, nc, body, 0)
        off = off + nc * SEG_ROWS
    return off // SEG_ROWS


def _dispatch_kernel(seg_ref, nch_ref, pad_ref, f_ref, ls_ref, xs_hbm, xl_s, zero_s, sem, zsem):
    i = pl.program_id(0)
    rows = xl_s.shape[0]
    ls_t = ls_ref[...].astype(F32).T
    r_iota = lax.broadcasted_iota(jnp.int32, (rows, TM), 0).astype(F32)
    perm = jnp.zeros((rows, TM), F32)
    for k in range(TOP_K):
        perm = jnp.where(r_iota == ls_t[k:k + 1, :], 1.0, perm)
    xl_s[...] = _pack_bf16_pairs(_bdot(perm.astype(BF16), f_ref[...]))

    n_chunks = _tile_segments(seg_ref, nch_ref, i, lambda lr, gr: _seg_copy(xl_s, xs_hbm, lr, gr, sem).start())

    @pl.when(i == 0)
    def _():
        zero_s[...] = jnp.zeros_like(zero_s)
        total = jnp.int32(0)
        for e in range(N_EXPERTS):
            nc = pad_ref[N_EXPERTS + e]
            g0 = pad_ref[e]

            def body(c, carry, g0=g0):
                _seg_copy(zero_s, xs_hbm, 0, pl.multiple_of(g0 + c * SEG_ROWS, SEG_ROWS), zsem).start()
                return carry

            lax.fori_loop(0, nc, body, 0)
            total = total + nc
        _drain(_seg_copy(zero_s, xs_hbm, 0, 0, zsem), total)

        def block_copy(b):
            return pltpu.make_async_copy(zero_s, xs_hbm.at[pl.ds(pl.multiple_of(b * MOE_BLOCK, MOE_BLOCK), MOE_BLOCK)], zsem)

        n_used = pad_ref[2 * N_EXPERTS]
        n_blocks = xs_hbm.shape[0] // MOE_BLOCK

        def start_block(b, carry):
            block_copy(b).start()
            return carry

        def wait_block(b, carry):
            block_copy(b).wait()
            return carry

        lax.fori_loop(n_used, n_blocks, start_block, 0)
        lax.fori_loop(n_used, n_blocks, wait_block, 0)

    _drain(_seg_copy(xl_s, xs_hbm, 0, 0, sem), n_chunks)


def _dispatch_call(seg, nch, pad, f, ls, n_rows):
    n, d = f.shape
    return pl.pallas_call(
        _dispatch_kernel,
        grid_spec=pltpu.PrefetchScalarGridSpec(
            num_scalar_prefetch=3,
            grid=(n // TM,),
            in_specs=[pl.BlockSpec((TM, d), lambda i, *_: (i, 0)), pl.BlockSpec((TM, LANES), lambda i, *_: (i, 0))],
            out_specs=pl.BlockSpec(memory_space=pl.ANY),
            scratch_shapes=[pltpu.VMEM((LOCAL_ROWS, d // 2), jnp.uint32), pltpu.VMEM((MOE_BLOCK, d // 2), jnp.uint32),
                            pltpu.SemaphoreType.DMA(()), pltpu.SemaphoreType.DMA(())],
        ),
        out_shape=jax.ShapeDtypeStruct((n_rows, d // 2), jnp.uint32),
        compiler_params=_cparams("arbitrary", row_dma=True),
        name="moe_dispatch",
    )(seg, nch, pad, f, ls)


def _expert_kernel(be_ref, nu_ref, x_ref, w1_ref, b1_ref, w2_ref, b2_ref, o_ref, w1_s, w2_s):
    i = pl.program_id(0)
    d_e = w2_ref.shape[2]
    half = x_ref.shape[1]
    live = i < nu_ref[0]
    new_expert = jnp.logical_or(i == 0, be_ref[i] != be_ref[jnp.maximum(i - 1, 0)])

    @pl.when(jnp.logical_and(live, new_expert))
    def _():
        w1_s[...] = w1_ref[0, 0].astype(BF16)
        w2_s[...] = w2_ref[0, 0].astype(BF16)

    @pl.when(live)
    def _():
        lo, hi = _unpack_bf16_pairs(x_ref[...])
        gu = _bdot(lo, w1_s[:half, :]) + _bdot(hi, w1_s[half:, :]) + b1_ref[0, 0]
        glu = jnp.minimum(gu[:, :d_e], SWIGLU_LIMIT)
        lin = jnp.clip(gu[:, d_e:], -SWIGLU_LIMIT, SWIGLU_LIMIT)
        act = glu * _sigmoid(SWIGLU_ALPHA * glu) * (lin + 1.0)
        o_ref[...] = _pack_bf16_pairs(_bdot(act.astype(BF16), w2_s[...]) + b2_ref[0, 0])

    @pl.when(i >= nu_ref[0])
    def _():
        o_ref[...] = jnp.zeros_like(o_ref)


def _expert_call(block_e, n_used, xs, w1, b1, w2, b2, layer):
    n_rows, half = xs.shape
    depth, n_e, d, d_gu = w1.shape
    d_e = w2.shape[2]
    used = lambda i, nu: jnp.minimum(i, nu[0] - 1)
    return pl.pallas_call(
        _expert_kernel,
        grid_spec=pltpu.PrefetchScalarGridSpec(
            num_scalar_prefetch=2,
            grid=(n_rows // MOE_BLOCK,),
            in_specs=[
                pl.BlockSpec((MOE_BLOCK, half), lambda i, be, nu: (used(i, nu), 0)),
                pl.BlockSpec((1, 1, d, d_gu), lambda i, be, nu: (layer, be[i], 0, 0)),
                pl.BlockSpec((1, 1, 1, d_gu), lambda i, be, nu: (layer, be[i], 0, 0)),
                pl.BlockSpec((1, 1, d_e, d), lambda i, be, nu: (layer, be[i], 0, 0)),
                pl.BlockSpec((1, 1, 1, d), lambda i, be, nu: (layer, be[i], 0, 0)),
            ],
            out_specs=pl.BlockSpec((MOE_BLOCK, half), lambda i, be, nu: (i, 0)),
            scratch_shapes=[pltpu.VMEM((d, d_gu), BF16), pltpu.VMEM((d_e, d), BF16)],
        ),
        out_shape=jax.ShapeDtypeStruct((n_rows, half), jnp.uint32),
        compiler_params=_cparams("arbitrary"),
        name="moe_experts",
    )(block_e, n_used, xs, w1, b1.reshape(depth, n_e, 1, d_gu), w2, b2.reshape(depth, n_e, 1, d))


def _combine_kernel(seg_ref, nch_ref, x_ref, ls_ref, g_ref, gate_ref, ys_hbm, o_ref, yl_s, sem):
    i = pl.program_id(0)
    n = pl.num_programs(0)
    rows = yl_s.shape[1]

    def fetch(tile, b):
        return _tile_segments(seg_ref, nch_ref, tile,
                              lambda lr, gr: _seg_copy(ys_hbm, yl_s.at[b], gr, lr, sem.at[b]).start())

    @pl.when(i == 0)
    def _():
        yl_s[...] = jnp.zeros_like(yl_s)
        fetch(0, 0)

    @pl.when(i + 1 < n)
    def _():
        fetch(i + 1, (i + 1) % 2)

    cur = i % 2
    total = jnp.int32(0)
    for e in range(N_EXPERTS):
        total = total + nch_ref[i * N_EXPERTS + e]
    _drain(_seg_copy(ys_hbm, yl_s.at[cur], 0, 0, sem.at[cur]), total)

    ls = ls_ref[...]
    g = g_ref[...]
    r_iota = lax.broadcasted_iota(jnp.int32, (TM, rows), 1)
    wsel = jnp.zeros((TM, rows), F32)
    for k in range(TOP_K):
        wsel = jnp.where(r_iota == ls[:, k:k + 1], g[:, k:k + 1], wsel)
    wsel = wsel.astype(BF16)
    lo, hi = _unpack_bf16_pairs(yl_s[cur])
    acc = jnp.concatenate([_bdot(wsel, lo), _bdot(wsel, hi)], axis=-1)
    o_ref[...] = x_ref[...] + gate_ref[0] * acc


def _combine_call(seg, nch, x, ls, g4, gate, ys, seg_of_tile, n_tok):
    d = x.shape[1]
    row = lambda width: pl.BlockSpec((TM, width), lambda i, *_: (i, 0))
    return pl.pallas_call(
        _combine_kernel,
        grid_spec=pltpu.PrefetchScalarGridSpec(
            num_scalar_prefetch=2,
            grid=(n_tok // TM,),
            in_specs=[row(d), row(LANES), row(LANES),
                      pl.BlockSpec((1, 1, d), lambda i, *_: (seg_of_tile(i), 0, 0)),
                      pl.BlockSpec(memory_space=pl.ANY)],
            out_specs=row(d),
            scratch_shapes=[pltpu.VMEM((2, LOCAL_ROWS, d // 2), jnp.uint32), pltpu.SemaphoreType.DMA((2,))],
        ),
        out_shape=jax.ShapeDtypeStruct((n_tok, d), F32),
        compiler_params=_cparams("arbitrary", row_dma=True),
        name="moe_combine",
    )(seg, nch, x, ls, g4, gate, ys)


def _moe(x, norm_g, sc, sh, gate, rw, rb, w1, b1, w2, b2, layer, seg_of_row_tile, n_tok):
    n_tiles = n_tok // TM
    f, e4, g4, mask = _router_call(x, norm_g, sc, sh, rw, rb, seg_of_row_tile(TM), n_tok)
    ls, meta, tot = _rank_call(mask, e4)
    rows_e = tot[0, :N_EXPERTS].astype(jnp.int32)
    region = (rows_e + MOE_BLOCK - 1) // MOE_BLOCK * MOE_BLOCK
    pend = jnp.cumsum(region)
    pstart = pend - region
    max_rows = n_tok * TOP_K + n_tiles * N_EXPERTS * (SEG_ROWS - 1) + N_EXPERTS * (MOE_BLOCK - 1)
    n_blocks = -(-max_rows // MOE_BLOCK)
    first_row = jnp.arange(n_blocks, dtype=jnp.int32) * MOE_BLOCK
    block_e = jnp.minimum(jnp.sum((pend[None, :] <= first_row[:, None]).astype(jnp.int32), axis=1), N_EXPERTS - 1)
    n_used = (pend[-1:] // MOE_BLOCK).astype(jnp.int32)
    seg = (pstart[None, :] + meta[:, 0, :N_EXPERTS].astype(jnp.int32)).reshape(-1)
    nch = (meta[:, 1, :N_EXPERTS].astype(jnp.int32) // SEG_ROWS).reshape(-1)
    pad = jnp.concatenate([pstart + rows_e, (region - rows_e) // SEG_ROWS, n_used])
    xs = _dispatch_call(seg, nch, pad, f, ls, n_blocks * MOE_BLOCK)
    ys = _expert_call(block_e, n_used, xs, w1, b1, w2, b2, layer)
    return _combine_call(seg, nch, x, ls, g4, gate, ys, seg_of_row_tile(TM), n_tok)


def _final_kernel(x_ref, g_ref, o_ref):
    o_ref[...] = _rms(x_ref[...]) * g_ref[...]


def _final_call(x, g, n_rows):
    d = x.shape[1]
    return pl.pallas_call(
        _final_kernel,
        grid=(n_rows // TM,),
        in_specs=[pl.BlockSpec((TM, d), lambda i: (i, 0)), pl.BlockSpec((1, d), lambda i: (0, 0))],
        out_specs=pl.BlockSpec((TM, d), lambda i: (i, 0)),
        out_shape=jax.ShapeDtypeStruct((n_rows, d), F32),
        compiler_params=_cparams("arbitrary"),
        name="final_norm",
    )(x, g)


def _blockdiag(w):
    h, hd, _ = w.shape
    return jnp.einsum('hij,hg->higj', w, jnp.eye(h, dtype=w.dtype)).reshape(h * hd, h * hd)


def kernel(x, c, ctx, c_ctx, ada_w, ada_b, norm_mix_g, w_in, lru_conv_w, lru_conv_b, lru_wa, lru_ba, lru_wx, lru_bx, lru_lam, hy_conv_w, hy_conv_b, hf_w1, hf_b1, hf_f1, hf_w2, hf_b2, hf_f2, hf_w3, hf_b3, hy_skip, sg_ln_g, sg_ln_b, sg_w, sg_b, grp_norm_g, w_out, norm_ffn_g, router_w, router_b, moe_w1, moe_b1, moe_w2, moe_b2, final_norm_g):
    batch, n_lat, d = x.shape
    n_ctx = ctx.shape[1]
    depth = ada_w.shape[0]
    d_lru = lru_conv_w.shape[2]
    d_hy = hy_skip.shape[2]
    d_sg = sg_ln_g.shape[1]
    off_hy = 2 * d_lru
    off_sg = off_hy + 3 * d_hy
    assert batch == 2 and n_ctx == TM and n_lat % (TM * 2) == 0 and batch + 1 <= SUBLANES
    n_lat_total = batch * n_lat
    n_all = n_lat_total + batch * n_ctx

    def seg_of_row_tile(rows):
        per_seq = n_lat // rows
        return lambda i: jnp.minimum(i // per_seq, batch)

    cond = jnp.zeros((SUBLANES, d), F32).at[:batch].set(c).at[batch].set(c_ctx)
    mods = _ada_call(cond, ada_w, ada_b)

    mats = _dft_tables(n_lat)
    ctx_mats = _ctx_dft_mats(n_ctx)
    n1 = mats["n1"]
    sg_bias_shape = (SG_CHUNK, d_sg)

    xt = jnp.concatenate([x.reshape(n_lat_total, d), ctx.reshape(batch * n_ctx, d)], axis=0)
    for l in range(depth):
        ctx_out = l < depth - 1
        m = mods[l, :batch + 1].reshape(batch + 1, N_MOD, d)
        mod = [m[:, j][:, None, :] for j in range(N_MOD)]

        p = _inproj_call(xt, norm_mix_g[l][None], mod[1], mod[0], w_in[l].astype(BF16), seg_of_row_tile(TM))

        hs = []
        for dr, rev in enumerate((False, True)):
            wg = jnp.concatenate([_blockdiag(lru_wa[l, dr]), _blockdiag(lru_wx[l, dr])], axis=1).astype(BF16)
            bg = jnp.concatenate([lru_ba[l, dr], lru_bx[l, dr]])[None]
            hs.append(_lru_call(p, lru_conv_w[l], lru_conv_b[l][None], wg, bg, lru_lam[l, dr][None],
                                reverse=rev, batch=batch, n_lat=n_lat, n_ctx=n_ctx))

        sg_bias = jnp.broadcast_to(sg_b[l].T[:, :, None], (SG_CHUNK, SG_HEADS, d_sg // SG_HEADS)).reshape(sg_bias_shape)
        sg = _sg_call(p, off_sg, sg_ln_g[l][None], sg_ln_b[l][None], sg_w[l].astype(BF16), sg_bias)

        filt = (hf_w1[l], hf_b1[l], hf_f1[l], hf_w2[l], hf_b2[l], hf_f2[l], hf_w3[l], hf_b3[l])
        kt, ss = _filter_call(n_lat, *filt, d_hy)
        ka = _dft_outer_call(kt.reshape(n1, DFT_INNER, HY_ORDER * d_hy), None, mats["outer_real"], d_hy)
        kf = _filter_spectrum_call(ka, mats["inner"])
        vxx = _hyconv_call(p, off_hy, hy_conv_w[l], hy_conv_b[l][None], batch=batch, n_lat=n_lat)
        vxx = vxx.reshape(3, batch, n1 // 2, DFT_INNER, d_hy)
        zs, z_lead = vxx, 0
        for o in range(HY_ORDER):
            a = _dft_outer_call(zs, z_lead, mats["outer"], d_hy)
            cp = _spectral_call(a, kf, o, mats["inner"], mats["inner_inv"])
            zs = _dft_outer_inv_call(cp, mats["outer_inv"], zs, z_lead, vxx, 1 + o, hy_skip[l, o][None],
                                     ss[0:1, o * d_hy:(o + 1) * d_hy])
            z_lead = 0
        hy = zs.reshape(n_lat_total, d_hy)
        if ctx_out:
            ktc, ssc = _filter_call(n_ctx, *filt, d_hy)
            hyc = _hyctx_call(p, off_hy, hy_conv_w[l], hy_conv_b[l][None], ktc, ssc, hy_skip[l], ctx_mats,
                              batch=batch, n_lat=n_lat, n_ctx=n_ctx)
            hy = jnp.concatenate([hy, hyc], axis=0)

        n_tok = n_all if ctx_out else n_lat_total
        xt = _merge_call(xt, hs[0], hs[1], p, hy, sg, grp_norm_g[l][None], w_out[l].astype(BF16), mod[2],
                         seg_of_row_tile(TM), n_tok)

        rw = jnp.zeros((d, LANES), F32).at[:, :N_EXPERTS].set(router_w[l])
        rb = jnp.full((1, LANES), -1e30, F32).at[0, :N_EXPERTS].set(router_b[l])
        xt = _moe(xt, norm_ffn_g[l][None], mod[4], mod[3], mod[5], rw, rb, moe_w1, moe_b1, moe_w2, moe_b2,
                  l, seg_of_row_tile, n_tok)

    return _final_call(xt, final_norm_g[None], n_lat_total).reshape(batch, n_lat, d)
```

```python
import functools
import math

import numpy as np
import jax
import jax.numpy as jnp
from jax import lax
from jax.experimental import pallas as pl
from jax.experimental.pallas import tpu as pltpu

F32 = jnp.float32
BF16 = jnp.bfloat16

EPS = 1e-6
N_MOD = 6
LRU_HEADS = 8
LRU_CONV = 4
LRU_C = 8.0
HY_ORDER = 2
HY_CONV = 3
HY_BANDS = 8
HY_EMB = 1 + 2 * HY_BANDS
HY_EMB_PAD = 32
HY_MIN_DECAY = math.log(1e-2) / 1.5
HY_MAX_DECAY = math.log(1e-2) / 0.3
SG_CHUNK = 128
SG_HEADS = 4
N_EXPERTS = 32
TOP_K = 4
MOE_BLOCK = 256
SWIGLU_LIMIT = 7.0
SWIGLU_ALPHA = 1.702

LANES = 128
SUBLANES = 8
TM = 256
HALO = SUBLANES
DFT_INNER = 128
SEG_ROWS = SUBLANES
LOCAL_ROWS = TM * TOP_K + N_EXPERTS * SEG_ROWS
VMEM_LIMIT = 48 * 1024 * 1024


def _cparams(*sem, vmem=VMEM_LIMIT, row_dma=False):
    return pltpu.CompilerParams(dimension_semantics=sem, vmem_limit_bytes=vmem, disable_bounds_checks=row_dma)


def _bdot(a, b):
    return jnp.dot(a, b, preferred_element_type=F32)


def _hdot(a, b):
    return jnp.dot(a, b, preferred_element_type=F32, precision=lax.Precision.HIGHEST)


def _gelu(x):
    return 0.5 * x * (1.0 + jnp.tanh(math.sqrt(2.0 / math.pi) * (x + 0.044715 * (x * x * x))))


def _sigmoid(x):
    return 0.5 * jnp.tanh(0.5 * x) + 0.5


def _rms(x):
    return x * lax.rsqrt(jnp.mean(x * x, axis=-1, keepdims=True) + EPS)


def _ada_kernel(s_ref, w_ref, b_ref, o_ref):
    s = s_ref[...]
    s = s * _sigmoid(s)
    o_ref[0] = _bdot(s.astype(BF16), w_ref[0].astype(BF16)) + b_ref[0]


def _ada_call(cond, ada_w, ada_b):
    depth, d, n = ada_w.shape
    tn = 1536
    return pl.pallas_call(
        _ada_kernel,
        grid=(depth, n // tn),
        in_specs=[
            pl.BlockSpec((SUBLANES, d), lambda l, j: (0, 0)),
            pl.BlockSpec((1, d, tn), lambda l, j: (l, 0, j)),
            pl.BlockSpec((1, 1, tn), lambda l, j: (l, 0, j)),
        ],
        out_specs=pl.BlockSpec((1, SUBLANES, tn), lambda l, j: (l, 0, j)),
        out_shape=jax.ShapeDtypeStruct((depth, SUBLANES, n), F32),
        compiler_params=_cparams("arbitrary", "arbitrary"),
        name="ada_mod",
    )(cond, ada_w, ada_b.reshape(depth, 1, n))


def _inproj_kernel(x_ref, g_ref, sc_ref, sh_ref, w_ref, o_ref):
    h = _rms(x_ref[...]) * g_ref[...] * (1.0 + sc_ref[0]) + sh_ref[0]
    o_ref[...] = _bdot(h.astype(BF16), w_ref[...])


def _inproj_call(x, g, sc, sh, w, seg_of_tile):
    t, d = x.shape
    n = w.shape[1]
    tn = n // 2
    return pl.pallas_call(
        _inproj_kernel,
        grid=(2, t // TM),
        in_specs=[
            pl.BlockSpec((TM, d), lambda j, i: (i, 0)),
            pl.BlockSpec((1, d), lambda j, i: (0, 0)),
            pl.BlockSpec((1, 1, d), lambda j, i: (seg_of_tile(i), 0, 0)),
            pl.BlockSpec((1, 1, d), lambda j, i: (seg_of_tile(i), 0, 0)),
            pl.BlockSpec((d, tn), lambda j, i: (0, j)),
        ],
        out_specs=pl.BlockSpec((TM, tn), lambda j, i: (i, j)),
        out_shape=jax.ShapeDtypeStruct((t, n), F32),
        compiler_params=_cparams("arbitrary", "arbitrary"),
        name="in_proj",
    )(x, g, sc, sh, w)


def _lru_kernel(x_ref, xp_ref, xn_ref, cw_ref, cb_ref, wg_ref, bg_ref, lam_ref, o_ref,
                xe_s, a_s, b_s, h_s, *, reverse, n_chunks, d_lru):
    s = pl.program_id(1)
    j = (n_chunks - s) if reverse else (s - 1)
    has_prev = jnp.logical_and(s > 0, j > 0)
    has_next = jnp.logical_and(s > 0, j < n_chunks - 1)

    @pl.when(s == 0)
    def _():
        h_s[...] = jnp.zeros_like(h_s)

    xe_s[0:HALO, :] = jnp.where(has_prev, xp_ref[...], 0.0)
    xe_s[HALO:HALO + TM, :] = x_ref[...]
    xe_s[HALO + TM:HALO + TM + HALO, :] = jnp.where(has_next, xn_ref[...], 0.0)
    left = LRU_CONV // 2
    xc = cb_ref[...]
    for k in range(LRU_CONV):
        xc = xc + cw_ref[k:k + 1, :] * xe_s[pl.ds(HALO + k - left, TM), :]

    g = _bdot(xc.astype(BF16), wg_ref[...]) + bg_ref[...]
    r = _sigmoid(g[:, :d_lru])
    ig = _sigmoid(g[:, d_lru:])
    lam = lam_ref[...]
    sp = jnp.maximum(-lam, 0.0) + jnp.log1p(jnp.exp(-jnp.abs(lam)))
    log_a = (-LRU_C * r) * sp
    a = jnp.exp(log_a)
    a_s[...] = a
    b_s[...] = jnp.sqrt(-jnp.tanh(log_a) * (a * a + 1.0)) * (ig * xc)

    def step(t, h):
        tt = (TM - 1 - t) if reverse else t
        h = a_s[pl.ds(tt, 1), :] * h + b_s[pl.ds(tt, 1), :]
        o_ref[pl.ds(tt, 1), :] = h
        return h

    h_s[0:1, :] = lax.fori_loop(0, TM, step, h_s[0:1, :], unroll=8)


def _lru_call(p, cw, cb, wg, bg, lam, *, reverse, batch, n_lat, n_ctx):
    t = p.shape[0]
    d_lru = cw.shape[1]
    n_chunks = n_lat // TM
    assert n_ctx == TM
    ctx0 = batch * n_chunks
    per = TM // HALO
    last = t // HALO - 1

    def rb(b, s):
        lat = b * n_chunks + ((n_chunks - s) if reverse else (s - 1))
        return jnp.where(s == 0, ctx0 + b, lat)

    kern = functools.partial(_lru_kernel, reverse=reverse, n_chunks=n_chunks, d_lru=d_lru)
    return pl.pallas_call(
        kern,
        grid=(batch, n_chunks + 1),
        in_specs=[
            pl.BlockSpec((TM, d_lru), lambda b, s: (rb(b, s), 0)),
            pl.BlockSpec((HALO, d_lru), lambda b, s: (jnp.maximum(rb(b, s) * per - 1, 0), 0)),
            pl.BlockSpec((HALO, d_lru), lambda b, s: (jnp.minimum((rb(b, s) + 1) * per, last), 0)),
            pl.BlockSpec((LRU_CONV, d_lru), lambda b, s: (0, 0)),
            pl.BlockSpec((1, d_lru), lambda b, s: (0, 0)),
            pl.BlockSpec((d_lru, 2 * d_lru), lambda b, s: (0, 0)),
            pl.BlockSpec((1, 2 * d_lru), lambda b, s: (0, 0)),
            pl.BlockSpec((1, d_lru), lambda b, s: (0, 0)),
        ],
        out_specs=pl.BlockSpec((TM, d_lru), lambda b, s: (rb(b, s), 0)),
        out_shape=jax.ShapeDtypeStruct((t, d_lru), F32),
        scratch_shapes=[
            pltpu.VMEM((TM + 2 * HALO, d_lru), F32),
            pltpu.VMEM((TM, d_lru), F32),
            pltpu.VMEM((TM, d_lru), F32),
            pltpu.VMEM((SUBLANES, d_lru), F32),
        ],
        compiler_params=_cparams("arbitrary", "arbitrary"),
        name="rglru_rev" if reverse else "rglru_fwd",
    )(p, p, p, cw, cb, wg, bg, lam)


def _sg_kernel(u0_ref, u1_ref, v0_ref, v1_ref, lg_ref, lb_ref, ws_ref, bias_ref, o_ref):
    u = _gelu(jnp.concatenate([u0_ref[...], u1_ref[...]], axis=-1))
    v = _gelu(jnp.concatenate([v0_ref[...], v1_ref[...]], axis=-1))
    mu = jnp.mean(v, axis=-1, keepdims=True)
    dv = v - mu
    var = jnp.mean(dv * dv, axis=-1, keepdims=True)
    vb = ((dv * lax.rsqrt(var + EPS)) * lg_ref[...] + lb_ref[...]).astype(BF16)
    hd = vb.shape[1] // SG_HEADS
    for ch in range(TM // SG_CHUNK):
        r0 = ch * SG_CHUNK
        ys = [_bdot(ws_ref[g], vb[r0:r0 + SG_CHUNK, g * hd:(g + 1) * hd]) for g in range(SG_HEADS)]
        y = jnp.concatenate(ys, axis=-1) + bias_ref[...]
        o_ref[r0:r0 + SG_CHUNK, :] = u[r0:r0 + SG_CHUNK, :] * y


def _sg_call(p, off_sg, lg, lb, ws, bias):
    t = p.shape[0]
    d_sg = lg.shape[1]
    half = d_sg // 2
    c0 = off_sg // half
    specs = [pl.BlockSpec((TM, half), (lambda i, c=c0 + k: (i, c))) for k in range(4)]
    return pl.pallas_call(
        _sg_kernel,
        grid=(t // TM,),
        in_specs=specs + [
            pl.BlockSpec((1, d_sg), lambda i: (0, 0)),
            pl.BlockSpec((1, d_sg), lambda i: (0, 0)),
            pl.BlockSpec((SG_HEADS, SG_CHUNK, SG_CHUNK), lambda i: (0, 0, 0)),
            pl.BlockSpec((SG_CHUNK, d_sg), lambda i: (0, 0)),
        ],
        out_specs=pl.BlockSpec((TM, d_sg), lambda i: (i, 0)),
        out_shape=jax.ShapeDtypeStruct((t, d_sg), F32),
        compiler_params=_cparams("arbitrary"),
        name="spatial_gating",
    )(p, p, p, p, lg, lb, ws, bias)


def _hyconv_kernel(x_ref, xp_ref, xn_ref, cw_ref, cb_ref, o_ref, xe_s, *, n_chunks):
    j = pl.program_id(1) % n_chunks
    xe_s[0:HALO, :] = jnp.where(j > 0, xp_ref[...], 0.0)
    xe_s[HALO:HALO + TM, :] = x_ref[...]
    xe_s[HALO + TM:HALO + TM + HALO, :] = jnp.where(j < n_chunks - 1, xn_ref[...], 0.0)
    left = HY_CONV // 2
    y = cb_ref[...]
    for k in range(HY_CONV):
        y = y + cw_ref[k:k + 1, :] * xe_s[pl.ds(HALO + k - left, TM), :]
    o_ref[0] = y


def _hyconv_call(p, off_hy, cw, cb, *, batch, n_lat):
    t = p.shape[0]
    d_hy = cw.shape[1] // 3
    n_chunks = n_lat // TM
    c0 = off_hy // d_hy
    per = TM // HALO
    last = t // HALO - 1
    kern = functools.partial(_hyconv_kernel, n_chunks=n_chunks)
    return pl.pallas_call(
        kern,
        grid=(3, batch * n_chunks),
        in_specs=[
            pl.BlockSpec((TM, d_hy), lambda c, i: (i, c0 + c)),
            pl.BlockSpec((HALO, d_hy), lambda c, i: (jnp.maximum(i * per - 1, 0), c0 + c)),
            pl.BlockSpec((HALO, d_hy), lambda c, i: (jnp.minimum((i + 1) * per, last), c0 + c)),
            pl.BlockSpec((HY_CONV, d_hy), lambda c, i: (0, c)),
            pl.BlockSpec((1, d_hy), lambda c, i: (0, c)),
        ],
        out_specs=pl.BlockSpec((1, TM, d_hy), lambda c, i: (c, i, 0)),
        out_shape=jax.ShapeDtypeStruct((3, batch * n_lat, d_hy), F32),
        scratch_shapes=[pltpu.VMEM((TM + 2 * HALO, d_hy), F32)],
        compiler_params=_cparams("arbitrary", "arbitrary"),
        name="hyena_shortconv",
    )(p, p, p, cw, cb)


def _filter_kernel(z_ref, zt_ref, z0t_ref, w1_ref, b1_ref, f1_ref, w2_ref, b2_ref, f2_ref, w3h_ref, w3l_ref, b3_ref,
                   w3bh_ref, w3bl_ref, b3b_ref, dl_ref, kt_ref, ss_ref, *, length, rows):
    i = pl.program_id(0)

    def mlp(zt, w3h, w3l, b3):
        h = jnp.sin(f1_ref[...] * (_hdot(w1_ref[...], zt) + b1_ref[...]))
        h = jnp.sin(f2_ref[...] * (_hdot(w2_ref[...], h) + b2_ref[...])).T
        h_hi = h.astype(BF16)
        h_lo = (h - h_hi.astype(F32)).astype(BF16)
        return _bdot(h_hi, w3h) + _bdot(h_lo, w3h) + _bdot(h_hi, w3l) + b3

    z = z_ref[...]
    win = jnp.exp(-z[:, 0:1] * dl_ref[...])
    k = mlp(zt_ref[...], w3h_ref[...], w3l_ref[...], b3_ref[...]) * jnp.concatenate([win] * HY_ORDER, axis=-1)
    lag0_back = mlp(z0t_ref[...], w3bh_ref[...], w3bl_ref[...], b3b_ref[...])[0:1, :]
    n = i * rows + lax.broadcasted_iota(jnp.int32, (rows, 1), 0)
    k = k + jnp.where(n == 0, lag0_back, 0.0)
    k = jnp.where(n == length, 0.0, k)
    kt_ref[...] = k

    @pl.when(i == 0)
    def _():
        ss_ref[...] = jnp.zeros_like(ss_ref)

    ss_ref[0:1, :] += jnp.sum(k * k, axis=0, keepdims=True)


def _filter_features(length):
    n = np.arange(2 * length)
    j = np.where(n < length, n, 2 * length - n).astype(np.float64)
    t = j / (length - 1)
    w = (2.0 * math.pi / length) * j
    f = np.linspace(1e-4, HY_BANDS - 1, HY_BANDS)
    z = np.zeros((2 * length, HY_EMB_PAD), np.float64)
    z[:, 0] = t
    z[:, 1:1 + HY_BANDS] = np.cos(w[:, None] * f[None, :])
    z[:, 1 + HY_BANDS:HY_EMB] = -np.sin(w[:, None] * f[None, :])
    return jnp.asarray(z, F32)


def _filter_call(length, w1, b1, f1, w2, b2, f2, w3, b3, d_hy):
    rows = TM
    hid = w1.shape[1]
    nc = HY_ORDER * d_hy
    ztab = _filter_features(length)
    ztab_t = ztab.T
    w1t = jnp.zeros((hid, HY_EMB_PAD), F32).at[:, :HY_EMB].set(w1.T)
    w3_hi = w3.astype(BF16)
    w3_lo = (w3 - w3_hi.astype(F32)).astype(BF16)
    deltas = jnp.asarray(np.abs(np.linspace(HY_MIN_DECAY, HY_MAX_DECAY, d_hy))[None, :], F32)
    n_fwd = length // rows
    kern = functools.partial(_filter_kernel, length=length, rows=rows)
    full = lambda shape: pl.BlockSpec(shape, lambda i: (0,) * len(shape))
    half = lambda i: (0, (i >= n_fwd).astype(jnp.int32))
    return pl.pallas_call(
        kern,
        grid=(2 * length // rows,),
        in_specs=[
            pl.BlockSpec((rows, HY_EMB_PAD), lambda i: (i, 0)),
            pl.BlockSpec((HY_EMB_PAD, rows), lambda i: (0, i)),
            pl.BlockSpec((HY_EMB_PAD, LANES), lambda i: (0, 0)),
            full((hid, HY_EMB_PAD)), full((hid, 1)), full((hid, 1)),
            full((hid, hid)), full((hid, 1)), full((hid, 1)),
            pl.BlockSpec((hid, nc), half), pl.BlockSpec((hid, nc), half), pl.BlockSpec((1, nc), half),
            pl.BlockSpec((hid, nc), lambda i: (0, 1)), pl.BlockSpec((hid, nc), lambda i: (0, 1)),
            pl.BlockSpec((1, nc), lambda i: (0, 1)),
            full((1, d_hy)),
        ],
        out_specs=[
            pl.BlockSpec((rows, nc), lambda i: (i, 0)),
            pl.BlockSpec((SUBLANES, nc), lambda i: (0, 0)),
        ],
        out_shape=[
            jax.ShapeDtypeStruct((2 * length, nc), F32),
            jax.ShapeDtypeStruct((SUBLANES, nc), F32),
        ],
        compiler_params=_cparams("arbitrary"),
        name="hyena_filter",
    )(ztab, ztab_t, ztab_t, w1t, b1[:, None], f1[:, None], w2.T, b2[:, None], f2[:, None],
      w3_hi, w3_lo, b3[None], w3_hi, w3_lo, b3[None], deltas)


def _dft_tables(length):
    n_fft = 2 * length
    n1 = n_fft // DFT_INNER
    n1h = n1 // 2
    unit = 2.0 * math.pi / n_fft
    k1 = jnp.arange(n1, dtype=jnp.int32)
    n2 = jnp.arange(DFT_INNER, dtype=jnp.int32)

    def cs(n1_count):
        n = DFT_INNER * jnp.arange(n1_count, dtype=jnp.int32)[None, None, :] + n2[:, None, None]
        th = ((k1[None, :, None] * n) % n_fft).astype(F32) * unit
        return jnp.cos(th), jnp.sin(th)

    c, s = cs(n1h)
    outer = jnp.concatenate([jnp.concatenate([c, s], -1), jnp.concatenate([-s, c], -1)], 1)
    ct, st = jnp.swapaxes(c, 1, 2) / n_fft, jnp.swapaxes(s, 1, 2) / n_fft
    outer_inv = jnp.concatenate([jnp.concatenate([ct, -st], -1), jnp.concatenate([st, ct], -1)], 1)
    c, s = cs(n1)
    outer_real = jnp.concatenate([c, -s], 1)
    m = np.arange(DFT_INNER)
    th = 2.0 * math.pi * np.outer(m, m) / DFT_INNER
    c, s = np.cos(th), np.sin(th)
    as_bf = lambda a: jnp.asarray(a, F32).astype(BF16)

    def per_step(w):
        w = w.astype(BF16).reshape(DFT_INNER // SUBLANES, SUBLANES, w.shape[1], w.shape[2])
        big = jnp.einsum('ijmr,jk->imkrj', w, jnp.eye(SUBLANES, dtype=BF16))
        return big.reshape(w.shape[0], w.shape[2] * SUBLANES, w.shape[3] * SUBLANES)

    return dict(outer=per_step(outer), outer_real=per_step(outer_real), outer_inv=per_step(outer_inv),
                inner=as_bf(np.block([[c, s], [-s, c]])), inner_inv=as_bf(np.block([[c, -s], [s, c]])), n1=n1)


def _dft_outer_kernel(x_ref, w_ref, o_ref):
    x = x_ref[...]
    r = _bdot(w_ref[0], x.reshape(-1, x.shape[-1]).astype(BF16))
    o_ref[...] = r.reshape(o_ref.shape)


def _dft_outer_call(x, lead, w, ct):
    c = x.shape[-1]
    m, r = w.shape[1] // SUBLANES, w.shape[2] // SUBLANES
    if lead is None:
        xspec = pl.BlockSpec((x.shape[0], SUBLANES, ct), lambda cc, i: (0, i, cc))
    else:
        xspec = pl.BlockSpec((None, 2, x.shape[2], SUBLANES, ct), lambda cc, i: (lead, 0, 0, i, cc))
    return pl.pallas_call(
        _dft_outer_kernel,
        grid=(c // ct, DFT_INNER // SUBLANES),
        in_specs=[xspec, pl.BlockSpec((1, m * SUBLANES, r * SUBLANES), lambda cc, i: (i, 0, 0))],
        out_specs=pl.BlockSpec((2, m // 2, SUBLANES, ct), lambda cc, i: (0, 0, i, cc)),
        out_shape=jax.ShapeDtypeStruct((2, m // 2, DFT_INNER, c), F32),
        compiler_params=_cparams("arbitrary", "arbitrary"),
        name="hyena_dft_outer",
    )(x, w)


def _filter_spectrum_kernel(a_ref, w_ref, o_ref):
    x = _bdot(w_ref[...], jnp.concatenate([a_ref[0, 0], a_ref[1, 0]], axis=0).astype(BF16))
    o_ref[0, 0] = x[:DFT_INNER].astype(BF16)
    o_ref[1, 0] = x[DFT_INNER:].astype(BF16)


def _filter_spectrum_call(a, w_inner):
    _, n1, _, c = a.shape
    blk = pl.BlockSpec((2, 1, DFT_INNER, c), lambda k: (0, k, 0, 0))
    return pl.pallas_call(
        _filter_spectrum_kernel,
        grid=(n1,),
        in_specs=[blk, pl.BlockSpec((2 * DFT_INNER, 2 * DFT_INNER), lambda k: (0, 0))],
        out_specs=blk,
        out_shape=jax.ShapeDtypeStruct(a.shape, BF16),
        compiler_params=_cparams("arbitrary"),
        name="hyena_filter_spectrum",
    )(a, w_inner)


def _spectral_kernel(a_ref, kf_ref, w_ref, wi_ref, o_ref):
    x = _bdot(w_ref[...], jnp.concatenate([a_ref[0, 0], a_ref[1, 0]], axis=0).astype(BF16))
    xr, xi = x[:DFT_INNER], x[DFT_INNER:]
    kr, ki = kf_ref[0, 0].astype(F32), kf_ref[1, 0].astype(F32)
    yr = xr * kr - xi * ki
    yi = xr * ki + xi * kr
    c = _bdot(wi_ref[...], jnp.concatenate([yr, yi], axis=0).astype(BF16))
    o_ref[0, 0] = c[:DFT_INNER]
    o_ref[1, 0] = c[DFT_INNER:]


def _spectral_call(a, kf, order, w_inner, w_inner_inv):
    _, n1, _, c = a.shape
    blk = pl.BlockSpec((2, 1, DFT_INNER, c), lambda k: (0, k, 0, 0))
    sq = pl.BlockSpec((2 * DFT_INNER, 2 * DFT_INNER), lambda k: (0, 0))
    return pl.pallas_call(
        _spectral_kernel,
        grid=(n1,),
        in_specs=[blk, pl.BlockSpec((2, 1, DFT_INNER, c), lambda k: (0, k, 0, order)), sq, sq],
        out_specs=blk,
        out_shape=jax.ShapeDtypeStruct(a.shape, F32),
        compiler_params=_cparams("arbitrary"),
        name="hyena_spectral",
    )(a, kf, w_inner, w_inner_inv)


def _dft_outer_inv_kernel(c_ref, w_ref, z_ref, gate_ref, skip_ref, ss_ref, o_ref):
    scale = lax.rsqrt(ss_ref[...] + EPS)
    cc = c_ref[...]
    y = _bdot(w_ref[0], cc.reshape(-1, cc.shape[-1]).astype(BF16)).reshape(o_ref.shape)
    o_ref[...] = gate_ref[...] * (scale * y + skip_ref[...] * z_ref[...])


def _dft_outer_inv_call(cp, w, zs, z_lead, gates, gate_lead, skip, ss):
    _, _, n1h, _, c = zs.shape
    n1 = cp.shape[1]
    zspec = lambda lead: pl.BlockSpec((None, 2, n1h, SUBLANES, c), lambda i: (lead, 0, 0, i, 0))
    vspec = pl.BlockSpec((1, c), lambda i: (0, 0))
    return pl.pallas_call(
        _dft_outer_inv_kernel,
        grid=(DFT_INNER // SUBLANES,),
        in_specs=[pl.BlockSpec((2, n1, SUBLANES, c), lambda i: (0, 0, i, 0)),
                  pl.BlockSpec((1, 2 * n1h * SUBLANES, 2 * n1 * SUBLANES), lambda i: (i, 0, 0)),
                  zspec(z_lead), zspec(gate_lead), vspec, vspec],
        out_specs=zspec(0),
        out_shape=jax.ShapeDtypeStruct((1, 2, n1h, DFT_INNER, c), F32),
        compiler_params=_cparams("arbitrary"),
        name="hyena_dft_outer_inv",
    )(cp, w, zs, gates, skip, ss)


def _ctx_dft_mats(n_ctx):
    n_fft = 2 * n_ctx
    f = np.arange(n_fft)
    th = 2.0 * math.pi * np.outer(f, np.arange(n_ctx)) / n_fft
    fz = np.concatenate([np.cos(th), -np.sin(th)], axis=0)
    th = 2.0 * math.pi * np.outer(f, f) / n_fft
    fk = np.concatenate([np.cos(th), -np.sin(th)], axis=0)
    th = 2.0 * math.pi * np.outer(np.arange(n_ctx), f) / n_fft
    fi = np.concatenate([np.cos(th), -np.sin(th)], axis=1) / n_fft
    as_bf = lambda a: jnp.asarray(a, F32).astype(BF16)
    return as_bf(fz), as_bf(fk), as_bf(fi)


def _hyctx_kernel(p0_ref, p1_ref, p2_ref, cw_ref, cb_ref, kt_ref, ss_ref, skip_ref,
                  fz_ref, fk_ref, fi_ref, o_ref, xe_s, *, d_hy):
    n_fft = fk_ref.shape[1]
    left = HY_CONV // 2

    def conv(p_ref, c):
        xe_s[0:HALO, :] = jnp.zeros((HALO, d_hy), F32)
        xe_s[HALO:HALO + TM, :] = p_ref[...]
        xe_s[HALO + TM:HALO + TM + HALO, :] = jnp.zeros((HALO, d_hy), F32)
        y = cb_ref[:, c * d_hy:(c + 1) * d_hy]
        for k in range(HY_CONV):
            y = y + cw_ref[k:k + 1, c * d_hy:(c + 1) * d_hy] * xe_s[pl.ds(HALO + k - left, TM), :]
        return y

    z = conv(p0_ref, 0)
    gates = (conv(p1_ref, 1), conv(p2_ref, 2))
    for o in range(HY_ORDER):
        cols = slice(o * d_hy, (o + 1) * d_hy)
        kf = _bdot(fk_ref[...], kt_ref[:, cols].astype(BF16))
        zf = _bdot(fz_ref[...], z.astype(BF16))
        zr, zi = zf[:n_fft], zf[n_fft:]
        kr, ki = kf[:n_fft], kf[n_fft:]
        yf = jnp.concatenate([zr * kr - zi * ki, zr * ki + zi * kr], axis=0)
        y = _bdot(fi_ref[...], yf.astype(BF16))
        scale = lax.rsqrt(ss_ref[0:1, cols] + EPS)
        z = gates[o] * (scale * y + skip_ref[o:o + 1, :] * z)
    o_ref[...] = z


def _hyctx_call(p, off_hy, cw, cb, kt, ss, skip, mats, *, batch, n_lat, n_ctx):
    d_hy = skip.shape[1]
    assert n_ctx == TM
    c0 = off_hy // d_hy
    r0 = batch * n_lat // TM
    fz, fk, fi = mats
    full = lambda a: pl.BlockSpec(a.shape, lambda b: (0,) * a.ndim)
    return pl.pallas_call(
        functools.partial(_hyctx_kernel, d_hy=d_hy),
        grid=(batch,),
        in_specs=[pl.BlockSpec((TM, d_hy), (lambda b, c=c0 + k: (r0 + b, c))) for k in range(3)]
        + [full(cw), full(cb), full(kt), full(ss), full(skip), full(fz), full(fk), full(fi)],
        out_specs=pl.BlockSpec((TM, d_hy), lambda b: (b, 0)),
        out_shape=jax.ShapeDtypeStruct((batch * n_ctx, d_hy), F32),
        scratch_shapes=[pltpu.VMEM((TM + 2 * HALO, d_hy), F32)],
        compiler_params=_cparams("arbitrary"),
        name="hyena_ctx",
    )(p, p, p, cw, cb, kt, ss, skip, fz, fk, fi)


def _merge_kernel(x_ref, hf_ref, hb_ref, pg_ref, hy_ref, sg_ref, gg_ref, w_ref, gate_ref, o_ref):
    a = (hf_ref[...] + hb_ref[...]) * _gelu(pg_ref[...])
    y = jnp.concatenate([_rms(a), _rms(hy_ref[...]), _rms(sg_ref[...])], axis=-1) * gg_ref[...]
    o_ref[...] = x_ref[...] + gate_ref[0] * _bdot(y.astype(BF16), w_ref[...])


def _merge_call(x, hf, hb, p, hy, sg, gg, w, gate, seg_of_tile, n_rows):
    d = x.shape[1]
    d_lru, d_hy, d_sg = hf.shape[1], hy.shape[1], sg.shape[1]
    row = lambda width, col=0: pl.BlockSpec((TM, width), lambda i: (i, col))
    return pl.pallas_call(
        _merge_kernel,
        grid=(n_rows // TM,),
        in_specs=[row(d), row(d_lru), row(d_lru), row(d_lru, 1), row(d_hy), row(d_sg),
                  pl.BlockSpec((1, d), lambda i: (0, 0)),
                  pl.BlockSpec((d, d), lambda i: (0, 0)),
                  pl.BlockSpec((1, 1, d), lambda i: (seg_of_tile(i), 0, 0))],
        out_specs=row(d),
        out_shape=jax.ShapeDtypeStruct((n_rows, d), F32),
        compiler_params=_cparams("arbitrary"),
        name="merge_out_proj",
    )(x, hf, hb, p, hy, sg, gg, w, gate)


def _pack_bf16_pairs(x, is_bf16_valued=False):
    half = x.shape[1] // 2
    lo, hi = x[:, :half], x[:, half:]
    if not is_bf16_valued:
        lo, hi = lo.astype(BF16).astype(F32), hi.astype(BF16).astype(F32)
    return (pltpu.bitcast(lo, jnp.uint32) >> 16) | (pltpu.bitcast(hi, jnp.uint32) & jnp.uint32(0xFFFF0000))


def _unpack_bf16_pairs(u):
    lo = pltpu.bitcast(u << 16, F32).astype(BF16)
    hi = pltpu.bitcast(u & jnp.uint32(0xFFFF0000), F32).astype(BF16)
    return lo, hi


def _router_kernel(x_ref, g_ref, sc_ref, sh_ref, rw_ref, rb_ref, f_ref, e_ref, gt_ref, m_ref):
    f = _rms(x_ref[...]) * g_ref[...] * (1.0 + sc_ref[0]) + sh_ref[0]
    f_hi = f.astype(BF16)
    f_ref[...] = f_hi
    f_lo = (f - f_hi.astype(F32)).astype(BF16)
    part = _bdot(f_hi, rw_ref[...])
    logits = part[:, :LANES] + part[:, LANES:] + _bdot(f_lo, rw_ref[:, :LANES]) + rb_ref[...]
    lane = lax.broadcasted_iota(jnp.int32, logits.shape, 1).astype(F32)
    work = logits
    e_out = jnp.zeros_like(logits)
    mask = jnp.zeros_like(logits)
    vals = []
    for k in range(TOP_K):
        mx = jnp.max(work, axis=-1, keepdims=True)
        idx = jnp.min(jnp.where(work == mx, lane, float(LANES)), axis=-1, keepdims=True)
        sel = lane == idx
        e_out = jnp.where(lane == float(k), idx, e_out)
        mask = jnp.where(sel, 1.0, mask)
        work = jnp.where(sel, -jnp.inf, work)
        vals.append(mx)
    ex = [jnp.exp(v - vals[0]) for v in vals]
    den = ex[0]
    for v in ex[1:]:
        den = den + v
    g_out = jnp.zeros_like(logits)
    for k in range(TOP_K):
        g_out = jnp.where(lane == float(k), ex[k] / den, g_out)
    e_ref[...] = e_out.astype(jnp.int32)
    gt_ref[...] = g_out
    m_ref[...] = mask


def _router_call(x, g, sc, sh, rw, rb, seg_of_tile, n_rows):
    d = x.shape[1]
    row = lambda width: pl.BlockSpec((TM, width), lambda i: (i, 0))
    mod = pl.BlockSpec((1, 1, d), lambda i: (seg_of_tile(i), 0, 0))
    return pl.pallas_call(
        _router_kernel,
        grid=(n_rows // TM,),
        in_specs=[row(d), pl.BlockSpec((1, d), lambda i: (0, 0)), mod, mod,
                  pl.BlockSpec((d, 2 * LANES), lambda i: (0, 0)), pl.BlockSpec((1, LANES), lambda i: (0, 0))],
        out_specs=[row(d), row(LANES), row(LANES), row(LANES)],
        out_shape=[jax.ShapeDtypeStruct((n_rows, d), BF16), jax.ShapeDtypeStruct((n_rows, LANES), jnp.int32),
                   jax.ShapeDtypeStruct((n_rows, LANES), F32), jax.ShapeDtypeStruct((n_rows, LANES), F32)],
        compiler_params=_cparams("arbitrary"),
        name="moe_router",
    )(x, g, sc, sh, rw, rb)


def _rank_kernel(m_ref, e_ref, tri_ref, utri_ref, ls_ref, meta_ref, tot_ref, carry_s):
    i = pl.program_id(0)

    @pl.when(i == 0)
    def _():
        carry_s[...] = jnp.zeros_like(carry_s)

    m = m_ref[...]
    rank = _bdot(tri_ref[...], m.astype(BF16))
    cnt = jnp.sum(m, axis=0, keepdims=True)
    seg = jnp.floor((cnt + (SEG_ROWS - 1)) * (1.0 / SEG_ROWS)) * SEG_ROWS
    seg8 = jnp.broadcast_to(seg, (SUBLANES, LANES))
    loc = _bdot(seg8.astype(BF16), utri_ref[...])[0:1, :]
    sd = rank + loc
    e = e_ref[...]
    lane = lax.broadcasted_iota(jnp.int32, sd.shape, 1)
    out = jnp.zeros_like(sd)
    for k in range(TOP_K):
        sk = jnp.sum(jnp.where(lane == e[:, k:k + 1], sd, 0.0), axis=-1, keepdims=True)
        out = jnp.where(lane == k, sk, out)
    ls_ref[...] = out.astype(jnp.int32)
    row = lax.broadcasted_iota(jnp.int32, (SUBLANES, LANES), 0)
    meta_ref[0] = jnp.where(row == 0, carry_s[...], jnp.where(row == 1, seg8, 0.0))
    carry_s[...] = carry_s[...] + seg8
    tot_ref[...] = carry_s[...]


def _rank_call(mask, e4):
    n = mask.shape[0]
    tri = jnp.asarray(np.tril(np.ones((TM, TM)), -1), BF16)
    utri = jnp.asarray(np.triu(np.ones((LANES, LANES)), 1), BF16)
    row = pl.BlockSpec((TM, LANES), lambda i: (i, 0))
    return pl.pallas_call(
        _rank_kernel,
        grid=(n // TM,),
        in_specs=[row, row, pl.BlockSpec((TM, TM), lambda i: (0, 0)), pl.BlockSpec((LANES, LANES), lambda i: (0, 0))],
        out_specs=[row, pl.BlockSpec((1, SUBLANES, LANES), lambda i: (i, 0, 0)),
                   pl.BlockSpec((SUBLANES, LANES), lambda i: (0, 0))],
        out_shape=[jax.ShapeDtypeStruct((n, LANES), jnp.int32), jax.ShapeDtypeStruct((n // TM, SUBLANES, LANES), F32),
                   jax.ShapeDtypeStruct((SUBLANES, LANES), F32)],
        scratch_shapes=[pltpu.VMEM((SUBLANES, LANES), F32)],
        compiler_params=_cparams("arbitrary"),
        name="moe_rank",
    )(mask, e4, tri, utri)


def _seg_copy(src, dst, src_row, dst_row, sem):
    return pltpu.make_async_copy(src.at[pl.ds(src_row, SEG_ROWS)], dst.at[pl.ds(dst_row, SEG_ROWS)], sem)


def _drain(copy, count):
    def body(r, c):
        copy.wait()
        return c

    lax.fori_loop(0, count, body, 0)


def _tile_segments(seg_ref, nch_ref, tile, move):
    off = jnp.int32(0)
    for e in range(N_EXPERTS):
        nc = nch_ref[tile * N_EXPERTS + e]
        g0 = seg_ref[tile * N_EXPERTS + e]

        def body(c, carry, off=off, g0=g0):
            move(pl.multiple_of(off + c * SEG_ROWS, SEG_ROWS), pl.multiple_of(g0 + c * SEG_ROWS, SEG_ROWS))
            return carry

        lax.fori_loop(0, nc, body, 0)
        off = off + nc * SEG_ROWS
    return off // SEG_ROWS


def _dispatch_kernel(seg_ref, nch_ref, pad_ref, f_ref, ls_ref, xs_hbm, xl_s, zero_s, sem, zsem):
    i = pl.program_id(0)
    rows = xl_s.shape[0]
    ls_t = ls_ref[...].astype(F32).T
    r_iota = lax.broadcasted_iota(jnp.int32, (rows, TM), 0).astype(F32)
    perm = jnp.zeros((rows, TM), F32)
    for k in range(TOP_K):
        perm = jnp.where(r_iota == ls_t[k:k + 1, :], 1.0, perm)
    xl_s[...] = _pack_bf16_pairs(_bdot(perm.astype(BF16), f_ref[...]), is_bf16_valued=True)

    n_chunks = _tile_segments(seg_ref, nch_ref, i, lambda lr, gr: _seg_copy(xl_s, xs_hbm, lr, gr, sem).start())

    @pl.when(i == 0)
    def _():
        zero_s[...] = jnp.zeros_like(zero_s)
        total = jnp.int32(0)
        for e in range(N_EXPERTS):
            nc = pad_ref[N_EXPERTS + e]
            g0 = pad_ref[e]

            def body(c, carry, g0=g0):
                _seg_copy(zero_s, xs_hbm, 0, pl.multiple_of(g0 + c * SEG_ROWS, SEG_ROWS), zsem).start()
                return carry

            lax.fori_loop(0, nc, body, 0)
            total = total + nc
        _drain(_seg_copy(zero_s, xs_hbm, 0, 0, zsem), total)

        def block_copy(b):
            return pltpu.make_async_copy(zero_s, xs_hbm.at[pl.ds(pl.multiple_of(b * MOE_BLOCK, MOE_BLOCK), MOE_BLOCK)], zsem)

        n_used = pad_ref[2 * N_EXPERTS]
        n_blocks = xs_hbm.shape[0] // MOE_BLOCK

        def start_block(b, carry):
            block_copy(b).start()
            return carry

        def wait_block(b, carry):
            block_copy(b).wait()
            return carry

        lax.fori_loop(n_used, n_blocks, start_block, 0)
        lax.fori_loop(n_used, n_blocks, wait_block, 0)

    _drain(_seg_copy(xl_s, xs_hbm, 0, 0, sem), n_chunks)


def _dispatch_call(seg, nch, pad, f, ls, n_rows):
    n, d = f.shape
    return pl.pallas_call(
        _dispatch_kernel,
        grid_spec=pltpu.PrefetchScalarGridSpec(
            num_scalar_prefetch=3,
            grid=(n // TM,),
            in_specs=[pl.BlockSpec((TM, d), lambda i, *_: (i, 0)), pl.BlockSpec((TM, LANES), lambda i, *_: (i, 0))],
            out_specs=pl.BlockSpec(memory_space=pl.ANY),
            scratch_shapes=[pltpu.VMEM((LOCAL_ROWS, d // 2), jnp.uint32), pltpu.VMEM((MOE_BLOCK, d // 2), jnp.uint32),
                            pltpu.SemaphoreType.DMA(()), pltpu.SemaphoreType.DMA(())],
        ),
        out_shape=jax.ShapeDtypeStruct((n_rows, d // 2), jnp.uint32),
        compiler_params=_cparams("arbitrary", row_dma=True),
        name="moe_dispatch",
    )(seg, nch, pad, f, ls)


def _expert_kernel(be_ref, nu_ref, x_ref, w1_ref, b1_ref, w2_ref, b2_ref, o_ref, w1_s, w2_s):
    i = pl.program_id(0)
    d_e = w2_ref.shape[2]
    half = x_ref.shape[1]
    live = i < nu_ref[0]
    new_expert = jnp.logical_or(i == 0, be_ref[i] != be_ref[jnp.maximum(i - 1, 0)])

    @pl.when(jnp.logical_and(live, new_expert))
    def _():
        w1_s[...] = w1_ref[0, 0].astype(BF16)
        w2_s[...] = w2_ref[0, 0].astype(BF16)

    @pl.when(live)
    def _():
        lo, hi = _unpack_bf16_pairs(x_ref[...])
        gu = _bdot(lo, w1_s[:half, :]) + _bdot(hi, w1_s[half:, :]) + b1_ref[0, 0]
        glu = jnp.minimum(gu[:, :d_e], SWIGLU_LIMIT)
        lin = jnp.clip(gu[:, d_e:], -SWIGLU_LIMIT, SWIGLU_LIMIT)
        act = glu * _sigmoid(SWIGLU_ALPHA * glu) * (lin + 1.0)
        o_ref[...] = _pack_bf16_pairs(_bdot(act.astype(BF16), w2_s[...]) + b2_ref[0, 0])

    @pl.when(i >= nu_ref[0])
    def _():
        o_ref[...] = jnp.zeros_like(o_ref)


def _expert_call(block_e, n_used, xs, w1, b1, w2, b2, layer):
    n_rows, half = xs.shape
    depth, n_e, d, d_gu = w1.shape
    d_e = w2.shape[2]
    used = lambda i, nu: jnp.minimum(i, nu[0] - 1)
    return pl.pallas_call(
        _expert_kernel,
        grid_spec=pltpu.PrefetchScalarGridSpec(
            num_scalar_prefetch=2,
            grid=(n_rows // MOE_BLOCK,),
            in_specs=[
                pl.BlockSpec((MOE_BLOCK, half), lambda i, be, nu: (used(i, nu), 0)),
                pl.BlockSpec((1, 1, d, d_gu), lambda i, be, nu: (layer, be[i], 0, 0)),
                pl.BlockSpec((1, 1, 1, d_gu), lambda i, be, nu: (layer, be[i], 0, 0)),
                pl.BlockSpec((1, 1, d_e, d), lambda i, be, nu: (layer, be[i], 0, 0)),
                pl.BlockSpec((1, 1, 1, d), lambda i, be, nu: (layer, be[i], 0, 0)),
            ],
            out_specs=pl.BlockSpec((MOE_BLOCK, half), lambda i, be, nu: (i, 0)),
            scratch_shapes=[pltpu.VMEM((d, d_gu), BF16), pltpu.VMEM((d_e, d), BF16)],
        ),
        out_shape=jax.ShapeDtypeStruct((n_rows, half), jnp.uint32),
        compiler_params=_cparams("arbitrary"),
        name="moe_experts",
    )(block_e, n_used, xs, w1, b1.reshape(depth, n_e, 1, d_gu), w2, b2.reshape(depth, n_e, 1, d))


def _combine_kernel(seg_ref, nch_ref, x_ref, ls_ref, g_ref, gate_ref, ys_hbm, o_ref, yl_s, sem):
    i = pl.program_id(0)
    n = pl.num_programs(0)
    rows = yl_s.shape[1]

    def fetch(tile, b):
        return _tile_segments(seg_ref, nch_ref, tile,
                              lambda lr, gr: _seg_copy(ys_hbm, yl_s.at[b], gr, lr, sem.at[b]).start())

    @pl.when(i == 0)
    def _():
        yl_s[...] = jnp.zeros_like(yl_s)
        fetch(0, 0)

    @pl.when(i + 1 < n)
    def _():
        fetch(i + 1, (i + 1) % 2)

    cur = i % 2
    total = jnp.int32(0)
    for e in range(N_EXPERTS):
        total = total + nch_ref[i * N_EXPERTS + e]
    _drain(_seg_copy(ys_hbm, yl_s.at[cur], 0, 0, sem.at[cur]), total)

    ls = ls_ref[...]
    g = g_ref[...]
    r_iota = lax.broadcasted_iota(jnp.int32, (TM, rows), 1)
    wsel = jnp.zeros((TM, rows), F32)
    for k in range(TOP_K):
        wsel = jnp.where(r_iota == ls[:, k:k + 1], g[:, k:k + 1], wsel)
    wsel = wsel.astype(BF16)
    lo, hi = _unpack_bf16_pairs(yl_s[cur])
    acc = jnp.concatenate([_bdot(wsel, lo), _bdot(wsel, hi)], axis=-1)
    o_ref[...] = x_ref[...] + gate_ref[0] * acc


def _combine_call(seg, nch, x, ls, g4, gate, ys, seg_of_tile, n_tok):
    d = x.shape[1]
    row = lambda width: pl.BlockSpec((TM, width), lambda i, *_: (i, 0))
    return pl.pallas_call(
        _combine_kernel,
        grid_spec=pltpu.PrefetchScalarGridSpec(
            num_scalar_prefetch=2,
            grid=(n_tok // TM,),
            in_specs=[row(d), row(LANES), row(LANES),
                      pl.BlockSpec((1, 1, d), lambda i, *_: (seg_of_tile(i), 0, 0)),
                      pl.BlockSpec(memory_space=pl.ANY)],
            out_specs=row(d),
            scratch_shapes=[pltpu.VMEM((2, LOCAL_ROWS, d // 2), jnp.uint32), pltpu.SemaphoreType.DMA((2,))],
        ),
        out_shape=jax.ShapeDtypeStruct((n_tok, d), F32),
        compiler_params=_cparams("arbitrary", row_dma=True),
        name="moe_combine",
    )(seg, nch, x, ls, g4, gate, ys)


def _moe(x, norm_g, sc, sh, gate, rw, rb, w1, b1, w2, b2, layer, seg_of_row_tile, n_tok):
    n_tiles = n_tok // TM
    f, e4, g4, mask = _router_call(x, norm_g, sc, sh, rw, rb, seg_of_row_tile(TM), n_tok)
    ls, meta, tot = _rank_call(mask, e4)
    rows_e = tot[0, :N_EXPERTS].astype(jnp.int32)
    region = (rows_e + MOE_BLOCK - 1) // MOE_BLOCK * MOE_BLOCK
    pend = jnp.cumsum(region)
    pstart = pend - region
    max_rows = n_tok * TOP_K + n_tiles * N_EXPERTS * (SEG_ROWS - 1) + N_EXPERTS * (MOE_BLOCK - 1)
    n_blocks = -(-max_rows // MOE_BLOCK)
    first_row = jnp.arange(n_blocks, dtype=jnp.int32) * MOE_BLOCK
    block_e = jnp.minimum(jnp.sum((pend[None, :] <= first_row[:, None]).astype(jnp.int32), axis=1), N_EXPERTS - 1)
    n_used = (pend[-1:] // MOE_BLOCK).astype(jnp.int32)
    seg = (pstart[None, :] + meta[:, 0, :N_EXPERTS].astype(jnp.int32)).reshape(-1)
    nch = (meta[:, 1, :N_EXPERTS].astype(jnp.int32) // SEG_ROWS).reshape(-1)
    pad = jnp.concatenate([pstart + rows_e, (region - rows_e) // SEG_ROWS, n_used])
    xs = _dispatch_call(seg, nch, pad, f, ls, n_blocks * MOE_BLOCK)
    ys = _expert_call(block_e, n_used, xs, w1, b1, w2, b2, layer)
    return _combine_call(seg, nch, x, ls, g4, gate, ys, seg_of_row_tile(TM), n_tok)


def _final_kernel(x_ref, g_ref, o_ref):
    o_ref[...] = _rms(x_ref[...]) * g_ref[...]


def _final_call(x, g, n_rows):
    d = x.shape[1]
    return pl.pallas_call(
        _final_kernel,
        grid=(n_rows // TM,),
        in_specs=[pl.BlockSpec((TM, d), lambda i: (i, 0)), pl.BlockSpec((1, d), lambda i: (0, 0))],
        out_specs=pl.BlockSpec((TM, d), lambda i: (i, 0)),
        out_shape=jax.ShapeDtypeStruct((n_rows, d), F32),
        compiler_params=_cparams("arbitrary"),
        name="final_norm",
    )(x, g)


def _blockdiag(w):
    h, hd, _ = w.shape
    return jnp.einsum('hij,hg->higj', w, jnp.eye(h, dtype=w.dtype)).reshape(h * hd, h * hd)


def kernel(x, c, ctx, c_ctx, ada_w, ada_b, norm_mix_g, w_in, lru_conv_w, lru_conv_b, lru_wa, lru_ba, lru_wx, lru_bx, lru_lam, hy_conv_w, hy_conv_b, hf_w1, hf_b1, hf_f1, hf_w2, hf_b2, hf_f2, hf_w3, hf_b3, hy_skip, sg_ln_g, sg_ln_b, sg_w, sg_b, grp_norm_g, w_out, norm_ffn_g, router_w, router_b, moe_w1, moe_b1, moe_w2, moe_b2, final_norm_g):
    batch, n_lat, d = x.shape
    n_ctx = ctx.shape[1]
    depth = ada_w.shape[0]
    d_lru = lru_conv_w.shape[2]
    d_hy = hy_skip.shape[2]
    d_sg = sg_ln_g.shape[1]
    off_hy = 2 * d_lru
    off_sg = off_hy + 3 * d_hy
    assert batch == 2 and n_ctx == TM and n_lat % (TM * 2) == 0 and batch + 1 <= SUBLANES
    n_lat_total = batch * n_lat
    n_all = n_lat_total + batch * n_ctx

    def seg_of_row_tile(rows):
        per_seq = n_lat // rows
        return lambda i: jnp.minimum(i // per_seq, batch)

    cond = jnp.zeros((SUBLANES, d), F32).at[:batch].set(c).at[batch].set(c_ctx)
    mods = _ada_call(cond, ada_w, ada_b)

    mats = _dft_tables(n_lat)
    ctx_mats = _ctx_dft_mats(n_ctx)
    n1 = mats["n1"]
    sg_bias_shape = (SG_CHUNK, d_sg)

    xt = jnp.concatenate([x.reshape(n_lat_total, d), ctx.reshape(batch * n_ctx, d)], axis=0)
    for l in range(depth):
        ctx_out = l < depth - 1
        m = mods[l, :batch + 1].reshape(batch + 1, N_MOD, d)
        mod = [m[:, j][:, None, :] for j in range(N_MOD)]

        p = _inproj_call(xt, norm_mix_g[l][None], mod[1], mod[0], w_in[l].astype(BF16), seg_of_row_tile(TM))

        hs = []
        for dr, rev in enumerate((False, True)):
            wg = jnp.concatenate([_blockdiag(lru_wa[l, dr]), _blockdiag(lru_wx[l, dr])], axis=1).astype(BF16)
            bg = jnp.concatenate([lru_ba[l, dr], lru_bx[l, dr]])[None]
            hs.append(_lru_call(p, lru_conv_w[l], lru_conv_b[l][None], wg, bg, lru_lam[l, dr][None],
                                reverse=rev, batch=batch, n_lat=n_lat, n_ctx=n_ctx))

        sg_bias = jnp.broadcast_to(sg_b[l].T[:, :, None], (SG_CHUNK, SG_HEADS, d_sg // SG_HEADS)).reshape(sg_bias_shape)
        sg = _sg_call(p, off_sg, sg_ln_g[l][None], sg_ln_b[l][None], sg_w[l].astype(BF16), sg_bias)

        filt = (hf_w1[l], hf_b1[l], hf_f1[l], hf_w2[l], hf_b2[l], hf_f2[l], hf_w3[l], hf_b3[l])
        kt, ss = _filter_call(n_lat, *filt, d_hy)
        ka = _dft_outer_call(kt.reshape(n1, DFT_INNER, HY_ORDER * d_hy), None, mats["outer_real"], d_hy)
        kf = _filter_spectrum_call(ka, mats["inner"])
        vxx = _hyconv_call(p, off_hy, hy_conv_w[l], hy_conv_b[l][None], batch=batch, n_lat=n_lat)
        vxx = vxx.reshape(3, batch, n1 // 2, DFT_INNER, d_hy)
        zs, z_lead = vxx, 0
        for o in range(HY_ORDER):
            a = _dft_outer_call(zs, z_lead, mats["outer"], d_hy)
            cp = _spectral_call(a, kf, o, mats["inner"], mats["inner_inv"])
            zs = _dft_outer_inv_call(cp, mats["outer_inv"], zs, z_lead, vxx, 1 + o, hy_skip[l, o][None],
                                     ss[0:1, o * d_hy:(o + 1) * d_hy])
            z_lead = 0
        hy = zs.reshape(n_lat_total, d_hy)
        if ctx_out:
            ktc, ssc = _filter_call(n_ctx, *filt, d_hy)
            hyc = _hyctx_call(p, off_hy, hy_conv_w[l], hy_conv_b[l][None], ktc, ssc, hy_skip[l], ctx_mats,
                              batch=batch, n_lat=n_lat, n_ctx=n_ctx)
            hy = jnp.concatenate([hy, hyc], axis=0)

        n_tok = n_all if ctx_out else n_lat_total
        xt = _merge_call(xt, hs[0], hs[1], p, hy, sg, grp_norm_g[l][None], w_out[l].astype(BF16), mod[2],
                         seg_of_row_tile(TM), n_tok)

        rw = jnp.zeros((d, LANES), F32).at[:, :N_EXPERTS].set(router_w[l])
        rw_hi = rw.astype(BF16)
        rw = jnp.concatenate([rw_hi, (rw - rw_hi.astype(F32)).astype(BF16)], axis=1)
        rb = jnp.full((1, LANES), -1e30, F32).at[0, :N_EXPERTS].set(router_b[l])
        xt = _moe(xt, norm_ffn_g[l][None], mod[4], mod[3], mod[5], rw, rb, moe_w1, moe_b1, moe_w2, moe_b2,
                  l, seg_of_row_tile, n_tok)

    return _final_call(xt, final_norm_g[None], n_lat_total).reshape(batch, n_lat, d)
```

```python
import functools
import math

import numpy as np
import jax
import jax.numpy as jnp
from jax import lax
from jax.experimental import pallas as pl
from jax.experimental.pallas import tpu as pltpu

F32 = jnp.float32
BF16 = jnp.bfloat16

EPS = 1e-6
N_MOD = 6
LRU_HEADS = 8
LRU_CONV = 4
LRU_C = 8.0
HY_ORDER = 2
HY_CONV = 3
HY_BANDS = 8
HY_EMB = 1 + 2 * HY_BANDS
HY_EMB_PAD = 32
HY_MIN_DECAY = math.log(1e-2) / 1.5
HY_MAX_DECAY = math.log(1e-2) / 0.3
SG_CHUNK = 128
SG_HEADS = 4
N_EXPERTS = 32
TOP_K = 4
MOE_BLOCK = 256
SWIGLU_LIMIT = 7.0
SWIGLU_ALPHA = 1.702

LANES = 128
SUBLANES = 8
TM = 256
HALO = SUBLANES
DFT_INNER = 128
K1_PER_STEP = 4
SEG_ROWS = SUBLANES
LOCAL_ROWS = TM * TOP_K + N_EXPERTS * SEG_ROWS
VMEM_LIMIT = 48 * 1024 * 1024


def _cparams(*sem, vmem=VMEM_LIMIT, row_dma=False):
    return pltpu.CompilerParams(dimension_semantics=sem, vmem_limit_bytes=vmem, disable_bounds_checks=row_dma)


def _bdot(a, b):
    return jnp.dot(a, b, preferred_element_type=F32)


def _hdot(a, b):
    return jnp.dot(a, b, preferred_element_type=F32, precision=lax.Precision.HIGHEST)


def _gelu(x):
    return 0.5 * x * (1.0 + jnp.tanh(math.sqrt(2.0 / math.pi) * (x + 0.044715 * (x * x * x))))


def _sigmoid(x):
    return 0.5 * jnp.tanh(0.5 * x) + 0.5


def _rms(x):
    return x * lax.rsqrt(jnp.mean(x * x, axis=-1, keepdims=True) + EPS)


def _ada_kernel(s_ref, w_ref, b_ref, o_ref):
    s = s_ref[...]
    s = s * _sigmoid(s)
    o_ref[0] = _bdot(s.astype(BF16), w_ref[0].astype(BF16)) + b_ref[0]


def _ada_call(cond, ada_w, ada_b):
    depth, d, n = ada_w.shape
    tn = 1536
    return pl.pallas_call(
        _ada_kernel,
        grid=(depth, n // tn),
        in_specs=[
            pl.BlockSpec((SUBLANES, d), lambda l, j: (0, 0)),
            pl.BlockSpec((1, d, tn), lambda l, j: (l, 0, j)),
            pl.BlockSpec((1, 1, tn), lambda l, j: (l, 0, j)),
        ],
        out_specs=pl.BlockSpec((1, SUBLANES, tn), lambda l, j: (l, 0, j)),
        out_shape=jax.ShapeDtypeStruct((depth, SUBLANES, n), F32),
        compiler_params=_cparams("arbitrary", "arbitrary"),
        name="ada_mod",
    )(cond, ada_w, ada_b.reshape(depth, 1, n))


def _inproj_kernel(x_ref, g_ref, sc_ref, sh_ref, w_ref, o_ref):
    h = _rms(x_ref[...]) * g_ref[...] * (1.0 + sc_ref[0]) + sh_ref[0]
    o_ref[...] = _bdot(h.astype(BF16), w_ref[...])


def _inproj_call(x, g, sc, sh, w, seg_of_tile):
    t, d = x.shape
    n = w.shape[1]
    tn = n // 2
    return pl.pallas_call(
        _inproj_kernel,
        grid=(2, t // TM),
        in_specs=[
            pl.BlockSpec((TM, d), lambda j, i: (i, 0)),
            pl.BlockSpec((1, d), lambda j, i: (0, 0)),
            pl.BlockSpec((1, 1, d), lambda j, i: (seg_of_tile(i), 0, 0)),
            pl.BlockSpec((1, 1, d), lambda j, i: (seg_of_tile(i), 0, 0)),
            pl.BlockSpec((d, tn), lambda j, i: (0, j)),
        ],
        out_specs=pl.BlockSpec((TM, tn), lambda j, i: (i, j)),
        out_shape=jax.ShapeDtypeStruct((t, n), F32),
        compiler_params=_cparams("arbitrary", "arbitrary"),
        name="in_proj",
    )(x, g, sc, sh, w)


def _lru_kernel(x_ref, xp_ref, xn_ref, cw_ref, cb_ref, wg_ref, bg_ref, lam_ref, o_ref,
                xe_s, a_s, b_s, h_s, *, reverse, n_chunks, d_lru):
    s = pl.program_id(1)
    j = (n_chunks - s) if reverse else (s - 1)
    has_prev = jnp.logical_and(s > 0, j > 0)
    has_next = jnp.logical_and(s > 0, j < n_chunks - 1)

    @pl.when(s == 0)
    def _():
        h_s[...] = jnp.zeros_like(h_s)

    xe_s[0:HALO, :] = jnp.where(has_prev, xp_ref[...], 0.0)
    xe_s[HALO:HALO + TM, :] = x_ref[...]
    xe_s[HALO + TM:HALO + TM + HALO, :] = jnp.where(has_next, xn_ref[...], 0.0)
    left = LRU_CONV // 2
    xc = cb_ref[...]
    for k in range(LRU_CONV):
        xc = xc + cw_ref[k:k + 1, :] * xe_s[pl.ds(HALO + k - left, TM), :]

    g = _bdot(xc.astype(BF16), wg_ref[...]) + bg_ref[...]
    r = _sigmoid(g[:, :d_lru])
    ig = _sigmoid(g[:, d_lru:])
    lam = lam_ref[...]
    sp = jnp.maximum(-lam, 0.0) + jnp.log1p(jnp.exp(-jnp.abs(lam)))
    log_a = (-LRU_C * r) * sp
    a = jnp.exp(log_a)
    a_s[...] = a
    b_s[...] = jnp.sqrt(-jnp.tanh(log_a) * (a * a + 1.0)) * (ig * xc)

    def step(t, h):
        tt = (TM - 1 - t) if reverse else t
        h = a_s[pl.ds(tt, 1), :] * h + b_s[pl.ds(tt, 1), :]
        o_ref[pl.ds(tt, 1), :] = h
        return h

    h_s[0:1, :] = lax.fori_loop(0, TM, step, h_s[0:1, :], unroll=8)


def _lru_call(p, cw, cb, wg, bg, lam, *, reverse, batch, n_lat, n_ctx):
    t = p.shape[0]
    d_lru = cw.shape[1]
    n_chunks = n_lat // TM
    assert n_ctx == TM
    ctx0 = batch * n_chunks
    per = TM // HALO
    last = t // HALO - 1

    def rb(b, s):
        lat = b * n_chunks + ((n_chunks - s) if reverse else (s - 1))
        return jnp.where(s == 0, ctx0 + b, lat)

    kern = functools.partial(_lru_kernel, reverse=reverse, n_chunks=n_chunks, d_lru=d_lru)
    return pl.pallas_call(
        kern,
        grid=(batch, n_chunks + 1),
        in_specs=[
            pl.BlockSpec((TM, d_lru), lambda b, s: (rb(b, s), 0)),
            pl.BlockSpec((HALO, d_lru), lambda b, s: (jnp.maximum(rb(b, s) * per - 1, 0), 0)),
            pl.BlockSpec((HALO, d_lru), lambda b, s: (jnp.minimum((rb(b, s) + 1) * per, last), 0)),
            pl.BlockSpec((LRU_CONV, d_lru), lambda b, s: (0, 0)),
            pl.BlockSpec((1, d_lru), lambda b, s: (0, 0)),
            pl.BlockSpec((d_lru, 2 * d_lru), lambda b, s: (0, 0)),
            pl.BlockSpec((1, 2 * d_lru), lambda b, s: (0, 0)),
            pl.BlockSpec((1, d_lru), lambda b, s: (0, 0)),
        ],
        out_specs=pl.BlockSpec((TM, d_lru), lambda b, s: (rb(b, s), 0)),
        out_shape=jax.ShapeDtypeStruct((t, d_lru), F32),
        scratch_shapes=[
            pltpu.VMEM((TM + 2 * HALO, d_lru), F32),
            pltpu.VMEM((TM, d_lru), F32),
            pltpu.VMEM((TM, d_lru), F32),
            pltpu.VMEM((SUBLANES, d_lru), F32),
        ],
        compiler_params=_cparams("arbitrary", "arbitrary"),
        name="rglru_rev" if reverse else "rglru_fwd",
    )(p, p, p, cw, cb, wg, bg, lam)


def _sg_kernel(u0_ref, u1_ref, v0_ref, v1_ref, lg_ref, lb_ref, ws_ref, bias_ref, o_ref):
    u = _gelu(jnp.concatenate([u0_ref[...], u1_ref[...]], axis=-1))
    v = _gelu(jnp.concatenate([v0_ref[...], v1_ref[...]], axis=-1))
    mu = jnp.mean(v, axis=-1, keepdims=True)
    dv = v - mu
    var = jnp.mean(dv * dv, axis=-1, keepdims=True)
    vb = ((dv * lax.rsqrt(var + EPS)) * lg_ref[...] + lb_ref[...]).astype(BF16)
    hd = vb.shape[1] // SG_HEADS
    for ch in range(TM // SG_CHUNK):
        r0 = ch * SG_CHUNK
        ys = [_bdot(ws_ref[g], vb[r0:r0 + SG_CHUNK, g * hd:(g + 1) * hd]) for g in range(SG_HEADS)]
        y = jnp.concatenate(ys, axis=-1) + bias_ref[...]
        o_ref[r0:r0 + SG_CHUNK, :] = u[r0:r0 + SG_CHUNK, :] * y


def _sg_call(p, off_sg, lg, lb, ws, bias):
    t = p.shape[0]
    d_sg = lg.shape[1]
    half = d_sg // 2
    c0 = off_sg // half
    specs = [pl.BlockSpec((TM, half), (lambda i, c=c0 + k: (i, c))) for k in range(4)]
    return pl.pallas_call(
        _sg_kernel,
        grid=(t // TM,),
        in_specs=specs + [
            pl.BlockSpec((1, d_sg), lambda i: (0, 0)),
            pl.BlockSpec((1, d_sg), lambda i: (0, 0)),
            pl.BlockSpec((SG_HEADS, SG_CHUNK, SG_CHUNK), lambda i: (0, 0, 0)),
            pl.BlockSpec((SG_CHUNK, d_sg), lambda i: (0, 0)),
        ],
        out_specs=pl.BlockSpec((TM, d_sg), lambda i: (i, 0)),
        out_shape=jax.ShapeDtypeStruct((t, d_sg), F32),
        compiler_params=_cparams("arbitrary"),
        name="spatial_gating",
    )(p, p, p, p, lg, lb, ws, bias)


def _hyconv_kernel(x_ref, xp_ref, xn_ref, cw_ref, cb_ref, o_ref, xe_s, *, n_chunks):
    j = pl.program_id(1) % n_chunks
    xe_s[0:HALO, :] = jnp.where(j > 0, xp_ref[...], 0.0)
    xe_s[HALO:HALO + TM, :] = x_ref[...]
    xe_s[HALO + TM:HALO + TM + HALO, :] = jnp.where(j < n_chunks - 1, xn_ref[...], 0.0)
    left = HY_CONV // 2
    y = cb_ref[...]
    for k in range(HY_CONV):
        y = y + cw_ref[k:k + 1, :] * xe_s[pl.ds(HALO + k - left, TM), :]
    o_ref[0] = y


def _hyconv_call(p, off_hy, cw, cb, *, batch, n_lat):
    t = p.shape[0]
    d_hy = cw.shape[1] // 3
    n_chunks = n_lat // TM
    c0 = off_hy // d_hy
    per = TM // HALO
    last = t // HALO - 1
    kern = functools.partial(_hyconv_kernel, n_chunks=n_chunks)
    return pl.pallas_call(
        kern,
        grid=(3, batch * n_chunks),
        in_specs=[
            pl.BlockSpec((TM, d_hy), lambda c, i: (i, c0 + c)),
            pl.BlockSpec((HALO, d_hy), lambda c, i: (jnp.maximum(i * per - 1, 0), c0 + c)),
            pl.BlockSpec((HALO, d_hy), lambda c, i: (jnp.minimum((i + 1) * per, last), c0 + c)),
            pl.BlockSpec((HY_CONV, d_hy), lambda c, i: (0, c)),
            pl.BlockSpec((1, d_hy), lambda c, i: (0, c)),
        ],
        out_specs=pl.BlockSpec((1, TM, d_hy), lambda c, i: (c, i, 0)),
        out_shape=jax.ShapeDtypeStruct((3, batch * n_lat, d_hy), F32),
        scratch_shapes=[pltpu.VMEM((TM + 2 * HALO, d_hy), F32)],
        compiler_params=_cparams("arbitrary", "arbitrary"),
        name="hyena_shortconv",
    )(p, p, p, cw, cb)


def _filter_kernel(z_ref, zt_ref, z0t_ref, w1_ref, b1_ref, f1_ref, w2_ref, b2_ref, f2_ref, w3h_ref, w3l_ref, b3_ref,
                   w3bh_ref, w3bl_ref, b3b_ref, dl_ref, kt_ref, ss_ref, *, length, rows):
    i = pl.program_id(0)

    def mlp(zt, w3h, w3l, b3):
        h = jnp.sin(f1_ref[...] * (_hdot(w1_ref[...], zt) + b1_ref[...]))
        h = jnp.sin(f2_ref[...] * (_hdot(w2_ref[...], h) + b2_ref[...])).T
        h_hi = h.astype(BF16)
        h_lo = (h - h_hi.astype(F32)).astype(BF16)
        return _bdot(h_hi, w3h) + _bdot(h_lo, w3h) + _bdot(h_hi, w3l) + b3

    z = z_ref[...]
    win = jnp.exp(-z[:, 0:1] * dl_ref[...])
    k = mlp(zt_ref[...], w3h_ref[...], w3l_ref[...], b3_ref[...]) * jnp.concatenate([win] * HY_ORDER, axis=-1)
    lag0_back = mlp(z0t_ref[...], w3bh_ref[...], w3bl_ref[...], b3b_ref[...])[0:1, :]
    n = i * rows + lax.broadcasted_iota(jnp.int32, (rows, 1), 0)
    k = k + jnp.where(n == 0, lag0_back, 0.0)
    k = jnp.where(n == length, 0.0, k)
    kt_ref[...] = k

    @pl.when(i == 0)
    def _():
        ss_ref[...] = jnp.zeros_like(ss_ref)

    ss_ref[0:1, :] += jnp.sum(k * k, axis=0, keepdims=True)


def _filter_features(length):
    n = np.arange(2 * length)
    j = np.where(n < length, n, 2 * length - n).astype(np.float64)
    t = j / (length - 1)
    w = (2.0 * math.pi / length) * j
    f = np.linspace(1e-4, HY_BANDS - 1, HY_BANDS)
    z = np.zeros((2 * length, HY_EMB_PAD), np.float64)
    z[:, 0] = t
    z[:, 1:1 + HY_BANDS] = np.cos(w[:, None] * f[None, :])
    z[:, 1 + HY_BANDS:HY_EMB] = -np.sin(w[:, None] * f[None, :])
    return jnp.asarray(z, F32)


def _filter_call(length, w1, b1, f1, w2, b2, f2, w3, b3, d_hy):
    rows = TM
    hid = w1.shape[1]
    nc = HY_ORDER * d_hy
    ztab = _filter_features(length)
    ztab_t = ztab.T
    w1t = jnp.zeros((hid, HY_EMB_PAD), F32).at[:, :HY_EMB].set(w1.T)
    w3_hi = w3.astype(BF16)
    w3_lo = (w3 - w3_hi.astype(F32)).astype(BF16)
    deltas = jnp.asarray(np.abs(np.linspace(HY_MIN_DECAY, HY_MAX_DECAY, d_hy))[None, :], F32)
    n_fwd = length // rows
    kern = functools.partial(_filter_kernel, length=length, rows=rows)
    full = lambda shape: pl.BlockSpec(shape, lambda i: (0,) * len(shape))
    half = lambda i: (0, (i >= n_fwd).astype(jnp.int32))
    return pl.pallas_call(
        kern,
        grid=(2 * length // rows,),
        in_specs=[
            pl.BlockSpec((rows, HY_EMB_PAD), lambda i: (i, 0)),
            pl.BlockSpec((HY_EMB_PAD, rows), lambda i: (0, i)),
            pl.BlockSpec((HY_EMB_PAD, LANES), lambda i: (0, 0)),
            full((hid, HY_EMB_PAD)), full((hid, 1)), full((hid, 1)),
            full((hid, hid)), full((hid, 1)), full((hid, 1)),
            pl.BlockSpec((hid, nc), half), pl.BlockSpec((hid, nc), half), pl.BlockSpec((1, nc), half),
            pl.BlockSpec((hid, nc), lambda i: (0, 1)), pl.BlockSpec((hid, nc), lambda i: (0, 1)),
            pl.BlockSpec((1, nc), lambda i: (0, 1)),
            full((1, d_hy)),
        ],
        out_specs=[
            pl.BlockSpec((rows, nc), lambda i: (i, 0)),
            pl.BlockSpec((SUBLANES, nc), lambda i: (0, 0)),
        ],
        out_shape=[
            jax.ShapeDtypeStruct((2 * length, nc), F32),
            jax.ShapeDtypeStruct((SUBLANES, nc), F32),
        ],
        compiler_params=_cparams("arbitrary"),
        name="hyena_filter",
    )(ztab, ztab_t, ztab_t, w1t, b1[:, None], f1[:, None], w2.T, b2[:, None], f2[:, None],
      w3_hi, w3_lo, b3[None], w3_hi, w3_lo, b3[None], deltas)


def _dft_tables(length):
    n_fft = 2 * length
    n1 = n_fft // DFT_INNER
    n1h = n1 // 2
    unit = 2.0 * math.pi / n_fft
    k1 = jnp.arange(n1, dtype=jnp.int32)
    n2 = jnp.arange(DFT_INNER, dtype=jnp.int32)

    def cs(n1_count):
        n = DFT_INNER * jnp.arange(n1_count, dtype=jnp.int32)[None, None, :] + n2[:, None, None]
        th = ((k1[None, :, None] * n) % n_fft).astype(F32) * unit
        return jnp.cos(th), jnp.sin(th)

    c, s = cs(n1h)
    outer = jnp.concatenate([jnp.concatenate([c, s], -1), jnp.concatenate([-s, c], -1)], 1)
    ct, st = jnp.swapaxes(c, 1, 2) / n_fft, jnp.swapaxes(s, 1, 2) / n_fft
    outer_inv = jnp.concatenate([jnp.concatenate([ct, -st], -1), jnp.concatenate([st, ct], -1)], 1)
    c, s = cs(n1)
    outer_real = jnp.concatenate([c, -s], 1)
    m = np.arange(DFT_INNER)
    th = 2.0 * math.pi * np.outer(m, m) / DFT_INNER
    c, s = np.cos(th), np.sin(th)
    as_bf = lambda a: jnp.asarray(a, F32).astype(BF16)

    def per_step(w):
        steps, m, r = DFT_INNER // SUBLANES, w.shape[1], w.shape[2]
        w = w.reshape(steps, SUBLANES, m, r).transpose(0, 2, 3, 1).reshape(steps, m, 1, r * SUBLANES)
        k = lax.broadcasted_iota(jnp.int32, (1, 1, SUBLANES, r * SUBLANES), 2)
        j = lax.broadcasted_iota(jnp.int32, (1, 1, SUBLANES, r * SUBLANES), 3) % SUBLANES
        return jnp.where(j == k, w, 0.0).reshape(steps, m * SUBLANES, r * SUBLANES).astype(BF16)

    return dict(outer=per_step(outer), outer_real=per_step(outer_real), outer_inv=per_step(outer_inv),
                inner=as_bf(np.block([[c, s], [-s, c]])), inner_inv=as_bf(np.block([[c, -s], [s, c]])), n1=n1)


def _dft_outer_kernel(x_ref, w_ref, o_ref):
    x = x_ref[...]
    r = _bdot(w_ref[0], x.reshape(-1, x.shape[-1]).astype(BF16))
    o_ref[...] = r.reshape(o_ref.shape)


def _dft_outer_call(x, lead, w, ct):
    c = x.shape[-1]
    m, r = w.shape[1] // SUBLANES, w.shape[2] // SUBLANES
    if lead is None:
        xspec = pl.BlockSpec((x.shape[0], SUBLANES, ct), lambda cc, i: (0, i, cc))
    else:
        xspec = pl.BlockSpec((None, 2, x.shape[2], SUBLANES, ct), lambda cc, i: (lead, 0, 0, i, cc))
    return pl.pallas_call(
        _dft_outer_kernel,
        grid=(c // ct, DFT_INNER // SUBLANES),
        in_specs=[xspec, pl.BlockSpec((1, m * SUBLANES, r * SUBLANES), lambda cc, i: (i, 0, 0))],
        out_specs=pl.BlockSpec((2, m // 2, SUBLANES, ct), lambda cc, i: (0, 0, i, cc)),
        out_shape=jax.ShapeDtypeStruct((2, m // 2, DFT_INNER, c), F32),
        compiler_params=_cparams("arbitrary", "arbitrary"),
        name="hyena_dft_outer",
    )(x, w)


def _filter_spectrum_kernel(a_ref, w_ref, o_ref):
    for q in range(K1_PER_STEP):
        x = _bdot(w_ref[...], jnp.concatenate([a_ref[0, q], a_ref[1, q]], axis=0).astype(BF16))
        o_ref[0, q] = x[:DFT_INNER].astype(BF16)
        o_ref[1, q] = x[DFT_INNER:].astype(BF16)


def _filter_spectrum_call(a, w_inner):
    _, n1, _, c = a.shape
    blk = pl.BlockSpec((2, K1_PER_STEP, DFT_INNER, c), lambda k: (0, k, 0, 0))
    return pl.pallas_call(
        _filter_spectrum_kernel,
        grid=(n1 // K1_PER_STEP,),
        in_specs=[blk, pl.BlockSpec((2 * DFT_INNER, 2 * DFT_INNER), lambda k: (0, 0))],
        out_specs=blk,
        out_shape=jax.ShapeDtypeStruct(a.shape, BF16),
        compiler_params=_cparams("arbitrary"),
        name="hyena_filter_spectrum",
    )(a, w_inner)


def _spectral_kernel(a_ref, kf_ref, w_ref, wi_ref, o_ref):
    for q in range(K1_PER_STEP):
        x = _bdot(w_ref[...], jnp.concatenate([a_ref[0, q], a_ref[1, q]], axis=0).astype(BF16))
        xr, xi = x[:DFT_INNER], x[DFT_INNER:]
        kr, ki = kf_ref[0, q].astype(F32), kf_ref[1, q].astype(F32)
        yr = xr * kr - xi * ki
        yi = xr * ki + xi * kr
        c = _bdot(wi_ref[...], jnp.concatenate([yr, yi], axis=0).astype(BF16))
        o_ref[0, q] = c[:DFT_INNER]
        o_ref[1, q] = c[DFT_INNER:]


def _spectral_call(a, kf, order, w_inner, w_inner_inv):
    _, n1, _, c = a.shape
    blk = pl.BlockSpec((2, K1_PER_STEP, DFT_INNER, c), lambda k: (0, k, 0, 0))
    sq = pl.BlockSpec((2 * DFT_INNER, 2 * DFT_INNER), lambda k: (0, 0))
    return pl.pallas_call(
        _spectral_kernel,
        grid=(n1 // K1_PER_STEP,),
        in_specs=[blk, pl.BlockSpec((2, K1_PER_STEP, DFT_INNER, c), lambda k: (0, k, 0, order)), sq, sq],
        out_specs=blk,
        out_shape=jax.ShapeDtypeStruct(a.shape, F32),
        compiler_params=_cparams("arbitrary"),
        name="hyena_spectral",
    )(a, kf, w_inner, w_inner_inv)


def _dft_outer_inv_kernel(c_ref, w_ref, z_ref, gate_ref, skip_ref, ss_ref, o_ref):
    scale = lax.rsqrt(ss_ref[...] + EPS)
    cc = c_ref[...]
    y = _bdot(w_ref[0], cc.reshape(-1, cc.shape[-1]).astype(BF16)).reshape(o_ref.shape)
    o_ref[...] = gate_ref[...] * (scale * y + skip_ref[...] * z_ref[...])


def _dft_outer_inv_call(cp, w, zs, z_lead, gates, gate_lead, skip, ss):
    _, _, n1h, _, c = zs.shape
    n1 = cp.shape[1]
    zspec = lambda lead: pl.BlockSpec((None, 2, n1h, SUBLANES, c), lambda i: (lead, 0, 0, i, 0))
    vspec = pl.BlockSpec((1, c), lambda i: (0, 0))
    return pl.pallas_call(
        _dft_outer_inv_kernel,
        grid=(DFT_INNER // SUBLANES,),
        in_specs=[pl.BlockSpec((2, n1, SUBLANES, c), lambda i: (0, 0, i, 0)),
                  pl.BlockSpec((1, 2 * n1h * SUBLANES, 2 * n1 * SUBLANES), lambda i: (i, 0, 0)),
                  zspec(z_lead), zspec(gate_lead), vspec, vspec],
        out_specs=zspec(0),
        out_shape=jax.ShapeDtypeStruct((1, 2, n1h, DFT_INNER, c), F32),
        compiler_params=_cparams("arbitrary"),
        name="hyena_dft_outer_inv",
    )(cp, w, zs, gates, skip, ss)


def _ctx_dft_mats(n_ctx):
    n_fft = 2 * n_ctx
    f = np.arange(n_fft)
    th = 2.0 * math.pi * np.outer(f, np.arange(n_ctx)) / n_fft
    fz = np.concatenate([np.cos(th), -np.sin(th)], axis=0)
    th = 2.0 * math.pi * np.outer(f, f) / n_fft
    fk = np.concatenate([np.cos(th), -np.sin(th)], axis=0)
    th = 2.0 * math.pi * np.outer(np.arange(n_ctx), f) / n_fft
    fi = np.concatenate([np.cos(th), -np.sin(th)], axis=1) / n_fft
    as_bf = lambda a: jnp.asarray(a, F32).astype(BF16)
    return as_bf(fz), as_bf(fk), as_bf(fi)


def _hyctx_kernel(p0_ref, p1_ref, p2_ref, cw_ref, cb_ref, kt_ref, ss_ref, skip_ref,
                  fz_ref, fk_ref, fi_ref, o_ref, xe_s, *, d_hy):
    n_fft = fk_ref.shape[1]
    left = HY_CONV // 2

    def conv(p_ref, c):
        xe_s[0:HALO, :] = jnp.zeros((HALO, d_hy), F32)
        xe_s[HALO:HALO + TM, :] = p_ref[...]
        xe_s[HALO + TM:HALO + TM + HALO, :] = jnp.zeros((HALO, d_hy), F32)
        y = cb_ref[:, c * d_hy:(c + 1) * d_hy]
        for k in range(HY_CONV):
            y = y + cw_ref[k:k + 1, c * d_hy:(c + 1) * d_hy] * xe_s[pl.ds(HALO + k - left, TM), :]
        return y

    z = conv(p0_ref, 0)
    gates = (conv(p1_ref, 1), conv(p2_ref, 2))
    for o in range(HY_ORDER):
        cols = slice(o * d_hy, (o + 1) * d_hy)
        kf = _bdot(fk_ref[...], kt_ref[:, cols].astype(BF16))
        zf = _bdot(fz_ref[...], z.astype(BF16))
        zr, zi = zf[:n_fft], zf[n_fft:]
        kr, ki = kf[:n_fft], kf[n_fft:]
        yf = jnp.concatenate([zr * kr - zi * ki, zr * ki + zi * kr], axis=0)
        y = _bdot(fi_ref[...], yf.astype(BF16))
        scale = lax.rsqrt(ss_ref[0:1, cols] + EPS)
        z = gates[o] * (scale * y + skip_ref[o:o + 1, :] * z)
    o_ref[...] = z


def _hyctx_call(p, off_hy, cw, cb, kt, ss, skip, mats, *, batch, n_lat, n_ctx):
    d_hy = skip.shape[1]
    assert n_ctx == TM
    c0 = off_hy // d_hy
    r0 = batch * n_lat // TM
    fz, fk, fi = mats
    full = lambda a: pl.BlockSpec(a.shape, lambda b: (0,) * a.ndim)
    return pl.pallas_call(
        functools.partial(_hyctx_kernel, d_hy=d_hy),
        grid=(batch,),
        in_specs=[pl.BlockSpec((TM, d_hy), (lambda b, c=c0 + k: (r0 + b, c))) for k in range(3)]
        + [full(cw), full(cb), full(kt), full(ss), full(skip), full(fz), full(fk), full(fi)],
        out_specs=pl.BlockSpec((TM, d_hy), lambda b: (b, 0)),
        out_shape=jax.ShapeDtypeStruct((batch * n_ctx, d_hy), F32),
        scratch_shapes=[pltpu.VMEM((TM + 2 * HALO, d_hy), F32)],
        compiler_params=_cparams("arbitrary"),
        name="hyena_ctx",
    )(p, p, p, cw, cb, kt, ss, skip, fz, fk, fi)


def _merge_kernel(x_ref, hf_ref, hb_ref, pg_ref, hy_ref, sg_ref, gg_ref, w_ref, gate_ref, o_ref):
    a = (hf_ref[...] + hb_ref[...]) * _gelu(pg_ref[...])
    y = jnp.concatenate([_rms(a), _rms(hy_ref[...]), _rms(sg_ref[...])], axis=-1) * gg_ref[...]
    o_ref[...] = x_ref[...] + gate_ref[0] * _bdot(y.astype(BF16), w_ref[...])


def _merge_call(x, hf, hb, p, hy, sg, gg, w, gate, seg_of_tile, n_rows):
    d = x.shape[1]
    d_lru, d_hy, d_sg = hf.shape[1], hy.shape[1], sg.shape[1]
    row = lambda width, col=0: pl.BlockSpec((TM, width), lambda i: (i, col))
    return pl.pallas_call(
        _merge_kernel,
        grid=(n_rows // TM,),
        in_specs=[row(d), row(d_lru), row(d_lru), row(d_lru, 1), row(d_hy), row(d_sg),
                  pl.BlockSpec((1, d), lambda i: (0, 0)),
                  pl.BlockSpec((d, d), lambda i: (0, 0)),
                  pl.BlockSpec((1, 1, d), lambda i: (seg_of_tile(i), 0, 0))],
        out_specs=row(d),
        out_shape=jax.ShapeDtypeStruct((n_rows, d), F32),
        compiler_params=_cparams("arbitrary"),
        name="merge_out_proj",
    )(x, hf, hb, p, hy, sg, gg, w, gate)


def _pack_bf16_pairs(x, is_bf16_valued=False):
    half = x.shape[1] // 2
    lo, hi = x[:, :half], x[:, half:]
    if not is_bf16_valued:
        lo, hi = lo.astype(BF16).astype(F32), hi.astype(BF16).astype(F32)
    return (pltpu.bitcast(lo, jnp.uint32) >> 16) | (pltpu.bitcast(hi, jnp.uint32) & jnp.uint32(0xFFFF0000))


def _unpack_bf16_pairs(u):
    lo = pltpu.bitcast(u << 16, F32).astype(BF16)
    hi = pltpu.bitcast(u & jnp.uint32(0xFFFF0000), F32).astype(BF16)
    return lo, hi


def _router_kernel(x_ref, g_ref, sc_ref, sh_ref, rw_ref, rb_ref, f_ref, e_ref, gt_ref, m_ref):
    f = _rms(x_ref[...]) * g_ref[...] * (1.0 + sc_ref[0]) + sh_ref[0]
    f_hi = f.astype(BF16)
    f_ref[...] = f_hi
    f_lo = (f - f_hi.astype(F32)).astype(BF16)
    part = _bdot(f_hi, rw_ref[...])
    logits = part[:, :LANES] + part[:, LANES:] + _bdot(f_lo, rw_ref[:, :LANES]) + rb_ref[...]
    lane = lax.broadcasted_iota(jnp.int32, logits.shape, 1).astype(F32)
    work = logits
    e_out = jnp.zeros_like(logits)
    mask = jnp.zeros_like(logits)
    vals = []
    for k in range(TOP_K):
        mx = jnp.max(work, axis=-1, keepdims=True)
        idx = jnp.min(jnp.where(work == mx, lane, float(LANES)), axis=-1, keepdims=True)
        sel = lane == idx
        e_out = jnp.where(lane == float(k), idx, e_out)
        mask = jnp.where(sel, 1.0, mask)
        work = jnp.where(sel, -jnp.inf, work)
        vals.append(mx)
    ex = [jnp.exp(v - vals[0]) for v in vals]
    den = ex[0]
    for v in ex[1:]:
        den = den + v
    g_out = jnp.zeros_like(logits)
    for k in range(TOP_K):
        g_out = jnp.where(lane == float(k), ex[k] / den, g_out)
    e_ref[...] = e_out.astype(jnp.int32)
    gt_ref[...] = g_out
    m_ref[...] = mask


def _router_call(x, g, sc, sh, rw, rb, seg_of_tile, n_rows):
    d = x.shape[1]
    row = lambda width: pl.BlockSpec((TM, width), lambda i: (i, 0))
    mod = pl.BlockSpec((1, 1, d), lambda i: (seg_of_tile(i), 0, 0))
    return pl.pallas_call(
        _router_kernel,
        grid=(n_rows // TM,),
        in_specs=[row(d), pl.BlockSpec((1, d), lambda i: (0, 0)), mod, mod,
                  pl.BlockSpec((d, 2 * LANES), lambda i: (0, 0)), pl.BlockSpec((1, LANES), lambda i: (0, 0))],
        out_specs=[row(d), row(LANES), row(LANES), row(LANES)],
        out_shape=[jax.ShapeDtypeStruct((n_rows, d), BF16), jax.ShapeDtypeStruct((n_rows, LANES), jnp.int32),
                   jax.ShapeDtypeStruct((n_rows, LANES), F32), jax.ShapeDtypeStruct((n_rows, LANES), F32)],
        compiler_params=_cparams("arbitrary"),
        name="moe_router",
    )(x, g, sc, sh, rw, rb)


def _rank_kernel(m_ref, e_ref, tri_ref, utri_ref, ls_ref, meta_ref, tot_ref, carry_s):
    i = pl.program_id(0)

    @pl.when(i == 0)
    def _():
        carry_s[...] = jnp.zeros_like(carry_s)

    m = m_ref[...]
    rank = _bdot(tri_ref[...], m.astype(BF16))
    cnt = jnp.sum(m, axis=0, keepdims=True)
    seg = jnp.floor((cnt + (SEG_ROWS - 1)) * (1.0 / SEG_ROWS)) * SEG_ROWS
    seg8 = jnp.broadcast_to(seg, (SUBLANES, LANES))
    loc = _bdot(seg8.astype(BF16), utri_ref[...])[0:1, :]
    sd = rank + loc
    e = e_ref[...]
    lane = lax.broadcasted_iota(jnp.int32, sd.shape, 1)
    out = jnp.zeros_like(sd)
    for k in range(TOP_K):
        sk = jnp.sum(jnp.where(lane == e[:, k:k + 1], sd, 0.0), axis=-1, keepdims=True)
        out = jnp.where(lane == k, sk, out)
    ls_ref[...] = out.astype(jnp.int32)
    row = lax.broadcasted_iota(jnp.int32, (SUBLANES, LANES), 0)
    meta_ref[0] = jnp.where(row == 0, carry_s[...], jnp.where(row == 1, seg8, 0.0))
    carry_s[...] = carry_s[...] + seg8
    tot_ref[...] = carry_s[...]


def _rank_call(mask, e4):
    n = mask.shape[0]
    tri = jnp.asarray(np.tril(np.ones((TM, TM)), -1), BF16)
    utri = jnp.asarray(np.triu(np.ones((LANES, LANES)), 1), BF16)
    row = pl.BlockSpec((TM, LANES), lambda i: (i, 0))
    return pl.pallas_call(
        _rank_kernel,
        grid=(n // TM,),
        in_specs=[row, row, pl.BlockSpec((TM, TM), lambda i: (0, 0)), pl.BlockSpec((LANES, LANES), lambda i: (0, 0))],
        out_specs=[row, pl.BlockSpec((1, SUBLANES, LANES), lambda i: (i, 0, 0)),
                   pl.BlockSpec((SUBLANES, LANES), lambda i: (0, 0))],
        out_shape=[jax.ShapeDtypeStruct((n, LANES), jnp.int32), jax.ShapeDtypeStruct((n // TM, SUBLANES, LANES), F32),
                   jax.ShapeDtypeStruct((SUBLANES, LANES), F32)],
        scratch_shapes=[pltpu.VMEM((SUBLANES, LANES), F32)],
        compiler_params=_cparams("arbitrary"),
        name="moe_rank",
    )(mask, e4, tri, utri)


def _seg_copy(src, dst, src_row, dst_row, sem):
    return pltpu.make_async_copy(src.at[pl.ds(src_row, SEG_ROWS)], dst.at[pl.ds(dst_row, SEG_ROWS)], sem)


def _drain(copy, count):
    def body(r, c):
        copy.wait()
        return c

    lax.fori_loop(0, count, body, 0)


def _tile_segments(seg_ref, nch_ref, tile, move):
    off = jnp.int32(0)
    for e in range(N_EXPERTS):
        nc = nch_ref[tile * N_EXPERTS + e]
        g0 = seg_ref[tile * N_EXPERTS + e]

        def body(c, carry, off=off, g0=g0):
            move(pl.multiple_of(off + c * SEG_ROWS, SEG_ROWS), pl.multiple_of(g0 + c * SEG_ROWS, SEG_ROWS))
            return carry

        lax.fori_loop(0, nc, body, 0)
        off = off + nc * SEG_ROWS
    return off // SEG_ROWS


def _tile_chunks(nch_ref, tile):
    total = jnp.int32(0)
    for e in range(N_EXPERTS):
        total = total + nch_ref[tile * N_EXPERTS + e]
    return total


def _dispatch_kernel(seg_ref, nch_ref, pad_ref, f_ref, ls_ref, xs_hbm, xl_s, zero_s, sem, zsem):
    i = pl.program_id(0)
    n = pl.num_programs(0)
    rows = xl_s.shape[1]
    cur = i % 2

    def sent(tile, b):
        _drain(_seg_copy(xl_s.at[b], xs_hbm, 0, 0, sem.at[b]), _tile_chunks(nch_ref, tile))

    @pl.when(i >= 2)
    def _():
        sent(i - 2, cur)

    ls_t = ls_ref[...].astype(F32).T
    r_iota = lax.broadcasted_iota(jnp.int32, (rows, TM), 0).astype(F32)
    perm = jnp.zeros((rows, TM), F32)
    for k in range(TOP_K):
        perm = jnp.where(r_iota == ls_t[k:k + 1, :], 1.0, perm)
    xl_s[cur] = _pack_bf16_pairs(_bdot(perm.astype(BF16), f_ref[...]), is_bf16_valued=True)

    _tile_segments(seg_ref, nch_ref, i, lambda lr, gr: _seg_copy(xl_s.at[cur], xs_hbm, lr, gr, sem.at[cur]).start())

    @pl.when(i == 0)
    def _():
        zero_s[...] = jnp.zeros_like(zero_s)
        total = jnp.int32(0)
        for e in range(N_EXPERTS):
            nc = pad_ref[N_EXPERTS + e]
            g0 = pad_ref[e]

            def body(c, carry, g0=g0):
                _seg_copy(zero_s, xs_hbm, 0, pl.multiple_of(g0 + c * SEG_ROWS, SEG_ROWS), zsem).start()
                return carry

            lax.fori_loop(0, nc, body, 0)
            total = total + nc
        _drain(_seg_copy(zero_s, xs_hbm, 0, 0, zsem), total)

        def block_copy(b):
            return pltpu.make_async_copy(zero_s, xs_hbm.at[pl.ds(pl.multiple_of(b * MOE_BLOCK, MOE_BLOCK), MOE_BLOCK)], zsem)

        n_used = pad_ref[2 * N_EXPERTS]
        n_blocks = xs_hbm.shape[0] // MOE_BLOCK

        def start_block(b, carry):
            block_copy(b).start()
            return carry

        def wait_block(b, carry):
            block_copy(b).wait()
            return carry

        lax.fori_loop(n_used, n_blocks, start_block, 0)
        lax.fori_loop(n_used, n_blocks, wait_block, 0)

    @pl.when(i == n - 1)
    def _():
        @pl.when(i >= 1)
        def _():
            sent(i - 1, 1 - cur)

        sent(i, cur)


def _dispatch_call(seg, nch, pad, f, ls, n_rows):
    n, d = f.shape
    return pl.pallas_call(
        _dispatch_kernel,
        grid_spec=pltpu.PrefetchScalarGridSpec(
            num_scalar_prefetch=3,
            grid=(n // TM,),
            in_specs=[pl.BlockSpec((TM, d), lambda i, *_: (i, 0)), pl.BlockSpec((TM, LANES), lambda i, *_: (i, 0))],
            out_specs=pl.BlockSpec(memory_space=pl.ANY),
            scratch_shapes=[pltpu.VMEM((2, LOCAL_ROWS, d // 2), jnp.uint32), pltpu.VMEM((MOE_BLOCK, d // 2), jnp.uint32),
                            pltpu.SemaphoreType.DMA((2,)), pltpu.SemaphoreType.DMA(())],
        ),
        out_shape=jax.ShapeDtypeStruct((n_rows, d // 2), jnp.uint32),
        compiler_params=_cparams("arbitrary", row_dma=True),
        name="moe_dispatch",
    )(seg, nch, pad, f, ls)


def _expert_kernel(be_ref, nu_ref, x_ref, w1_ref, b1_ref, w2_ref, b2_ref, o_ref, w1_s, w2_s):
    i = pl.program_id(0)
    d_e = w2_ref.shape[2]
    half = x_ref.shape[1]
    live = i < nu_ref[0]
    new_expert = jnp.logical_or(i == 0, be_ref[i] != be_ref[jnp.maximum(i - 1, 0)])

    @pl.when(jnp.logical_and(live, new_expert))
    def _():
        w1_s[...] = w1_ref[0, 0].astype(BF16)
        w2_s[...] = w2_ref[0, 0].astype(BF16)

    @pl.when(live)
    def _():
        lo, hi = _unpack_bf16_pairs(x_ref[...])
        gu = _bdot(lo, w1_s[:half, :]) + _bdot(hi, w1_s[half:, :]) + b1_ref[0, 0]
        glu = jnp.minimum(gu[:, :d_e], SWIGLU_LIMIT)
        lin = jnp.clip(gu[:, d_e:], -SWIGLU_LIMIT, SWIGLU_LIMIT)
        act = glu * _sigmoid(SWIGLU_ALPHA * glu) * (lin + 1.0)
        o_ref[...] = _pack_bf16_pairs(_bdot(act.astype(BF16), w2_s[...]) + b2_ref[0, 0])

    @pl.when(i >= nu_ref[0])
    def _():
        o_ref[...] = jnp.zeros_like(o_ref)


def _expert_call(block_e, n_used, xs, w1, b1, w2, b2, layer):
    n_rows, half = xs.shape
    depth, n_e, d, d_gu = w1.shape
    d_e = w2.shape[2]
    used = lambda i, nu: jnp.minimum(i, nu[0] - 1)
    return pl.pallas_call(
        _expert_kernel,
        grid_spec=pltpu.PrefetchScalarGridSpec(
            num_scalar_prefetch=2,
            grid=(n_rows // MOE_BLOCK,),
            in_specs=[
                pl.BlockSpec((MOE_BLOCK, half), lambda i, be, nu: (used(i, nu), 0)),
                pl.BlockSpec((1, 1, d, d_gu), lambda i, be, nu: (layer, be[i], 0, 0)),
                pl.BlockSpec((1, 1, 1, d_gu), lambda i, be, nu: (layer, be[i], 0, 0)),
                pl.BlockSpec((1, 1, d_e, d), lambda i, be, nu: (layer, be[i], 0, 0)),
                pl.BlockSpec((1, 1, 1, d), lambda i, be, nu: (layer, be[i], 0, 0)),
            ],
            out_specs=pl.BlockSpec((MOE_BLOCK, half), lambda i, be, nu: (i, 0)),
            scratch_shapes=[pltpu.VMEM((d, d_gu), BF16), pltpu.VMEM((d_e, d), BF16)],
        ),
        out_shape=jax.ShapeDtypeStruct((n_rows, half), jnp.uint32),
        compiler_params=_cparams("arbitrary"),
        name="moe_experts",
    )(block_e, n_used, xs, w1, b1.reshape(depth, n_e, 1, d_gu), w2, b2.reshape(depth, n_e, 1, d))


def _combine_kernel(seg_ref, nch_ref, x_ref, ls_ref, g_ref, gate_ref, ys_hbm, o_ref, yl_s, sem):
    i = pl.program_id(0)
    n = pl.num_programs(0)
    rows = yl_s.shape[1]

    def fetch(tile, b):
        return _tile_segments(seg_ref, nch_ref, tile,
                              lambda lr, gr: _seg_copy(ys_hbm, yl_s.at[b], gr, lr, sem.at[b]).start())

    @pl.when(i == 0)
    def _():
        yl_s[...] = jnp.zeros_like(yl_s)
        fetch(0, 0)

    @pl.when(i + 1 < n)
    def _():
        fetch(i + 1, (i + 1) % 2)

    cur = i % 2
    total = jnp.int32(0)
    for e in range(N_EXPERTS):
        total = total + nch_ref[i * N_EXPERTS + e]
    _drain(_seg_copy(ys_hbm, yl_s.at[cur], 0, 0, sem.at[cur]), total)

    ls = ls_ref[...]
    g = g_ref[...]
    r_iota = lax.broadcasted_iota(jnp.int32, (TM, rows), 1)
    wsel = jnp.zeros((TM, rows), F32)
    for k in range(TOP_K):
        wsel = jnp.where(r_iota == ls[:, k:k + 1], g[:, k:k + 1], wsel)
    wsel = wsel.astype(BF16)
    lo, hi = _unpack_bf16_pairs(yl_s[cur])
    acc = jnp.concatenate([_bdot(wsel, lo), _bdot(wsel, hi)], axis=-1)
    o_ref[...] = x_ref[...] + gate_ref[0] * acc


def _combine_call(seg, nch, x, ls, g4, gate, ys, seg_of_tile, n_tok):
    d = x.shape[1]
    row = lambda width: pl.BlockSpec((TM, width), lambda i, *_: (i, 0))
    return pl.pallas_call(
        _combine_kernel,
        grid_spec=pltpu.PrefetchScalarGridSpec(
            num_scalar_prefetch=2,
            grid=(n_tok // TM,),
            in_specs=[row(d), row(LANES), row(LANES),
                      pl.BlockSpec((1, 1, d), lambda i, *_: (seg_of_tile(i), 0, 0)),
                      pl.BlockSpec(memory_space=pl.ANY)],
            out_specs=row(d),
            scratch_shapes=[pltpu.VMEM((2, LOCAL_ROWS, d // 2), jnp.uint32), pltpu.SemaphoreType.DMA((2,))],
        ),
        out_shape=jax.ShapeDtypeStruct((n_tok, d), F32),
        compiler_params=_cparams("arbitrary", row_dma=True),
        name="moe_combine",
    )(seg, nch, x, ls, g4, gate, ys)


def _moe(x, norm_g, sc, sh, gate, rw, rb, w1, b1, w2, b2, layer, seg_of_row_tile, n_tok):
    n_tiles = n_tok // TM
    f, e4, g4, mask = _router_call(x, norm_g, sc, sh, rw, rb, seg_of_row_tile(TM), n_tok)
    ls, meta, tot = _rank_call(mask, e4)
    rows_e = tot[0, :N_EXPERTS].astype(jnp.int32)
    region = (rows_e + MOE_BLOCK - 1) // MOE_BLOCK * MOE_BLOCK
    pend = jnp.cumsum(region)
    pstart = pend - region
    max_rows = n_tok * TOP_K + n_tiles * N_EXPERTS * (SEG_ROWS - 1) + N_EXPERTS * (MOE_BLOCK - 1)
    n_blocks = -(-max_rows // MOE_BLOCK)
    first_row = jnp.arange(n_blocks, dtype=jnp.int32) * MOE_BLOCK
    block_e = jnp.minimum(jnp.sum((pend[None, :] <= first_row[:, None]).astype(jnp.int32), axis=1), N_EXPERTS - 1)
    n_used = (pend[-1:] // MOE_BLOCK).astype(jnp.int32)
    seg = (pstart[None, :] + meta[:, 0, :N_EXPERTS].astype(jnp.int32)).reshape(-1)
    nch = (meta[:, 1, :N_EXPERTS].astype(jnp.int32) // SEG_ROWS).reshape(-1)
    pad = jnp.concatenate([pstart + rows_e, (region - rows_e) // SEG_ROWS, n_used])
    xs = _dispatch_call(seg, nch, pad, f, ls, n_blocks * MOE_BLOCK)
    ys = _expert_call(block_e, n_used, xs, w1, b1, w2, b2, layer)
    return _combine_call(seg, nch, x, ls, g4, gate, ys, seg_of_row_tile(TM), n_tok)


def _final_kernel(x_ref, g_ref, o_ref):
    o_ref[...] = _rms(x_ref[...]) * g_ref[...]


def _final_call(x, g, n_rows):
    d = x.shape[1]
    return pl.pallas_call(
        _final_kernel,
        grid=(n_rows // TM,),
        in_specs=[pl.BlockSpec((TM, d), lambda i: (i, 0)), pl.BlockSpec((1, d), lambda i: (0, 0))],
        out_specs=pl.BlockSpec((TM, d), lambda i: (i, 0)),
        out_shape=jax.ShapeDtypeStruct((n_rows, d), F32),
        compiler_params=_cparams("arbitrary"),
        name="final_norm",
    )(x, g)


def _blockdiag(w):
    h, hd, _ = w.shape
    return jnp.einsum('hij,hg->higj', w, jnp.eye(h, dtype=w.dtype)).reshape(h * hd, h * hd)


def kernel(x, c, ctx, c_ctx, ada_w, ada_b, norm_mix_g, w_in, lru_conv_w, lru_conv_b, lru_wa, lru_ba, lru_wx, lru_bx, lru_lam, hy_conv_w, hy_conv_b, hf_w1, hf_b1, hf_f1, hf_w2, hf_b2, hf_f2, hf_w3, hf_b3, hy_skip, sg_ln_g, sg_ln_b, sg_w, sg_b, grp_norm_g, w_out, norm_ffn_g, router_w, router_b, moe_w1, moe_b1, moe_w2, moe_b2, final_norm_g):
    batch, n_lat, d = x.shape
    n_ctx = ctx.shape[1]
    depth = ada_w.shape[0]
    d_lru = lru_conv_w.shape[2]
    d_hy = hy_skip.shape[2]
    d_sg = sg_ln_g.shape[1]
    off_hy = 2 * d_lru
    off_sg = off_hy + 3 * d_hy
    assert batch == 2 and n_ctx == TM and n_lat % (TM * 2) == 0 and batch + 1 <= SUBLANES
    n_lat_total = batch * n_lat
    n_all = n_lat_total + batch * n_ctx

    def seg_of_row_tile(rows):
        per_seq = n_lat // rows
        return lambda i: jnp.minimum(i // per_seq, batch)

    cond = jnp.zeros((SUBLANES, d), F32).at[:batch].set(c).at[batch].set(c_ctx)
    mods = _ada_call(cond, ada_w, ada_b)

    mats = _dft_tables(n_lat)
    ctx_mats = _ctx_dft_mats(n_ctx)
    n1 = mats["n1"]
    sg_bias_shape = (SG_CHUNK, d_sg)

    xt = jnp.concatenate([x.reshape(n_lat_total, d), ctx.reshape(batch * n_ctx, d)], axis=0)
    for l in range(depth):
        ctx_out = l < depth - 1
        m = mods[l, :batch + 1].reshape(batch + 1, N_MOD, d)
        mod = [m[:, j][:, None, :] for j in range(N_MOD)]

        p = _inproj_call(xt, norm_mix_g[l][None], mod[1], mod[0], w_in[l].astype(BF16), seg_of_row_tile(TM))

        hs = []
        for dr, rev in enumerate((False, True)):
            wg = jnp.concatenate([_blockdiag(lru_wa[l, dr]), _blockdiag(lru_wx[l, dr])], axis=1).astype(BF16)
            bg = jnp.concatenate([lru_ba[l, dr], lru_bx[l, dr]])[None]
            hs.append(_lru_call(p, lru_conv_w[l], lru_conv_b[l][None], wg, bg, lru_lam[l, dr][None],
                                reverse=rev, batch=batch, n_lat=n_lat, n_ctx=n_ctx))

        sg_bias = jnp.broadcast_to(sg_b[l].T[:, :, None], (SG_CHUNK, SG_HEADS, d_sg // SG_HEADS)).reshape(sg_bias_shape)
        sg = _sg_call(p, off_sg, sg_ln_g[l][None], sg_ln_b[l][None], sg_w[l].astype(BF16), sg_bias)

        filt = (hf_w1[l], hf_b1[l], hf_f1[l], hf_w2[l], hf_b2[l], hf_f2[l], hf_w3[l], hf_b3[l])
        kt, ss = _filter_call(n_lat, *filt, d_hy)
        ka = _dft_outer_call(kt.reshape(n1, DFT_INNER, HY_ORDER * d_hy), None, mats["outer_real"], d_hy)
        kf = _filter_spectrum_call(ka, mats["inner"])
        vxx = _hyconv_call(p, off_hy, hy_conv_w[l], hy_conv_b[l][None], batch=batch, n_lat=n_lat)
        vxx = vxx.reshape(3, batch, n1 // 2, DFT_INNER, d_hy)
        zs, z_lead = vxx, 0
        for o in range(HY_ORDER):
            a = _dft_outer_call(zs, z_lead, mats["outer"], d_hy)
            cp = _spectral_call(a, kf, o, mats["inner"], mats["inner_inv"])
            zs = _dft_outer_inv_call(cp, mats["outer_inv"], zs, z_lead, vxx, 1 + o, hy_skip[l, o][None],
                                     ss[0:1, o * d_hy:(o + 1) * d_hy])
            z_lead = 0
        hy = zs.reshape(n_lat_total, d_hy)
        if ctx_out:
            ktc, ssc = _filter_call(n_ctx, *filt, d_hy)
            hyc = _hyctx_call(p, off_hy, hy_conv_w[l], hy_conv_b[l][None], ktc, ssc, hy_skip[l], ctx_mats,
                              batch=batch, n_lat=n_lat, n_ctx=n_ctx)
            hy = jnp.concatenate([hy, hyc], axis=0)

        n_tok = n_all if ctx_out else n_lat_total
        xt = _merge_call(xt, hs[0], hs[1], p, hy, sg, grp_norm_g[l][None], w_out[l].astype(BF16), mod[2],
                         seg_of_row_tile(TM), n_tok)

        rw = jnp.zeros((d, LANES), F32).at[:, :N_EXPERTS].set(router_w[l])
        rw_hi = rw.astype(BF16)
        rw = jnp.concatenate([rw_hi, (rw - rw_hi.astype(F32)).astype(BF16)], axis=1)
        rb = jnp.full((1, LANES), -1e30, F32).at[0, :N_EXPERTS].set(router_b[l])
        xt = _moe(xt, norm_ffn_g[l][None], mod[4], mod[3], mod[5], rw, rb, moe_w1, moe_b1, moe_w2, moe_b2,
                  l, seg_of_row_tile, n_tok)

    return _final_call(xt, final_norm_g[None], n_lat_total).reshape(batch, n_lat, d)
```

```python
import functools
import math

import numpy as np
import jax
import jax.numpy as jnp
from jax import lax
from jax.experimental import pallas as pl
from jax.experimental.pallas import tpu as pltpu

F32 = jnp.float32
BF16 = jnp.bfloat16

EPS = 1e-6
N_MOD = 6
LRU_HEADS = 8
LRU_CONV = 4
LRU_C = 8.0
HY_ORDER = 2
HY_CONV = 3
HY_BANDS = 8
HY_EMB = 1 + 2 * HY_BANDS
HY_EMB_PAD = 32
HY_MIN_DECAY = math.log(1e-2) / 1.5
HY_MAX_DECAY = math.log(1e-2) / 0.3
SG_CHUNK = 128
SG_HEADS = 4
N_EXPERTS = 32
TOP_K = 4
MOE_BLOCK = 256
SWIGLU_LIMIT = 7.0
SWIGLU_ALPHA = 1.702

LANES = 128
SUBLANES = 8
TM = 256
HALO = SUBLANES
DFT_INNER = 128
K1_PER_STEP = 4
HYCONV_ROWS = 1024
SEG_ROWS = SUBLANES
LOCAL_ROWS = TM * TOP_K + N_EXPERTS * SEG_ROWS
VMEM_LIMIT = 48 * 1024 * 1024


def _cparams(*sem, vmem=VMEM_LIMIT, row_dma=False):
    return pltpu.CompilerParams(dimension_semantics=sem, vmem_limit_bytes=vmem, disable_bounds_checks=row_dma)


def _bdot(a, b):
    return jnp.dot(a, b, preferred_element_type=F32)


def _hdot(a, b):
    return jnp.dot(a, b, preferred_element_type=F32, precision=lax.Precision.HIGHEST)


def _gelu(x):
    return 0.5 * x * (1.0 + jnp.tanh(math.sqrt(2.0 / math.pi) * (x + 0.044715 * (x * x * x))))


def _sigmoid(x):
    return 0.5 * jnp.tanh(0.5 * x) + 0.5


def _rms(x):
    return x * lax.rsqrt(jnp.mean(x * x, axis=-1, keepdims=True) + EPS)


def _ada_kernel(s_ref, w_ref, b_ref, o_ref):
    s = s_ref[...]
    s = s * _sigmoid(s)
    o_ref[0] = _bdot(s.astype(BF16), w_ref[0].astype(BF16)) + b_ref[0]


def _ada_call(cond, ada_w, ada_b):
    depth, d, n = ada_w.shape
    tn = 1536
    return pl.pallas_call(
        _ada_kernel,
        grid=(depth, n // tn),
        in_specs=[
            pl.BlockSpec((SUBLANES, d), lambda l, j: (0, 0)),
            pl.BlockSpec((1, d, tn), lambda l, j: (l, 0, j)),
            pl.BlockSpec((1, 1, tn), lambda l, j: (l, 0, j)),
        ],
        out_specs=pl.BlockSpec((1, SUBLANES, tn), lambda l, j: (l, 0, j)),
        out_shape=jax.ShapeDtypeStruct((depth, SUBLANES, n), F32),
        compiler_params=_cparams("arbitrary", "arbitrary"),
        name="ada_mod",
    )(cond, ada_w, ada_b.reshape(depth, 1, n))


def _inproj_kernel(x_ref, g_ref, sc_ref, sh_ref, w_ref, o_ref):
    h = _rms(x_ref[...]) * g_ref[...] * (1.0 + sc_ref[0]) + sh_ref[0]
    o_ref[...] = _bdot(h.astype(BF16), w_ref[...])


def _inproj_call(x, g, sc, sh, w, seg_of_tile):
    t, d = x.shape
    n = w.shape[1]
    tn = n // 2
    return pl.pallas_call(
        _inproj_kernel,
        grid=(2, t // TM),
        in_specs=[
            pl.BlockSpec((TM, d), lambda j, i: (i, 0)),
            pl.BlockSpec((1, d), lambda j, i: (0, 0)),
            pl.BlockSpec((1, 1, d), lambda j, i: (seg_of_tile(i), 0, 0)),
            pl.BlockSpec((1, 1, d), lambda j, i: (seg_of_tile(i), 0, 0)),
            pl.BlockSpec((d, tn), lambda j, i: (0, j)),
        ],
        out_specs=pl.BlockSpec((TM, tn), lambda j, i: (i, j)),
        out_shape=jax.ShapeDtypeStruct((t, n), F32),
        compiler_params=_cparams("arbitrary", "arbitrary"),
        name="in_proj",
    )(x, g, sc, sh, w)


def _lru_kernel(x_ref, xp_ref, xn_ref, cw_ref, cb_ref, wg_ref, bg_ref, lam_ref, o_ref,
                xe_s, a_s, b_s, h_s, *, reverse, n_chunks, d_lru):
    s = pl.program_id(1)
    j = (n_chunks - s) if reverse else (s - 1)
    has_prev = jnp.logical_and(s > 0, j > 0)
    has_next = jnp.logical_and(s > 0, j < n_chunks - 1)

    @pl.when(s == 0)
    def _():
        h_s[...] = jnp.zeros_like(h_s)

    xe_s[0:HALO, :] = jnp.where(has_prev, xp_ref[...], 0.0)
    xe_s[HALO:HALO + TM, :] = x_ref[...]
    xe_s[HALO + TM:HALO + TM + HALO, :] = jnp.where(has_next, xn_ref[...], 0.0)
    left = LRU_CONV // 2
    xc = cb_ref[...]
    for k in range(LRU_CONV):
        xc = xc + cw_ref[k:k + 1, :] * xe_s[pl.ds(HALO + k - left, TM), :]

    g = _bdot(xc.astype(BF16), wg_ref[...]) + bg_ref[...]
    r = _sigmoid(g[:, :d_lru])
    ig = _sigmoid(g[:, d_lru:])
    lam = lam_ref[...]
    sp = jnp.maximum(-lam, 0.0) + jnp.log1p(jnp.exp(-jnp.abs(lam)))
    log_a = (-LRU_C * r) * sp
    a = jnp.exp(log_a)
    a_s[...] = a
    b_s[...] = jnp.sqrt(-jnp.tanh(log_a) * (a * a + 1.0)) * (ig * xc)

    def step(t, h):
        tt = (TM - 1 - t) if reverse else t
        h = a_s[pl.ds(tt, 1), :] * h + b_s[pl.ds(tt, 1), :]
        o_ref[pl.ds(tt, 1), :] = h
        return h

    h_s[0:1, :] = lax.fori_loop(0, TM, step, h_s[0:1, :], unroll=8)


def _lru_call(p, cw, cb, wg, bg, lam, *, reverse, batch, n_lat, n_ctx):
    t = p.shape[0]
    d_lru = cw.shape[1]
    n_chunks = n_lat // TM
    assert n_ctx == TM
    ctx0 = batch * n_chunks
    per = TM // HALO
    last = t // HALO - 1

    def rb(b, s):
        lat = b * n_chunks + ((n_chunks - s) if reverse else (s - 1))
        return jnp.where(s == 0, ctx0 + b, lat)

    kern = functools.partial(_lru_kernel, reverse=reverse, n_chunks=n_chunks, d_lru=d_lru)
    return pl.pallas_call(
        kern,
        grid=(batch, n_chunks + 1),
        in_specs=[
            pl.BlockSpec((TM, d_lru), lambda b, s: (rb(b, s), 0)),
            pl.BlockSpec((HALO, d_lru), lambda b, s: (jnp.maximum(rb(b, s) * per - 1, 0), 0)),
            pl.BlockSpec((HALO, d_lru), lambda b, s: (jnp.minimum((rb(b, s) + 1) * per, last), 0)),
            pl.BlockSpec((LRU_CONV, d_lru), lambda b, s: (0, 0)),
            pl.BlockSpec((1, d_lru), lambda b, s: (0, 0)),
            pl.BlockSpec((d_lru, 2 * d_lru), lambda b, s: (0, 0)),
            pl.BlockSpec((1, 2 * d_lru), lambda b, s: (0, 0)),
            pl.BlockSpec((1, d_lru), lambda b, s: (0, 0)),
        ],
        out_specs=pl.BlockSpec((TM, d_lru), lambda b, s: (rb(b, s), 0)),
        out_shape=jax.ShapeDtypeStruct((t, d_lru), F32),
        scratch_shapes=[
            pltpu.VMEM((TM + 2 * HALO, d_lru), F32),
            pltpu.VMEM((TM, d_lru), F32),
            pltpu.VMEM((TM, d_lru), F32),
            pltpu.VMEM((SUBLANES, d_lru), F32),
        ],
        compiler_params=_cparams("arbitrary", "arbitrary"),
        name="rglru_rev" if reverse else "rglru_fwd",
    )(p, p, p, cw, cb, wg, bg, lam)


def _sg_kernel(u0_ref, u1_ref, v0_ref, v1_ref, lg_ref, lb_ref, ws_ref, bias_ref, o_ref):
    u = _gelu(jnp.concatenate([u0_ref[...], u1_ref[...]], axis=-1))
    v = _gelu(jnp.concatenate([v0_ref[...], v1_ref[...]], axis=-1))
    mu = jnp.mean(v, axis=-1, keepdims=True)
    dv = v - mu
    var = jnp.mean(dv * dv, axis=-1, keepdims=True)
    vb = ((dv * lax.rsqrt(var + EPS)) * lg_ref[...] + lb_ref[...]).astype(BF16)
    hd = vb.shape[1] // SG_HEADS
    for ch in range(TM // SG_CHUNK):
        r0 = ch * SG_CHUNK
        ys = [_bdot(ws_ref[g], vb[r0:r0 + SG_CHUNK, g * hd:(g + 1) * hd]) for g in range(SG_HEADS)]
        y = jnp.concatenate(ys, axis=-1) + bias_ref[...]
        o_ref[r0:r0 + SG_CHUNK, :] = u[r0:r0 + SG_CHUNK, :] * y


def _sg_call(p, off_sg, lg, lb, ws, bias):
    t = p.shape[0]
    d_sg = lg.shape[1]
    half = d_sg // 2
    c0 = off_sg // half
    specs = [pl.BlockSpec((TM, half), (lambda i, c=c0 + k: (i, c))) for k in range(4)]
    return pl.pallas_call(
        _sg_kernel,
        grid=(t // TM,),
        in_specs=specs + [
            pl.BlockSpec((1, d_sg), lambda i: (0, 0)),
            pl.BlockSpec((1, d_sg), lambda i: (0, 0)),
            pl.BlockSpec((SG_HEADS, SG_CHUNK, SG_CHUNK), lambda i: (0, 0, 0)),
            pl.BlockSpec((SG_CHUNK, d_sg), lambda i: (0, 0)),
        ],
        out_specs=pl.BlockSpec((TM, d_sg), lambda i: (i, 0)),
        out_shape=jax.ShapeDtypeStruct((t, d_sg), F32),
        compiler_params=_cparams("arbitrary"),
        name="spatial_gating",
    )(p, p, p, p, lg, lb, ws, bias)


def _hyconv_kernel(x_ref, xp_ref, xn_ref, cw_ref, cb_ref, o_ref, xe_s, *, n_chunks):
    rows = x_ref.shape[0]
    j = pl.program_id(1) % n_chunks
    xe_s[0:HALO, :] = jnp.where(j > 0, xp_ref[...], 0.0)
    xe_s[HALO:HALO + rows, :] = x_ref[...]
    xe_s[HALO + rows:HALO + rows + HALO, :] = jnp.where(j < n_chunks - 1, xn_ref[...], 0.0)
    left = HY_CONV // 2
    y = cb_ref[...]
    for k in range(HY_CONV):
        y = y + cw_ref[k:k + 1, :] * xe_s[pl.ds(HALO + k - left, rows), :]
    o_ref[0] = y


def _hyconv_call(p, off_hy, cw, cb, *, batch, n_lat):
    t = p.shape[0]
    d_hy = cw.shape[1] // 3
    rows = math.gcd(n_lat, HYCONV_ROWS)
    n_chunks = n_lat // rows
    c0 = off_hy // d_hy
    per = rows // HALO
    last = t // HALO - 1
    kern = functools.partial(_hyconv_kernel, n_chunks=n_chunks)
    return pl.pallas_call(
        kern,
        grid=(3, batch * n_chunks),
        in_specs=[
            pl.BlockSpec((rows, d_hy), lambda c, i: (i, c0 + c)),
            pl.BlockSpec((HALO, d_hy), lambda c, i: (jnp.maximum(i * per - 1, 0), c0 + c)),
            pl.BlockSpec((HALO, d_hy), lambda c, i: (jnp.minimum((i + 1) * per, last), c0 + c)),
            pl.BlockSpec((HY_CONV, d_hy), lambda c, i: (0, c)),
            pl.BlockSpec((1, d_hy), lambda c, i: (0, c)),
        ],
        out_specs=pl.BlockSpec((1, rows, d_hy), lambda c, i: (c, i, 0)),
        out_shape=jax.ShapeDtypeStruct((3, batch * n_lat, d_hy), F32),
        scratch_shapes=[pltpu.VMEM((rows + 2 * HALO, d_hy), F32)],
        compiler_params=_cparams("arbitrary", "arbitrary"),
        name="hyena_shortconv",
    )(p, p, p, cw, cb)


def _filter_kernel(z_ref, zt_ref, z0t_ref, w1_ref, b1_ref, f1_ref, w2_ref, b2_ref, f2_ref, w3h_ref, w3l_ref, b3_ref,
                   w3bh_ref, w3bl_ref, b3b_ref, dl_ref, kt_ref, ss_ref, *, length, rows):
    i = pl.program_id(0)

    def mlp(zt, w3h, w3l, b3):
        h = jnp.sin(f1_ref[...] * (_hdot(w1_ref[...], zt) + b1_ref[...]))
        h = jnp.sin(f2_ref[...] * (_hdot(w2_ref[...], h) + b2_ref[...])).T
        h_hi = h.astype(BF16)
        h_lo = (h - h_hi.astype(F32)).astype(BF16)
        return _bdot(h_hi, w3h) + _bdot(h_lo, w3h) + _bdot(h_hi, w3l) + b3

    z = z_ref[...]
    win = jnp.exp(-z[:, 0:1] * dl_ref[...])
    k = mlp(zt_ref[...], w3h_ref[...], w3l_ref[...], b3_ref[...]) * jnp.concatenate([win] * HY_ORDER, axis=-1)
    lag0_back = mlp(z0t_ref[...], w3bh_ref[...], w3bl_ref[...], b3b_ref[...])[0:1, :]
    n = i * rows + lax.broadcasted_iota(jnp.int32, (rows, 1), 0)
    k = k + jnp.where(n == 0, lag0_back, 0.0)
    k = jnp.where(n == length, 0.0, k)
    kt_ref[...] = k

    @pl.when(i == 0)
    def _():
        ss_ref[...] = jnp.zeros_like(ss_ref)

    ss_ref[0:1, :] += jnp.sum(k * k, axis=0, keepdims=True)


def _filter_features(length):
    n = np.arange(2 * length)
    j = np.where(n < length, n, 2 * length - n).astype(np.float64)
    t = j / (length - 1)
    w = (2.0 * math.pi / length) * j
    f = np.linspace(1e-4, HY_BANDS - 1, HY_BANDS)
    z = np.zeros((2 * length, HY_EMB_PAD), np.float64)
    z[:, 0] = t
    z[:, 1:1 + HY_BANDS] = np.cos(w[:, None] * f[None, :])
    z[:, 1 + HY_BANDS:HY_EMB] = -np.sin(w[:, None] * f[None, :])
    return jnp.asarray(z, F32)


def _filter_call(length, w1, b1, f1, w2, b2, f2, w3, b3, d_hy):
    rows = TM
    hid = w1.shape[1]
    nc = HY_ORDER * d_hy
    ztab = _filter_features(length)
    ztab_t = ztab.T
    w1t = jnp.zeros((hid, HY_EMB_PAD), F32).at[:, :HY_EMB].set(w1.T)
    w3_hi = w3.astype(BF16)
    w3_lo = (w3 - w3_hi.astype(F32)).astype(BF16)
    deltas = jnp.asarray(np.abs(np.linspace(HY_MIN_DECAY, HY_MAX_DECAY, d_hy))[None, :], F32)
    n_fwd = length // rows
    kern = functools.partial(_filter_kernel, length=length, rows=rows)
    full = lambda shape: pl.BlockSpec(shape, lambda i: (0,) * len(shape))
    half = lambda i: (0, (i >= n_fwd).astype(jnp.int32))
    return pl.pallas_call(
        kern,
        grid=(2 * length // rows,),
        in_specs=[
            pl.BlockSpec((rows, HY_EMB_PAD), lambda i: (i, 0)),
            pl.BlockSpec((HY_EMB_PAD, rows), lambda i: (0, i)),
            pl.BlockSpec((HY_EMB_PAD, LANES), lambda i: (0, 0)),
            full((hid, HY_EMB_PAD)), full((hid, 1)), full((hid, 1)),
            full((hid, hid)), full((hid, 1)), full((hid, 1)),
            pl.BlockSpec((hid, nc), half), pl.BlockSpec((hid, nc), half), pl.BlockSpec((1, nc), half),
            pl.BlockSpec((hid, nc), lambda i: (0, 1)), pl.BlockSpec((hid, nc), lambda i: (0, 1)),
            pl.BlockSpec((1, nc), lambda i: (0, 1)),
            full((1, d_hy)),
        ],
        out_specs=[
            pl.BlockSpec((rows, nc), lambda i: (i, 0)),
            pl.BlockSpec((SUBLANES, nc), lambda i: (0, 0)),
        ],
        out_shape=[
            jax.ShapeDtypeStruct((2 * length, nc), F32),
            jax.ShapeDtypeStruct((SUBLANES, nc), F32),
        ],
        compiler_params=_cparams("arbitrary"),
        name="hyena_filter",
    )(ztab, ztab_t, ztab_t, w1t, b1[:, None], f1[:, None], w2.T, b2[:, None], f2[:, None],
      w3_hi, w3_lo, b3[None], w3_hi, w3_lo, b3[None], deltas)


def _dft_tables(length):
    n_fft = 2 * length
    n1 = n_fft // DFT_INNER
    n1h = n1 // 2
    unit = 2.0 * math.pi / n_fft
    k1 = jnp.arange(n1, dtype=jnp.int32)
    n2 = jnp.arange(DFT_INNER, dtype=jnp.int32)

    def cs(n1_count):
        n = DFT_INNER * jnp.arange(n1_count, dtype=jnp.int32)[None, None, :] + n2[:, None, None]
        th = ((k1[None, :, None] * n) % n_fft).astype(F32) * unit
        return jnp.cos(th), jnp.sin(th)

    c, s = cs(n1h)
    outer = jnp.concatenate([jnp.concatenate([c, s], -1), jnp.concatenate([-s, c], -1)], 1)
    ct, st = jnp.swapaxes(c, 1, 2) / n_fft, jnp.swapaxes(s, 1, 2) / n_fft
    outer_inv = jnp.concatenate([jnp.concatenate([ct, -st], -1), jnp.concatenate([st, ct], -1)], 1)
    c, s = cs(n1)
    outer_real = jnp.concatenate([c, -s], 1)
    m = np.arange(DFT_INNER)
    th = 2.0 * math.pi * np.outer(m, m) / DFT_INNER
    c, s = np.cos(th), np.sin(th)
    as_bf = lambda a: jnp.asarray(a, F32).astype(BF16)

    def per_step(w):
        steps, m, r = DFT_INNER // SUBLANES, w.shape[1], w.shape[2]
        w = w.reshape(steps, SUBLANES, m, r).transpose(0, 2, 3, 1).reshape(steps, m, 1, r * SUBLANES)
        k = lax.broadcasted_iota(jnp.int32, (1, 1, SUBLANES, r * SUBLANES), 2)
        j = lax.broadcasted_iota(jnp.int32, (1, 1, SUBLANES, r * SUBLANES), 3) % SUBLANES
        return jnp.where(j == k, w, 0.0).reshape(steps, m * SUBLANES, r * SUBLANES).astype(BF16)

    return dict(outer=per_step(outer), outer_real=per_step(outer_real), outer_inv=per_step(outer_inv),
                inner=as_bf(np.block([[c, s], [-s, c]])), inner_inv=as_bf(np.block([[c, -s], [s, c]])), n1=n1)


def _dft_outer_kernel(x_ref, w_ref, o_ref):
    x = x_ref[...]
    r = _bdot(w_ref[0], x.reshape(-1, x.shape[-1]).astype(BF16))
    o_ref[...] = r.reshape(o_ref.shape)


def _dft_outer_call(x, lead, w, ct):
    c = x.shape[-1]
    m, r = w.shape[1] // SUBLANES, w.shape[2] // SUBLANES
    if lead is None:
        xspec = pl.BlockSpec((x.shape[0], SUBLANES, ct), lambda cc, i: (0, i, cc))
    else:
        xspec = pl.BlockSpec((None, 2, x.shape[2], SUBLANES, ct), lambda cc, i: (lead, 0, 0, i, cc))
    return pl.pallas_call(
        _dft_outer_kernel,
        grid=(c // ct, DFT_INNER // SUBLANES),
        in_specs=[xspec, pl.BlockSpec((1, m * SUBLANES, r * SUBLANES), lambda cc, i: (i, 0, 0))],
        out_specs=pl.BlockSpec((2, m // 2, SUBLANES, ct), lambda cc, i: (0, 0, i, cc)),
        out_shape=jax.ShapeDtypeStruct((2, m // 2, DFT_INNER, c), F32),
        compiler_params=_cparams("arbitrary", "arbitrary"),
        name="hyena_dft_outer",
    )(x, w)


def _filter_spectrum_kernel(a_ref, w_ref, o_ref):
    for q in range(K1_PER_STEP):
        x = _bdot(w_ref[...], jnp.concatenate([a_ref[0, q], a_ref[1, q]], axis=0).astype(BF16))
        o_ref[0, q] = x[:DFT_INNER].astype(BF16)
        o_ref[1, q] = x[DFT_INNER:].astype(BF16)


def _filter_spectrum_call(a, w_inner):
    _, n1, _, c = a.shape
    blk = pl.BlockSpec((2, K1_PER_STEP, DFT_INNER, c), lambda k: (0, k, 0, 0))
    return pl.pallas_call(
        _filter_spectrum_kernel,
        grid=(n1 // K1_PER_STEP,),
        in_specs=[blk, pl.BlockSpec((2 * DFT_INNER, 2 * DFT_INNER), lambda k: (0, 0))],
        out_specs=blk,
        out_shape=jax.ShapeDtypeStruct(a.shape, BF16),
        compiler_params=_cparams("arbitrary"),
        name="hyena_filter_spectrum",
    )(a, w_inner)


def _spectral_kernel(a_ref, kf_ref, w_ref, wi_ref, o_ref):
    for q in range(K1_PER_STEP):
        x = _bdot(w_ref[...], jnp.concatenate([a_ref[0, q], a_ref[1, q]], axis=0).astype(BF16))
        xr, xi = x[:DFT_INNER], x[DFT_INNER:]
        kr, ki = kf_ref[0, q].astype(F32), kf_ref[1, q].astype(F32)
        yr = xr * kr - xi * ki
        yi = xr * ki + xi * kr
        c = _bdot(wi_ref[...], jnp.concatenate([yr, yi], axis=0).astype(BF16))
        o_ref[0, q] = c[:DFT_INNER]
        o_ref[1, q] = c[DFT_INNER:]


def _spectral_call(a, kf, order, w_inner, w_inner_inv):
    _, n1, _, c = a.shape
    blk = pl.BlockSpec((2, K1_PER_STEP, DFT_INNER, c), lambda k: (0, k, 0, 0))
    sq = pl.BlockSpec((2 * DFT_INNER, 2 * DFT_INNER), lambda k: (0, 0))
    return pl.pallas_call(
        _spectral_kernel,
        grid=(n1 // K1_PER_STEP,),
        in_specs=[blk, pl.BlockSpec((2, K1_PER_STEP, DFT_INNER, c), lambda k: (0, k, 0, order)), sq, sq],
        out_specs=blk,
        out_shape=jax.ShapeDtypeStruct(a.shape, F32),
        compiler_params=_cparams("arbitrary"),
        name="hyena_spectral",
    )(a, kf, w_inner, w_inner_inv)


def _dft_outer_inv_kernel(c_ref, w_ref, z_ref, gate_ref, skip_ref, ss_ref, o_ref):
    scale = lax.rsqrt(ss_ref[...] + EPS)
    cc = c_ref[...]
    y = _bdot(w_ref[0], cc.reshape(-1, cc.shape[-1]).astype(BF16)).reshape(o_ref.shape)
    o_ref[...] = gate_ref[...] * (scale * y + skip_ref[...] * z_ref[...])


def _dft_outer_inv_call(cp, w, zs, z_lead, gates, gate_lead, skip, ss):
    _, _, n1h, _, c = zs.shape
    n1 = cp.shape[1]
    zspec = lambda lead: pl.BlockSpec((None, 2, n1h, SUBLANES, c), lambda i: (lead, 0, 0, i, 0))
    vspec = pl.BlockSpec((1, c), lambda i: (0, 0))
    return pl.pallas_call(
        _dft_outer_inv_kernel,
        grid=(DFT_INNER // SUBLANES,),
        in_specs=[pl.BlockSpec((2, n1, SUBLANES, c), lambda i: (0, 0, i, 0)),
                  pl.BlockSpec((1, 2 * n1h * SUBLANES, 2 * n1 * SUBLANES), lambda i: (i, 0, 0)),
                  zspec(z_lead), zspec(gate_lead), vspec, vspec],
        out_specs=zspec(0),
        out_shape=jax.ShapeDtypeStruct((1, 2, n1h, DFT_INNER, c), F32),
        compiler_params=_cparams("arbitrary"),
        name="hyena_dft_outer_inv",
    )(cp, w, zs, gates, skip, ss)


def _ctx_dft_mats(n_ctx):
    n_fft = 2 * n_ctx
    f = np.arange(n_fft)
    th = 2.0 * math.pi * np.outer(f, np.arange(n_ctx)) / n_fft
    fz = np.concatenate([np.cos(th), -np.sin(th)], axis=0)
    th = 2.0 * math.pi * np.outer(f, f) / n_fft
    fk = np.concatenate([np.cos(th), -np.sin(th)], axis=0)
    th = 2.0 * math.pi * np.outer(np.arange(n_ctx), f) / n_fft
    fi = np.concatenate([np.cos(th), -np.sin(th)], axis=1) / n_fft
    as_bf = lambda a: jnp.asarray(a, F32).astype(BF16)
    return as_bf(fz), as_bf(fk), as_bf(fi)


def _hyctx_kernel(p0_ref, p1_ref, p2_ref, cw_ref, cb_ref, kt_ref, ss_ref, skip_ref,
                  fz_ref, fk_ref, fi_ref, o_ref, xe_s, *, d_hy):
    n_fft = fk_ref.shape[1]
    left = HY_CONV // 2

    def conv(p_ref, c):
        xe_s[0:HALO, :] = jnp.zeros((HALO, d_hy), F32)
        xe_s[HALO:HALO + TM, :] = p_ref[...]
        xe_s[HALO + TM:HALO + TM + HALO, :] = jnp.zeros((HALO, d_hy), F32)
        y = cb_ref[:, c * d_hy:(c + 1) * d_hy]
        for k in range(HY_CONV):
            y = y + cw_ref[k:k + 1, c * d_hy:(c + 1) * d_hy] * xe_s[pl.ds(HALO + k - left, TM), :]
        return y

    z = conv(p0_ref, 0)
    gates = (conv(p1_ref, 1), conv(p2_ref, 2))
    for o in range(HY_ORDER):
        cols = slice(o * d_hy, (o + 1) * d_hy)
        kf = _bdot(fk_ref[...], kt_ref[:, cols].astype(BF16))
        zf = _bdot(fz_ref[...], z.astype(BF16))
        zr, zi = zf[:n_fft], zf[n_fft:]
        kr, ki = kf[:n_fft], kf[n_fft:]
        yf = jnp.concatenate([zr * kr - zi * ki, zr * ki + zi * kr], axis=0)
        y = _bdot(fi_ref[...], yf.astype(BF16))
        scale = lax.rsqrt(ss_ref[0:1, cols] + EPS)
        z = gates[o] * (scale * y + skip_ref[o:o + 1, :] * z)
    o_ref[...] = z


def _hyctx_call(p, off_hy, cw, cb, kt, ss, skip, mats, *, batch, n_lat, n_ctx):
    d_hy = skip.shape[1]
    assert n_ctx == TM
    c0 = off_hy // d_hy
    r0 = batch * n_lat // TM
    fz, fk, fi = mats
    full = lambda a: pl.BlockSpec(a.shape, lambda b: (0,) * a.ndim)
    return pl.pallas_call(
        functools.partial(_hyctx_kernel, d_hy=d_hy),
        grid=(batch,),
        in_specs=[pl.BlockSpec((TM, d_hy), (lambda b, c=c0 + k: (r0 + b, c))) for k in range(3)]
        + [full(cw), full(cb), full(kt), full(ss), full(skip), full(fz), full(fk), full(fi)],
        out_specs=pl.BlockSpec((TM, d_hy), lambda b: (b, 0)),
        out_shape=jax.ShapeDtypeStruct((batch * n_ctx, d_hy), F32),
        scratch_shapes=[pltpu.VMEM((TM + 2 * HALO, d_hy), F32)],
        compiler_params=_cparams("arbitrary"),
        name="hyena_ctx",
    )(p, p, p, cw, cb, kt, ss, skip, fz, fk, fi)


def _merge_kernel(x_ref, hf_ref, hb_ref, pg_ref, hy_ref, sg_ref, gg_ref, w_ref, gate_ref, o_ref):
    a = (hf_ref[...] + hb_ref[...]) * _gelu(pg_ref[...])
    y = jnp.concatenate([_rms(a), _rms(hy_ref[...]), _rms(sg_ref[...])], axis=-1) * gg_ref[...]
    o_ref[...] = x_ref[...] + gate_ref[0] * _bdot(y.astype(BF16), w_ref[...])


def _merge_call(x, hf, hb, p, hy, sg, gg, w, gate, seg_of_tile, n_rows):
    d = x.shape[1]
    d_lru, d_hy, d_sg = hf.shape[1], hy.shape[1], sg.shape[1]
    row = lambda width, col=0: pl.BlockSpec((TM, width), lambda i: (i, col))
    return pl.pallas_call(
        _merge_kernel,
        grid=(n_rows // TM,),
        in_specs=[row(d), row(d_lru), row(d_lru), row(d_lru, 1), row(d_hy), row(d_sg),
                  pl.BlockSpec((1, d), lambda i: (0, 0)),
                  pl.BlockSpec((d, d), lambda i: (0, 0)),
                  pl.BlockSpec((1, 1, d), lambda i: (seg_of_tile(i), 0, 0))],
        out_specs=row(d),
        out_shape=jax.ShapeDtypeStruct((n_rows, d), F32),
        compiler_params=_cparams("arbitrary"),
        name="merge_out_proj",
    )(x, hf, hb, p, hy, sg, gg, w, gate)


def _pack_bf16_pairs(x, is_bf16_valued=False):
    half = x.shape[1] // 2
    lo, hi = x[:, :half], x[:, half:]
    if not is_bf16_valued:
        lo, hi = lo.astype(BF16).astype(F32), hi.astype(BF16).astype(F32)
    return (pltpu.bitcast(lo, jnp.uint32) >> 16) | (pltpu.bitcast(hi, jnp.uint32) & jnp.uint32(0xFFFF0000))


def _unpack_bf16_pairs(u):
    lo = pltpu.bitcast(u << 16, F32).astype(BF16)
    hi = pltpu.bitcast(u & jnp.uint32(0xFFFF0000), F32).astype(BF16)
    return lo, hi


def _router_kernel(x_ref, g_ref, sc_ref, sh_ref, rw_ref, rb_ref, f_ref, e_ref, gt_ref, m_ref):
    f = _rms(x_ref[...]) * g_ref[...] * (1.0 + sc_ref[0]) + sh_ref[0]
    f_hi = f.astype(BF16)
    f_ref[...] = f_hi
    f_lo = (f - f_hi.astype(F32)).astype(BF16)
    part = _bdot(f_hi, rw_ref[...])
    logits = part[:, :LANES] + part[:, LANES:] + _bdot(f_lo, rw_ref[:, :LANES]) + rb_ref[...]
    lane = lax.broadcasted_iota(jnp.int32, logits.shape, 1).astype(F32)
    work = logits
    e_out = jnp.zeros_like(logits)
    mask = jnp.zeros_like(logits)
    vals = []
    for k in range(TOP_K):
        mx = jnp.max(work, axis=-1, keepdims=True)
        idx = jnp.min(jnp.where(work == mx, lane, float(LANES)), axis=-1, keepdims=True)
        sel = lane == idx
        e_out = jnp.where(lane == float(k), idx, e_out)
        mask = jnp.where(sel, 1.0, mask)
        work = jnp.where(sel, -jnp.inf, work)
        vals.append(mx)
    ex = [jnp.exp(v - vals[0]) for v in vals]
    den = ex[0]
    for v in ex[1:]:
        den = den + v
    g_out = jnp.zeros_like(logits)
    for k in range(TOP_K):
        g_out = jnp.where(lane == float(k), ex[k] / den, g_out)
    e_ref[...] = e_out.astype(jnp.int32)
    gt_ref[...] = g_out
    m_ref[...] = mask


def _router_call(x, g, sc, sh, rw, rb, seg_of_tile, n_rows):
    d = x.shape[1]
    row = lambda width: pl.BlockSpec((TM, width), lambda i: (i, 0))
    mod = pl.BlockSpec((1, 1, d), lambda i: (seg_of_tile(i), 0, 0))
    return pl.pallas_call(
        _router_kernel,
        grid=(n_rows // TM,),
        in_specs=[row(d), pl.BlockSpec((1, d), lambda i: (0, 0)), mod, mod,
                  pl.BlockSpec((d, 2 * LANES), lambda i: (0, 0)), pl.BlockSpec((1, LANES), lambda i: (0, 0))],
        out_specs=[row(d), row(LANES), row(LANES), row(LANES)],
        out_shape=[jax.ShapeDtypeStruct((n_rows, d), BF16), jax.ShapeDtypeStruct((n_rows, LANES), jnp.int32),
                   jax.ShapeDtypeStruct((n_rows, LANES), F32), jax.ShapeDtypeStruct((n_rows, LANES), F32)],
        compiler_params=_cparams("arbitrary"),
        name="moe_router",
    )(x, g, sc, sh, rw, rb)


def _rank_kernel(m_ref, e_ref, tri_ref, utri_ref, ls_ref, meta_ref, tot_ref, carry_s):
    i = pl.program_id(0)

    @pl.when(i == 0)
    def _():
        carry_s[...] = jnp.zeros_like(carry_s)

    m = m_ref[...]
    rank = _bdot(tri_ref[...], m.astype(BF16))
    cnt = jnp.sum(m, axis=0, keepdims=True)
    seg = jnp.floor((cnt + (SEG_ROWS - 1)) * (1.0 / SEG_ROWS)) * SEG_ROWS
    seg8 = jnp.broadcast_to(seg, (SUBLANES, LANES))
    loc = _bdot(seg8.astype(BF16), utri_ref[...])[0:1, :]
    sd = rank + loc
    e = e_ref[...]
    lane = lax.broadcasted_iota(jnp.int32, sd.shape, 1)
    out = jnp.zeros_like(sd)
    for k in range(TOP_K):
        sk = jnp.sum(jnp.where(lane == e[:, k:k + 1], sd, 0.0), axis=-1, keepdims=True)
        out = jnp.where(lane == k, sk, out)
    ls_ref[...] = out.astype(jnp.int32)
    row = lax.broadcasted_iota(jnp.int32, (SUBLANES, LANES), 0)
    meta_ref[0] = jnp.where(row == 0, carry_s[...], jnp.where(row == 1, seg8, 0.0))
    carry_s[...] = carry_s[...] + seg8
    tot_ref[...] = carry_s[...]


def _rank_call(mask, e4):
    n = mask.shape[0]
    tri = jnp.asarray(np.tril(np.ones((TM, TM)), -1), BF16)
    utri = jnp.asarray(np.triu(np.ones((LANES, LANES)), 1), BF16)
    row = pl.BlockSpec((TM, LANES), lambda i: (i, 0))
    return pl.pallas_call(
        _rank_kernel,
        grid=(n // TM,),
        in_specs=[row, row, pl.BlockSpec((TM, TM), lambda i: (0, 0)), pl.BlockSpec((LANES, LANES), lambda i: (0, 0))],
        out_specs=[row, pl.BlockSpec((1, SUBLANES, LANES), lambda i: (i, 0, 0)),
                   pl.BlockSpec((SUBLANES, LANES), lambda i: (0, 0))],
        out_shape=[jax.ShapeDtypeStruct((n, LANES), jnp.int32), jax.ShapeDtypeStruct((n // TM, SUBLANES, LANES), F32),
                   jax.ShapeDtypeStruct((SUBLANES, LANES), F32)],
        scratch_shapes=[pltpu.VMEM((SUBLANES, LANES), F32)],
        compiler_params=_cparams("arbitrary"),
        name="moe_rank",
    )(mask, e4, tri, utri)


def _seg_copy(src, dst, src_row, dst_row, sem):
    return pltpu.make_async_copy(src.at[pl.ds(src_row, SEG_ROWS)], dst.at[pl.ds(dst_row, SEG_ROWS)], sem)


def _drain(copy, count):
    def body(r, c):
        copy.wait()
        return c

    lax.fori_loop(0, count, body, 0)


def _tile_segments(seg_ref, nch_ref, tile, move):
    off = jnp.int32(0)
    for e in range(N_EXPERTS):
        nc = nch_ref[tile * N_EXPERTS + e]
        g0 = seg_ref[tile * N_EXPERTS + e]

        def body(c, carry, off=off, g0=g0):
            move(pl.multiple_of(off + c * SEG_ROWS, SEG_ROWS), pl.multiple_of(g0 + c * SEG_ROWS, SEG_ROWS))
            return carry

        lax.fori_loop(0, nc, body, 0)
        off = off + nc * SEG_ROWS
    return off // SEG_ROWS


def _tile_chunks(nch_ref, tile):
    total = jnp.int32(0)
    for e in range(N_EXPERTS):
        total = total + nch_ref[tile * N_EXPERTS + e]
    return total


def _dispatch_kernel(seg_ref, nch_ref, pad_ref, f_ref, ls_ref, xs_hbm, xl_s, zero_s, sem, zsem):
    i = pl.program_id(0)
    n = pl.num_programs(0)
    rows = xl_s.shape[1]
    cur = i % 2

    def sent(tile, b):
        _drain(_seg_copy(xl_s.at[b], xs_hbm, 0, 0, sem.at[b]), _tile_chunks(nch_ref, tile))

    @pl.when(i >= 2)
    def _():
        sent(i - 2, cur)

    ls_t = ls_ref[...].astype(F32).T
    r_iota = lax.broadcasted_iota(jnp.int32, (rows, TM), 0).astype(F32)
    perm = jnp.zeros((rows, TM), F32)
    for k in range(TOP_K):
        perm = jnp.where(r_iota == ls_t[k:k + 1, :], 1.0, perm)
    xl_s[cur] = _pack_bf16_pairs(_bdot(perm.astype(BF16), f_ref[...]), is_bf16_valued=True)

    _tile_segments(seg_ref, nch_ref, i, lambda lr, gr: _seg_copy(xl_s.at[cur], xs_hbm, lr, gr, sem.at[cur]).start())

    @pl.when(i == 0)
    def _():
        zero_s[...] = jnp.zeros_like(zero_s)
        total = jnp.int32(0)
        for e in range(N_EXPERTS):
            nc = pad_ref[N_EXPERTS + e]
            g0 = pad_ref[e]

            def body(c, carry, g0=g0):
                _seg_copy(zero_s, xs_hbm, 0, pl.multiple_of(g0 + c * SEG_ROWS, SEG_ROWS), zsem).start()
                return carry

            lax.fori_loop(0, nc, body, 0)
            total = total + nc
        _drain(_seg_copy(zero_s, xs_hbm, 0, 0, zsem), total)

        def block_copy(b):
            return pltpu.make_async_copy(zero_s, xs_hbm.at[pl.ds(pl.multiple_of(b * MOE_BLOCK, MOE_BLOCK), MOE_BLOCK)], zsem)

        n_used = pad_ref[2 * N_EXPERTS]
        n_blocks = xs_hbm.shape[0] // MOE_BLOCK

        def start_block(b, carry):
            block_copy(b).start()
            return carry

        def wait_block(b, carry):
            block_copy(b).wait()
            return carry

        lax.fori_loop(n_used, n_blocks, start_block, 0)
        lax.fori_loop(n_used, n_blocks, wait_block, 0)

    @pl.when(i == n - 1)
    def _():
        @pl.when(i >= 1)
        def _():
            sent(i - 1, 1 - cur)

        sent(i, cur)


def _dispatch_call(seg, nch, pad, f, ls, n_rows):
    n, d = f.shape
    return pl.pallas_call(
        _dispatch_kernel,
        grid_spec=pltpu.PrefetchScalarGridSpec(
            num_scalar_prefetch=3,
            grid=(n // TM,),
            in_specs=[pl.BlockSpec((TM, d), lambda i, *_: (i, 0)), pl.BlockSpec((TM, LANES), lambda i, *_: (i, 0))],
            out_specs=pl.BlockSpec(memory_space=pl.ANY),
            scratch_shapes=[pltpu.VMEM((2, LOCAL_ROWS, d // 2), jnp.uint32), pltpu.VMEM((MOE_BLOCK, d // 2), jnp.uint32),
                            pltpu.SemaphoreType.DMA((2,)), pltpu.SemaphoreType.DMA(())],
        ),
        out_shape=jax.ShapeDtypeStruct((n_rows, d // 2), jnp.uint32),
        compiler_params=_cparams("arbitrary", row_dma=True),
        name="moe_dispatch",
    )(seg, nch, pad, f, ls)


def _expert_kernel(be_ref, nu_ref, x_ref, w1_ref, b1_ref, w2_ref, b2_ref, o_ref, w1_s, w2_s):
    i = pl.program_id(0)
    d_e = w2_ref.shape[2]
    half = x_ref.shape[1]
    live = i < nu_ref[0]
    new_expert = jnp.logical_or(i == 0, be_ref[i] != be_ref[jnp.maximum(i - 1, 0)])

    @pl.when(jnp.logical_and(live, new_expert))
    def _():
        w1_s[...] = w1_ref[0, 0].astype(BF16)
        w2_s[...] = w2_ref[0, 0].astype(BF16)

    @pl.when(live)
    def _():
        lo, hi = _unpack_bf16_pairs(x_ref[...])
        gu = _bdot(lo, w1_s[:half, :]) + _bdot(hi, w1_s[half:, :]) + b1_ref[0, 0]
        glu = jnp.minimum(gu[:, :d_e], SWIGLU_LIMIT)
        lin = jnp.clip(gu[:, d_e:], -SWIGLU_LIMIT, SWIGLU_LIMIT)
        act = glu * _sigmoid(SWIGLU_ALPHA * glu) * (lin + 1.0)
        o_ref[...] = _pack_bf16_pairs(_bdot(act.astype(BF16), w2_s[...]) + b2_ref[0, 0])

    @pl.when(i >= nu_ref[0])
    def _():
        o_ref[...] = jnp.zeros_like(o_ref)


def _expert_call(block_e, n_used, xs, w1, b1, w2, b2, layer):
    n_rows, half = xs.shape
    depth, n_e, d, d_gu = w1.shape
    d_e = w2.shape[2]
    used = lambda i, nu: jnp.minimum(i, nu[0] - 1)
    return pl.pallas_call(
        _expert_kernel,
        grid_spec=pltpu.PrefetchScalarGridSpec(
            num_scalar_prefetch=2,
            grid=(n_rows // MOE_BLOCK,),
            in_specs=[
                pl.BlockSpec((MOE_BLOCK, half), lambda i, be, nu: (used(i, nu), 0)),
                pl.BlockSpec((1, 1, d, d_gu), lambda i, be, nu: (layer, be[i], 0, 0)),
                pl.BlockSpec((1, 1, 1, d_gu), lambda i, be, nu: (layer, be[i], 0, 0)),
                pl.BlockSpec((1, 1, d_e, d), lambda i, be, nu: (layer, be[i], 0, 0)),
                pl.BlockSpec((1, 1, 1, d), lambda i, be, nu: (layer, be[i], 0, 0)),
            ],
            out_specs=pl.BlockSpec((MOE_BLOCK, half), lambda i, be, nu: (i, 0)),
            scratch_shapes=[pltpu.VMEM((d, d_gu), BF16), pltpu.VMEM((d_e, d), BF16)],
        ),
        out_shape=jax.ShapeDtypeStruct((n_rows, half), jnp.uint32),
        compiler_params=_cparams("arbitrary"),
        name="moe_experts",
    )(block_e, n_used, xs, w1, b1.reshape(depth, n_e, 1, d_gu), w2, b2.reshape(depth, n_e, 1, d))


def _combine_kernel(seg_ref, nch_ref, x_ref, ls_ref, g_ref, gate_ref, ys_hbm, o_ref, yl_s, sem):
    i = pl.program_id(0)
    n = pl.num_programs(0)
    rows = yl_s.shape[1]

    def fetch(tile, b):
        return _tile_segments(seg_ref, nch_ref, tile,
                              lambda lr, gr: _seg_copy(ys_hbm, yl_s.at[b], gr, lr, sem.at[b]).start())

    @pl.when(i == 0)
    def _():
        yl_s[...] = jnp.zeros_like(yl_s)
        fetch(0, 0)

    @pl.when(i + 1 < n)
    def _():
        fetch(i + 1, (i + 1) % 2)

    cur = i % 2
    total = jnp.int32(0)
    for e in range(N_EXPERTS):
        total = total + nch_ref[i * N_EXPERTS + e]
    _drain(_seg_copy(ys_hbm, yl_s.at[cur], 0, 0, sem.at[cur]), total)

    ls = ls_ref[...]
    g = g_ref[...]
    r_iota = lax.broadcasted_iota(jnp.int32, (TM, rows), 1)
    wsel = jnp.zeros((TM, rows), F32)
    for k in range(TOP_K):
        wsel = jnp.where(r_iota == ls[:, k:k + 1], g[:, k:k + 1], wsel)
    wsel = wsel.astype(BF16)
    lo, hi = _unpack_bf16_pairs(yl_s[cur])
    acc = jnp.concatenate([_bdot(wsel, lo), _bdot(wsel, hi)], axis=-1)
    o_ref[...] = x_ref[...] + gate_ref[0] * acc


def _combine_call(seg, nch, x, ls, g4, gate, ys, seg_of_tile, n_tok):
    d = x.shape[1]
    row = lambda width: pl.BlockSpec((TM, width), lambda i, *_: (i, 0))
    return pl.pallas_call(
        _combine_kernel,
        grid_spec=pltpu.PrefetchScalarGridSpec(
            num_scalar_prefetch=2,
            grid=(n_tok // TM,),
            in_specs=[row(d), row(LANES), row(LANES),
                      pl.BlockSpec((1, 1, d), lambda i, *_: (seg_of_tile(i), 0, 0)),
                      pl.BlockSpec(memory_space=pl.ANY)],
            out_specs=row(d),
            scratch_shapes=[pltpu.VMEM((2, LOCAL_ROWS, d // 2), jnp.uint32), pltpu.SemaphoreType.DMA((2,))],
        ),
        out_shape=jax.ShapeDtypeStruct((n_tok, d), F32),
        compiler_params=_cparams("arbitrary", row_dma=True),
        name="moe_combine",
    )(seg, nch, x, ls, g4, gate, ys)


def _moe(x, norm_g, sc, sh, gate, rw, rb, w1, b1, w2, b2, layer, seg_of_row_tile, n_tok):
    n_tiles = n_tok // TM
    f, e4, g4, mask = _router_call(x, norm_g, sc, sh, rw, rb, seg_of_row_tile(TM), n_tok)
    ls, meta, tot = _rank_call(mask, e4)
    rows_e = tot[0, :N_EXPERTS].astype(jnp.int32)
    region = (rows_e + MOE_BLOCK - 1) // MOE_BLOCK * MOE_BLOCK
    pend = jnp.cumsum(region)
    pstart = pend - region
    max_rows = n_tok * TOP_K + n_tiles * N_EXPERTS * (SEG_ROWS - 1) + N_EXPERTS * (MOE_BLOCK - 1)
    n_blocks = -(-max_rows // MOE_BLOCK)
    first_row = jnp.arange(n_blocks, dtype=jnp.int32) * MOE_BLOCK
    block_e = jnp.minimum(jnp.sum((pend[None, :] <= first_row[:, None]).astype(jnp.int32), axis=1), N_EXPERTS - 1)
    n_used = (pend[-1:] // MOE_BLOCK).astype(jnp.int32)
    seg = (pstart[None, :] + meta[:, 0, :N_EXPERTS].astype(jnp.int32)).reshape(-1)
    nch = (meta[:, 1, :N_EXPERTS].astype(jnp.int32) // SEG_ROWS).reshape(-1)
    pad = jnp.concatenate([pstart + rows_e, (region - rows_e) // SEG_ROWS, n_used])
    xs = _dispatch_call(seg, nch, pad, f, ls, n_blocks * MOE_BLOCK)
    ys = _expert_call(block_e, n_used, xs, w1, b1, w2, b2, layer)
    return _combine_call(seg, nch, x, ls, g4, gate, ys, seg_of_row_tile(TM), n_tok)


def _final_kernel(x_ref, g_ref, o_ref):
    o_ref[...] = _rms(x_ref[...]) * g_ref[...]


def _final_call(x, g, n_rows):
    d = x.shape[1]
    return pl.pallas_call(
        _final_kernel,
        grid=(n_rows // TM,),
        in_specs=[pl.BlockSpec((TM, d), lambda i: (i, 0)), pl.BlockSpec((1, d), lambda i: (0, 0))],
        out_specs=pl.BlockSpec((TM, d), lambda i: (i, 0)),
        out_shape=jax.ShapeDtypeStruct((n_rows, d), F32),
        compiler_params=_cparams("arbitrary"),
        name="final_norm",
    )(x, g)


def _blockdiag(w):
    h, hd, _ = w.shape
    return jnp.einsum('hij,hg->higj', w, jnp.eye(h, dtype=w.dtype)).reshape(h * hd, h * hd)


def kernel(x, c, ctx, c_ctx, ada_w, ada_b, norm_mix_g, w_in, lru_conv_w, lru_conv_b, lru_wa, lru_ba, lru_wx, lru_bx, lru_lam, hy_conv_w, hy_conv_b, hf_w1, hf_b1, hf_f1, hf_w2, hf_b2, hf_f2, hf_w3, hf_b3, hy_skip, sg_ln_g, sg_ln_b, sg_w, sg_b, grp_norm_g, w_out, norm_ffn_g, router_w, router_b, moe_w1, moe_b1, moe_w2, moe_b2, final_norm_g):
    batch, n_lat, d = x.shape
    n_ctx = ctx.shape[1]
    depth = ada_w.shape[0]
    d_lru = lru_conv_w.shape[2]
    d_hy = hy_skip.shape[2]
    d_sg = sg_ln_g.shape[1]
    off_hy = 2 * d_lru
    off_sg = off_hy + 3 * d_hy
    assert batch == 2 and n_ctx == TM and n_lat % (TM * 2) == 0 and batch + 1 <= SUBLANES
    n_lat_total = batch * n_lat
    n_all = n_lat_total + batch * n_ctx

    def seg_of_row_tile(rows):
        per_seq = n_lat // rows
        return lambda i: jnp.minimum(i // per_seq, batch)

    cond = jnp.zeros((SUBLANES, d), F32).at[:batch].set(c).at[batch].set(c_ctx)
    mods = _ada_call(cond, ada_w, ada_b)

    mats = _dft_tables(n_lat)
    ctx_mats = _ctx_dft_mats(n_ctx)
    n1 = mats["n1"]
    sg_bias_shape = (SG_CHUNK, d_sg)

    xt = jnp.concatenate([x.reshape(n_lat_total, d), ctx.reshape(batch * n_ctx, d)], axis=0)
    for l in range(depth):
        ctx_out = l < depth - 1
        m = mods[l, :batch + 1].reshape(batch + 1, N_MOD, d)
        mod = [m[:, j][:, None, :] for j in range(N_MOD)]

        p = _inproj_call(xt, norm_mix_g[l][None], mod[1], mod[0], w_in[l].astype(BF16), seg_of_row_tile(TM))

        hs = []
        for dr, rev in enumerate((False, True)):
            wg = jnp.concatenate([_blockdiag(lru_wa[l, dr]), _blockdiag(lru_wx[l, dr])], axis=1).astype(BF16)
            bg = jnp.concatenate([lru_ba[l, dr], lru_bx[l, dr]])[None]
            hs.append(_lru_call(p, lru_conv_w[l], lru_conv_b[l][None], wg, bg, lru_lam[l, dr][None],
                                reverse=rev, batch=batch, n_lat=n_lat, n_ctx=n_ctx))

        sg_bias = jnp.broadcast_to(sg_b[l].T[:, :, None], (SG_CHUNK, SG_HEADS, d_sg // SG_HEADS)).reshape(sg_bias_shape)
        sg = _sg_call(p, off_sg, sg_ln_g[l][None], sg_ln_b[l][None], sg_w[l].astype(BF16), sg_bias)

        filt = (hf_w1[l], hf_b1[l], hf_f1[l], hf_w2[l], hf_b2[l], hf_f2[l], hf_w3[l], hf_b3[l])
        kt, ss = _filter_call(n_lat, *filt, d_hy)
        ka = _dft_outer_call(kt.reshape(n1, DFT_INNER, HY_ORDER * d_hy), None, mats["outer_real"], d_hy)
        kf = _filter_spectrum_call(ka, mats["inner"])
        vxx = _hyconv_call(p, off_hy, hy_conv_w[l], hy_conv_b[l][None], batch=batch, n_lat=n_lat)
        vxx = vxx.reshape(3, batch, n1 // 2, DFT_INNER, d_hy)
        zs, z_lead = vxx, 0
        for o in range(HY_ORDER):
            a = _dft_outer_call(zs, z_lead, mats["outer"], d_hy)
            cp = _spectral_call(a, kf, o, mats["inner"], mats["inner_inv"])
            zs = _dft_outer_inv_call(cp, mats["outer_inv"], zs, z_lead, vxx, 1 + o, hy_skip[l, o][None],
                                     ss[0:1, o * d_hy:(o + 1) * d_hy])
            z_lead = 0
        hy = zs.reshape(n_lat_total, d_hy)
        if ctx_out:
            ktc, ssc = _filter_call(n_ctx, *filt, d_hy)
            hyc = _hyctx_call(p, off_hy, hy_conv_w[l], hy_conv_b[l][None], ktc, ssc, hy_skip[l], ctx_mats,
                              batch=batch, n_lat=n_lat, n_ctx=n_ctx)
            hy = jnp.concatenate([hy, hyc], axis=0)

        n_tok = n_all if ctx_out else n_lat_total
        xt = _merge_call(xt, hs[0], hs[1], p, hy, sg, grp_norm_g[l][None], w_out[l].astype(BF16), mod[2],
                         seg_of_row_tile(TM), n_tok)

        rw = jnp.zeros((d, LANES), F32).at[:, :N_EXPERTS].set(router_w[l])
        rw_hi = rw.astype(BF16)
        rw = jnp.concatenate([rw_hi, (rw - rw_hi.astype(F32)).astype(BF16)], axis=1)
        rb = jnp.full((1, LANES), -1e30, F32).at[0, :N_EXPERTS].set(router_b[l])
        xt = _moe(xt, norm_ffn_g[l][None], mod[4], mod[3], mod[5], rw, rb, moe_w1, moe_b1, moe_w2, moe_b2,
                  l, seg_of_row_tile, n_tok)

    return _final_call(xt, final_norm_g[None], n_lat_total).reshape(batch, n_lat, d)
```

```python
import functools
import math

import numpy as np
import jax
import jax.numpy as jnp
from jax import lax
from jax.experimental import pallas as pl
from jax.experimental.pallas import tpu as pltpu

F32 = jnp.float32
BF16 = jnp.bfloat16

EPS = 1e-6
N_MOD = 6
LRU_HEADS = 8
LRU_CONV = 4
LRU_C = 8.0
HY_ORDER = 2
HY_CONV = 3
HY_BANDS = 8
HY_EMB = 1 + 2 * HY_BANDS
HY_EMB_PAD = 32
HY_MIN_DECAY = math.log(1e-2) / 1.5
HY_MAX_DECAY = math.log(1e-2) / 0.3
SG_CHUNK = 128
SG_HEADS = 4
N_EXPERTS = 32
TOP_K = 4
MOE_BLOCK = 512
SWIGLU_LIMIT = 7.0
SWIGLU_ALPHA = 1.702

LANES = 128
SUBLANES = 8
TM = 256
HALO = SUBLANES
DFT_INNER = 128
K1_PER_STEP = 4
HYCONV_ROWS = 1024
SEG_ROWS = SUBLANES
LOCAL_ROWS = TM * TOP_K + N_EXPERTS * SEG_ROWS
VMEM_LIMIT = 48 * 1024 * 1024


def _cparams(*sem, vmem=VMEM_LIMIT, row_dma=False):
    return pltpu.CompilerParams(dimension_semantics=sem, vmem_limit_bytes=vmem, disable_bounds_checks=row_dma)


def _bdot(a, b):
    return jnp.dot(a, b, preferred_element_type=F32)


def _hdot(a, b):
    return jnp.dot(a, b, preferred_element_type=F32, precision=lax.Precision.HIGHEST)


def _gelu(x):
    return 0.5 * x * (1.0 + jnp.tanh(math.sqrt(2.0 / math.pi) * (x + 0.044715 * (x * x * x))))


def _sigmoid(x):
    return 0.5 * jnp.tanh(0.5 * x) + 0.5


def _rms(x):
    return x * lax.rsqrt(jnp.mean(x * x, axis=-1, keepdims=True) + EPS)


def _ada_kernel(s_ref, w_ref, b_ref, o_ref):
    s = s_ref[...]
    s = s * _sigmoid(s)
    o_ref[0] = _bdot(s.astype(BF16), w_ref[0].astype(BF16)) + b_ref[0]


def _ada_call(cond, ada_w, ada_b):
    depth, d, n = ada_w.shape
    tn = 1536
    return pl.pallas_call(
        _ada_kernel,
        grid=(depth, n // tn),
        in_specs=[
            pl.BlockSpec((SUBLANES, d), lambda l, j: (0, 0)),
            pl.BlockSpec((1, d, tn), lambda l, j: (l, 0, j)),
            pl.BlockSpec((1, 1, tn), lambda l, j: (l, 0, j)),
        ],
        out_specs=pl.BlockSpec((1, SUBLANES, tn), lambda l, j: (l, 0, j)),
        out_shape=jax.ShapeDtypeStruct((depth, SUBLANES, n), F32),
        compiler_params=_cparams("arbitrary", "arbitrary"),
        name="ada_mod",
    )(cond, ada_w, ada_b.reshape(depth, 1, n))


def _inproj_kernel(x_ref, g_ref, sc_ref, sh_ref, w_ref, o_ref):
    h = _rms(x_ref[...]) * g_ref[...] * (1.0 + sc_ref[0]) + sh_ref[0]
    o_ref[...] = _bdot(h.astype(BF16), w_ref[...])


def _inproj_call(x, g, sc, sh, w, seg_of_tile):
    t, d = x.shape
    n = w.shape[1]
    tn = n // 2
    return pl.pallas_call(
        _inproj_kernel,
        grid=(2, t // TM),
        in_specs=[
            pl.BlockSpec((TM, d), lambda j, i: (i, 0)),
            pl.BlockSpec((1, d), lambda j, i: (0, 0)),
            pl.BlockSpec((1, 1, d), lambda j, i: (seg_of_tile(i), 0, 0)),
            pl.BlockSpec((1, 1, d), lambda j, i: (seg_of_tile(i), 0, 0)),
            pl.BlockSpec((d, tn), lambda j, i: (0, j)),
        ],
        out_specs=pl.BlockSpec((TM, tn), lambda j, i: (i, j)),
        out_shape=jax.ShapeDtypeStruct((t, n), F32),
        compiler_params=_cparams("arbitrary", "arbitrary"),
        name="in_proj",
    )(x, g, sc, sh, w)


def _lru_kernel(x_ref, xp_ref, xn_ref, cw_ref, cb_ref, wg_ref, bg_ref, lam_ref, o_ref,
                xe_s, a_s, b_s, h_s, *, reverse, n_chunks, d_lru):
    s = pl.program_id(1)
    j = (n_chunks - s) if reverse else (s - 1)
    has_prev = jnp.logical_and(s > 0, j > 0)
    has_next = jnp.logical_and(s > 0, j < n_chunks - 1)

    @pl.when(s == 0)
    def _():
        h_s[...] = jnp.zeros_like(h_s)

    xe_s[0:HALO, :] = jnp.where(has_prev, xp_ref[...], 0.0)
    xe_s[HALO:HALO + TM, :] = x_ref[...]
    xe_s[HALO + TM:HALO + TM + HALO, :] = jnp.where(has_next, xn_ref[...], 0.0)
    left = LRU_CONV // 2
    xc = cb_ref[...]
    for k in range(LRU_CONV):
        xc = xc + cw_ref[k:k + 1, :] * xe_s[pl.ds(HALO + k - left, TM), :]

    g = _bdot(xc.astype(BF16), wg_ref[...]) + bg_ref[...]
    r = _sigmoid(g[:, :d_lru])
    ig = _sigmoid(g[:, d_lru:])
    lam = lam_ref[...]
    sp = jnp.maximum(-lam, 0.0) + jnp.log1p(jnp.exp(-jnp.abs(lam)))
    log_a = (-LRU_C * r) * sp
    a = jnp.exp(log_a)
    a_s[...] = a
    b_s[...] = jnp.sqrt(-jnp.tanh(log_a) * (a * a + 1.0)) * (ig * xc)

    def step(t, h):
        tt = (TM - 1 - t) if reverse else t
        h = a_s[pl.ds(tt, 1), :] * h + b_s[pl.ds(tt, 1), :]
        o_ref[pl.ds(tt, 1), :] = h
        return h

    h_s[0:1, :] = lax.fori_loop(0, TM, step, h_s[0:1, :], unroll=8)


def _lru_call(p, cw, cb, wg, bg, lam, *, reverse, batch, n_lat, n_ctx):
    t = p.shape[0]
    d_lru = cw.shape[1]
    n_chunks = n_lat // TM
    assert n_ctx == TM
    ctx0 = batch * n_chunks
    per = TM // HALO
    last = t // HALO - 1

    def rb(b, s):
        lat = b * n_chunks + ((n_chunks - s) if reverse else (s - 1))
        return jnp.where(s == 0, ctx0 + b, lat)

    kern = functools.partial(_lru_kernel, reverse=reverse, n_chunks=n_chunks, d_lru=d_lru)
    return pl.pallas_call(
        kern,
        grid=(batch, n_chunks + 1),
        in_specs=[
            pl.BlockSpec((TM, d_lru), lambda b, s: (rb(b, s), 0)),
            pl.BlockSpec((HALO, d_lru), lambda b, s: (jnp.maximum(rb(b, s) * per - 1, 0), 0)),
            pl.BlockSpec((HALO, d_lru), lambda b, s: (jnp.minimum((rb(b, s) + 1) * per, last), 0)),
            pl.BlockSpec((LRU_CONV, d_lru), lambda b, s: (0, 0)),
            pl.BlockSpec((1, d_lru), lambda b, s: (0, 0)),
            pl.BlockSpec((d_lru, 2 * d_lru), lambda b, s: (0, 0)),
            pl.BlockSpec((1, 2 * d_lru), lambda b, s: (0, 0)),
            pl.BlockSpec((1, d_lru), lambda b, s: (0, 0)),
        ],
        out_specs=pl.BlockSpec((TM, d_lru), lambda b, s: (rb(b, s), 0)),
        out_shape=jax.ShapeDtypeStruct((t, d_lru), F32),
        scratch_shapes=[
            pltpu.VMEM((TM + 2 * HALO, d_lru), F32),
            pltpu.VMEM((TM, d_lru), F32),
            pltpu.VMEM((TM, d_lru), F32),
            pltpu.VMEM((SUBLANES, d_lru), F32),
        ],
        compiler_params=_cparams("arbitrary", "arbitrary"),
        name="rglru_rev" if reverse else "rglru_fwd",
    )(p, p, p, cw, cb, wg, bg, lam)


def _sg_kernel(u0_ref, u1_ref, v0_ref, v1_ref, lg_ref, lb_ref, ws_ref, bias_ref, o_ref):
    u = _gelu(jnp.concatenate([u0_ref[...], u1_ref[...]], axis=-1))
    v = _gelu(jnp.concatenate([v0_ref[...], v1_ref[...]], axis=-1))
    mu = jnp.mean(v, axis=-1, keepdims=True)
    dv = v - mu
    var = jnp.mean(dv * dv, axis=-1, keepdims=True)
    vb = ((dv * lax.rsqrt(var + EPS)) * lg_ref[...] + lb_ref[...]).astype(BF16)
    hd = vb.shape[1] // SG_HEADS
    for ch in range(TM // SG_CHUNK):
        r0 = ch * SG_CHUNK
        ys = [_bdot(ws_ref[g], vb[r0:r0 + SG_CHUNK, g * hd:(g + 1) * hd]) for g in range(SG_HEADS)]
        y = jnp.concatenate(ys, axis=-1) + bias_ref[...]
        o_ref[r0:r0 + SG_CHUNK, :] = u[r0:r0 + SG_CHUNK, :] * y


def _sg_call(p, off_sg, lg, lb, ws, bias):
    t = p.shape[0]
    d_sg = lg.shape[1]
    half = d_sg // 2
    c0 = off_sg // half
    specs = [pl.BlockSpec((TM, half), (lambda i, c=c0 + k: (i, c))) for k in range(4)]
    return pl.pallas_call(
        _sg_kernel,
        grid=(t // TM,),
        in_specs=specs + [
            pl.BlockSpec((1, d_sg), lambda i: (0, 0)),
            pl.BlockSpec((1, d_sg), lambda i: (0, 0)),
            pl.BlockSpec((SG_HEADS, SG_CHUNK, SG_CHUNK), lambda i: (0, 0, 0)),
            pl.BlockSpec((SG_CHUNK, d_sg), lambda i: (0, 0)),
        ],
        out_specs=pl.BlockSpec((TM, d_sg), lambda i: (i, 0)),
        out_shape=jax.ShapeDtypeStruct((t, d_sg), F32),
        compiler_params=_cparams("arbitrary"),
        name="spatial_gating",
    )(p, p, p, p, lg, lb, ws, bias)


def _hyconv_kernel(x_ref, xp_ref, xn_ref, cw_ref, cb_ref, o_ref, xe_s, *, n_chunks):
    rows = x_ref.shape[0]
    j = pl.program_id(1) % n_chunks
    xe_s[0:HALO, :] = jnp.where(j > 0, xp_ref[...], 0.0)
    xe_s[HALO:HALO + rows, :] = x_ref[...]
    xe_s[HALO + rows:HALO + rows + HALO, :] = jnp.where(j < n_chunks - 1, xn_ref[...], 0.0)
    left = HY_CONV // 2
    y = cb_ref[...]
    for k in range(HY_CONV):
        y = y + cw_ref[k:k + 1, :] * xe_s[pl.ds(HALO + k - left, rows), :]
    o_ref[0] = y


def _hyconv_call(p, off_hy, cw, cb, *, batch, n_lat):
    t = p.shape[0]
    d_hy = cw.shape[1] // 3
    rows = math.gcd(n_lat, HYCONV_ROWS)
    n_chunks = n_lat // rows
    c0 = off_hy // d_hy
    per = rows // HALO
    last = t // HALO - 1
    kern = functools.partial(_hyconv_kernel, n_chunks=n_chunks)
    return pl.pallas_call(
        kern,
        grid=(3, batch * n_chunks),
        in_specs=[
            pl.BlockSpec((rows, d_hy), lambda c, i: (i, c0 + c)),
            pl.BlockSpec((HALO, d_hy), lambda c, i: (jnp.maximum(i * per - 1, 0), c0 + c)),
            pl.BlockSpec((HALO, d_hy), lambda c, i: (jnp.minimum((i + 1) * per, last), c0 + c)),
            pl.BlockSpec((HY_CONV, d_hy), lambda c, i: (0, c)),
            pl.BlockSpec((1, d_hy), lambda c, i: (0, c)),
        ],
        out_specs=pl.BlockSpec((1, rows, d_hy), lambda c, i: (c, i, 0)),
        out_shape=jax.ShapeDtypeStruct((3, batch * n_lat, d_hy), F32),
        scratch_shapes=[pltpu.VMEM((rows + 2 * HALO, d_hy), F32)],
        compiler_params=_cparams("arbitrary", "arbitrary"),
        name="hyena_shortconv",
    )(p, p, p, cw, cb)


def _filter_kernel(z_ref, zt_ref, z0t_ref, w1_ref, b1_ref, f1_ref, w2_ref, b2_ref, f2_ref, w3h_ref, w3l_ref, b3_ref,
                   w3bh_ref, w3bl_ref, b3b_ref, dl_ref, kt_ref, ss_ref, *, length, rows):
    i = pl.program_id(0)

    def mlp(zt, w3h, w3l, b3):
        h = jnp.sin(f1_ref[...] * (_hdot(w1_ref[...], zt) + b1_ref[...]))
        h = jnp.sin(f2_ref[...] * (_hdot(w2_ref[...], h) + b2_ref[...])).T
        h_hi = h.astype(BF16)
        h_lo = (h - h_hi.astype(F32)).astype(BF16)
        return _bdot(h_hi, w3h) + _bdot(h_lo, w3h) + _bdot(h_hi, w3l) + b3

    z = z_ref[...]
    win = jnp.exp(-z[:, 0:1] * dl_ref[...])
    k = mlp(zt_ref[...], w3h_ref[...], w3l_ref[...], b3_ref[...]) * jnp.concatenate([win] * HY_ORDER, axis=-1)
    lag0_back = mlp(z0t_ref[...], w3bh_ref[...], w3bl_ref[...], b3b_ref[...])[0:1, :]
    n = i * rows + lax.broadcasted_iota(jnp.int32, (rows, 1), 0)
    k = k + jnp.where(n == 0, lag0_back, 0.0)
    k = jnp.where(n == length, 0.0, k)
    kt_ref[...] = k

    @pl.when(i == 0)
    def _():
        ss_ref[...] = jnp.zeros_like(ss_ref)

    ss_ref[0:1, :] += jnp.sum(k * k, axis=0, keepdims=True)


def _filter_features(length):
    n = np.arange(2 * length)
    j = np.where(n < length, n, 2 * length - n).astype(np.float64)
    t = j / (length - 1)
    w = (2.0 * math.pi / length) * j
    f = np.linspace(1e-4, HY_BANDS - 1, HY_BANDS)
    z = np.zeros((2 * length, HY_EMB_PAD), np.float64)
    z[:, 0] = t
    z[:, 1:1 + HY_BANDS] = np.cos(w[:, None] * f[None, :])
    z[:, 1 + HY_BANDS:HY_EMB] = -np.sin(w[:, None] * f[None, :])
    return jnp.asarray(z, F32)


def _filter_call(length, w1, b1, f1, w2, b2, f2, w3, b3, d_hy):
    rows = TM
    hid = w1.shape[1]
    nc = HY_ORDER * d_hy
    ztab = _filter_features(length)
    ztab_t = ztab.T
    w1t = jnp.zeros((hid, HY_EMB_PAD), F32).at[:, :HY_EMB].set(w1.T)
    w3_hi = w3.astype(BF16)
    w3_lo = (w3 - w3_hi.astype(F32)).astype(BF16)
    deltas = jnp.asarray(np.abs(np.linspace(HY_MIN_DECAY, HY_MAX_DECAY, d_hy))[None, :], F32)
    n_fwd = length // rows
    kern = functools.partial(_filter_kernel, length=length, rows=rows)
    full = lambda shape: pl.BlockSpec(shape, lambda i: (0,) * len(shape))
    half = lambda i: (0, (i >= n_fwd).astype(jnp.int32))
    return pl.pallas_call(
        kern,
        grid=(2 * length // rows,),
        in_specs=[
            pl.BlockSpec((rows, HY_EMB_PAD), lambda i: (i, 0)),
            pl.BlockSpec((HY_EMB_PAD, rows), lambda i: (0, i)),
            pl.BlockSpec((HY_EMB_PAD, LANES), lambda i: (0, 0)),
            full((hid, HY_EMB_PAD)), full((hid, 1)), full((hid, 1)),
            full((hid, hid)), full((hid, 1)), full((hid, 1)),
            pl.BlockSpec((hid, nc), half), pl.BlockSpec((hid, nc), half), pl.BlockSpec((1, nc), half),
            pl.BlockSpec((hid, nc), lambda i: (0, 1)), pl.BlockSpec((hid, nc), lambda i: (0, 1)),
            pl.BlockSpec((1, nc), lambda i: (0, 1)),
            full((1, d_hy)),
        ],
        out_specs=[
            pl.BlockSpec((rows, nc), lambda i: (i, 0)),
            pl.BlockSpec((SUBLANES, nc), lambda i: (0, 0)),
        ],
        out_shape=[
            jax.ShapeDtypeStruct((2 * length, nc), F32),
            jax.ShapeDtypeStruct((SUBLANES, nc), F32),
        ],
        compiler_params=_cparams("arbitrary"),
        name="hyena_filter",
    )(ztab, ztab_t, ztab_t, w1t, b1[:, None], f1[:, None], w2.T, b2[:, None], f2[:, None],
      w3_hi, w3_lo, b3[None], w3_hi, w3_lo, b3[None], deltas)


def _dft_tables(length):
    n_fft = 2 * length
    n1 = n_fft // DFT_INNER
    n1h = n1 // 2
    unit = 2.0 * math.pi / n_fft
    k1 = jnp.arange(n1, dtype=jnp.int32)
    n2 = jnp.arange(DFT_INNER, dtype=jnp.int32)

    def cs(n1_count):
        n = DFT_INNER * jnp.arange(n1_count, dtype=jnp.int32)[None, None, :] + n2[:, None, None]
        th = ((k1[None, :, None] * n) % n_fft).astype(F32) * unit
        return jnp.cos(th), jnp.sin(th)

    c, s = cs(n1h)
    outer = jnp.concatenate([jnp.concatenate([c, s], -1), jnp.concatenate([-s, c], -1)], 1)
    ct, st = jnp.swapaxes(c, 1, 2) / n_fft, jnp.swapaxes(s, 1, 2) / n_fft
    outer_inv = jnp.concatenate([jnp.concatenate([ct, -st], -1), jnp.concatenate([st, ct], -1)], 1)
    c, s = cs(n1)
    outer_real = jnp.concatenate([c, -s], 1)
    m = np.arange(DFT_INNER)
    th = 2.0 * math.pi * np.outer(m, m) / DFT_INNER
    c, s = np.cos(th), np.sin(th)
    as_bf = lambda a: jnp.asarray(a, F32).astype(BF16)

    def per_step(w):
        steps, m, r = DFT_INNER // SUBLANES, w.shape[1], w.shape[2]
        w = w.reshape(steps, SUBLANES, m, r).transpose(0, 2, 3, 1).reshape(steps, m, 1, r * SUBLANES)
        k = lax.broadcasted_iota(jnp.int32, (1, 1, SUBLANES, r * SUBLANES), 2)
        j = lax.broadcasted_iota(jnp.int32, (1, 1, SUBLANES, r * SUBLANES), 3) % SUBLANES
        return jnp.where(j == k, w, 0.0).reshape(steps, m * SUBLANES, r * SUBLANES).astype(BF16)

    return dict(outer=per_step(outer), outer_real=per_step(outer_real), outer_inv=per_step(outer_inv),
                inner=as_bf(np.block([[c, s], [-s, c]])), inner_inv=as_bf(np.block([[c, -s], [s, c]])), n1=n1)


def _dft_outer_kernel(x_ref, w_ref, o_ref):
    x = x_ref[...]
    r = _bdot(w_ref[0], x.reshape(-1, x.shape[-1]).astype(BF16))
    o_ref[...] = r.reshape(o_ref.shape)


def _dft_outer_call(x, lead, w, ct):
    c = x.shape[-1]
    m, r = w.shape[1] // SUBLANES, w.shape[2] // SUBLANES
    if lead is None:
        xspec = pl.BlockSpec((x.shape[0], SUBLANES, ct), lambda cc, i: (0, i, cc))
    else:
        xspec = pl.BlockSpec((None, 2, x.shape[2], SUBLANES, ct), lambda cc, i: (lead, 0, 0, i, cc))
    return pl.pallas_call(
        _dft_outer_kernel,
        grid=(c // ct, DFT_INNER // SUBLANES),
        in_specs=[xspec, pl.BlockSpec((1, m * SUBLANES, r * SUBLANES), lambda cc, i: (i, 0, 0))],
        out_specs=pl.BlockSpec((2, m // 2, SUBLANES, ct), lambda cc, i: (0, 0, i, cc)),
        out_shape=jax.ShapeDtypeStruct((2, m // 2, DFT_INNER, c), F32),
        compiler_params=_cparams("arbitrary", "arbitrary"),
        name="hyena_dft_outer",
    )(x, w)


def _filter_spectrum_kernel(a_ref, w_ref, o_ref):
    for q in range(K1_PER_STEP):
        x = _bdot(w_ref[...], jnp.concatenate([a_ref[0, q], a_ref[1, q]], axis=0).astype(BF16))
        o_ref[0, q] = x[:DFT_INNER].astype(BF16)
        o_ref[1, q] = x[DFT_INNER:].astype(BF16)


def _filter_spectrum_call(a, w_inner):
    _, n1, _, c = a.shape
    blk = pl.BlockSpec((2, K1_PER_STEP, DFT_INNER, c), lambda k: (0, k, 0, 0))
    return pl.pallas_call(
        _filter_spectrum_kernel,
        grid=(n1 // K1_PER_STEP,),
        in_specs=[blk, pl.BlockSpec((2 * DFT_INNER, 2 * DFT_INNER), lambda k: (0, 0))],
        out_specs=blk,
        out_shape=jax.ShapeDtypeStruct(a.shape, BF16),
        compiler_params=_cparams("arbitrary"),
        name="hyena_filter_spectrum",
    )(a, w_inner)


def _spectral_kernel(a_ref, kf_ref, w_ref, wi_ref, o_ref):
    for q in range(K1_PER_STEP):
        x = _bdot(w_ref[...], jnp.concatenate([a_ref[0, q], a_ref[1, q]], axis=0).astype(BF16))
        xr, xi = x[:DFT_INNER], x[DFT_INNER:]
        kr, ki = kf_ref[0, q].astype(F32), kf_ref[1, q].astype(F32)
        yr = xr * kr - xi * ki
        yi = xr * ki + xi * kr
        c = _bdot(wi_ref[...], jnp.concatenate([yr, yi], axis=0).astype(BF16))
        o_ref[0, q] = c[:DFT_INNER]
        o_ref[1, q] = c[DFT_INNER:]


def _spectral_call(a, kf, order, w_inner, w_inner_inv):
    _, n1, _, c = a.shape
    blk = pl.BlockSpec((2, K1_PER_STEP, DFT_INNER, c), lambda k: (0, k, 0, 0))
    sq = pl.BlockSpec((2 * DFT_INNER, 2 * DFT_INNER), lambda k: (0, 0))
    return pl.pallas_call(
        _spectral_kernel,
        grid=(n1 // K1_PER_STEP,),
        in_specs=[blk, pl.BlockSpec((2, K1_PER_STEP, DFT_INNER, c), lambda k: (0, k, 0, order)), sq, sq],
        out_specs=blk,
        out_shape=jax.ShapeDtypeStruct(a.shape, F32),
        compiler_params=_cparams("arbitrary"),
        name="hyena_spectral",
    )(a, kf, w_inner, w_inner_inv)


def _dft_outer_inv_kernel(c_ref, w_ref, z_ref, gate_ref, skip_ref, ss_ref, o_ref):
    scale = lax.rsqrt(ss_ref[...] + EPS)
    cc = c_ref[...]
    y = _bdot(w_ref[0], cc.reshape(-1, cc.shape[-1]).astype(BF16)).reshape(o_ref.shape)
    o_ref[...] = gate_ref[...] * (scale * y + skip_ref[...] * z_ref[...])


def _dft_outer_inv_call(cp, w, zs, z_lead, gates, gate_lead, skip, ss):
    _, _, n1h, _, c = zs.shape
    n1 = cp.shape[1]
    zspec = lambda lead: pl.BlockSpec((None, 2, n1h, SUBLANES, c), lambda i: (lead, 0, 0, i, 0))
    vspec = pl.BlockSpec((1, c), lambda i: (0, 0))
    return pl.pallas_call(
        _dft_outer_inv_kernel,
        grid=(DFT_INNER // SUBLANES,),
        in_specs=[pl.BlockSpec((2, n1, SUBLANES, c), lambda i: (0, 0, i, 0)),
                  pl.BlockSpec((1, 2 * n1h * SUBLANES, 2 * n1 * SUBLANES), lambda i: (i, 0, 0)),
                  zspec(z_lead), zspec(gate_lead), vspec, vspec],
        out_specs=zspec(0),
        out_shape=jax.ShapeDtypeStruct((1, 2, n1h, DFT_INNER, c), F32),
        compiler_params=_cparams("arbitrary"),
        name="hyena_dft_outer_inv",
    )(cp, w, zs, gates, skip, ss)


def _ctx_dft_mats(n_ctx):
    n_fft = 2 * n_ctx
    f = np.arange(n_fft)
    th = 2.0 * math.pi * np.outer(f, np.arange(n_ctx)) / n_fft
    fz = np.concatenate([np.cos(th), -np.sin(th)], axis=0)
    th = 2.0 * math.pi * np.outer(f, f) / n_fft
    fk = np.concatenate([np.cos(th), -np.sin(th)], axis=0)
    th = 2.0 * math.pi * np.outer(np.arange(n_ctx), f) / n_fft
    fi = np.concatenate([np.cos(th), -np.sin(th)], axis=1) / n_fft
    as_bf = lambda a: jnp.asarray(a, F32).astype(BF16)
    return as_bf(fz), as_bf(fk), as_bf(fi)


def _hyctx_kernel(p0_ref, p1_ref, p2_ref, cw_ref, cb_ref, kt_ref, ss_ref, skip_ref,
                  fz_ref, fk_ref, fi_ref, o_ref, xe_s, *, d_hy):
    n_fft = fk_ref.shape[1]
    left = HY_CONV // 2

    def conv(p_ref, c):
        xe_s[0:HALO, :] = jnp.zeros((HALO, d_hy), F32)
        xe_s[HALO:HALO + TM, :] = p_ref[...]
        xe_s[HALO + TM:HALO + TM + HALO, :] = jnp.zeros((HALO, d_hy), F32)
        y = cb_ref[:, c * d_hy:(c + 1) * d_hy]
        for k in range(HY_CONV):
            y = y + cw_ref[k:k + 1, c * d_hy:(c + 1) * d_hy] * xe_s[pl.ds(HALO + k - left, TM), :]
        return y

    z = conv(p0_ref, 0)
    gates = (conv(p1_ref, 1), conv(p2_ref, 2))
    for o in range(HY_ORDER):
        cols = slice(o * d_hy, (o + 1) * d_hy)
        kf = _bdot(fk_ref[...], kt_ref[:, cols].astype(BF16))
        zf = _bdot(fz_ref[...], z.astype(BF16))
        zr, zi = zf[:n_fft], zf[n_fft:]
        kr, ki = kf[:n_fft], kf[n_fft:]
        yf = jnp.concatenate([zr * kr - zi * ki, zr * ki + zi * kr], axis=0)
        y = _bdot(fi_ref[...], yf.astype(BF16))
        scale = lax.rsqrt(ss_ref[0:1, cols] + EPS)
        z = gates[o] * (scale * y + skip_ref[o:o + 1, :] * z)
    o_ref[...] = z


def _hyctx_call(p, off_hy, cw, cb, kt, ss, skip, mats, *, batch, n_lat, n_ctx):
    d_hy = skip.shape[1]
    assert n_ctx == TM
    c0 = off_hy // d_hy
    r0 = batch * n_lat // TM
    fz, fk, fi = mats
    full = lambda a: pl.BlockSpec(a.shape, lambda b: (0,) * a.ndim)
    return pl.pallas_call(
        functools.partial(_hyctx_kernel, d_hy=d_hy),
        grid=(batch,),
        in_specs=[pl.BlockSpec((TM, d_hy), (lambda b, c=c0 + k: (r0 + b, c))) for k in range(3)]
        + [full(cw), full(cb), full(kt), full(ss), full(skip), full(fz), full(fk), full(fi)],
        out_specs=pl.BlockSpec((TM, d_hy), lambda b: (b, 0)),
        out_shape=jax.ShapeDtypeStruct((batch * n_ctx, d_hy), F32),
        scratch_shapes=[pltpu.VMEM((TM + 2 * HALO, d_hy), F32)],
        compiler_params=_cparams("arbitrary"),
        name="hyena_ctx",
    )(p, p, p, cw, cb, kt, ss, skip, fz, fk, fi)


def _merge_kernel(x_ref, hf_ref, hb_ref, pg_ref, hy_ref, sg_ref, gg_ref, w_ref, gate_ref, o_ref):
    a = (hf_ref[...] + hb_ref[...]) * _gelu(pg_ref[...])
    y = jnp.concatenate([_rms(a), _rms(hy_ref[...]), _rms(sg_ref[...])], axis=-1) * gg_ref[...]
    o_ref[...] = x_ref[...] + gate_ref[0] * _bdot(y.astype(BF16), w_ref[...])


def _merge_call(x, hf, hb, p, hy, sg, gg, w, gate, seg_of_tile, n_rows):
    d = x.shape[1]
    d_lru, d_hy, d_sg = hf.shape[1], hy.shape[1], sg.shape[1]
    row = lambda width, col=0: pl.BlockSpec((TM, width), lambda i: (i, col))
    return pl.pallas_call(
        _merge_kernel,
        grid=(n_rows // TM,),
        in_specs=[row(d), row(d_lru), row(d_lru), row(d_lru, 1), row(d_hy), row(d_sg),
                  pl.BlockSpec((1, d), lambda i: (0, 0)),
                  pl.BlockSpec((d, d), lambda i: (0, 0)),
                  pl.BlockSpec((1, 1, d), lambda i: (seg_of_tile(i), 0, 0))],
        out_specs=row(d),
        out_shape=jax.ShapeDtypeStruct((n_rows, d), F32),
        compiler_params=_cparams("arbitrary"),
        name="merge_out_proj",
    )(x, hf, hb, p, hy, sg, gg, w, gate)


def _pack_bf16_pairs(x, is_bf16_valued=False):
    half = x.shape[1] // 2
    lo, hi = x[:, :half], x[:, half:]
    if not is_bf16_valued:
        lo, hi = lo.astype(BF16).astype(F32), hi.astype(BF16).astype(F32)
    return (pltpu.bitcast(lo, jnp.uint32) >> 16) | (pltpu.bitcast(hi, jnp.uint32) & jnp.uint32(0xFFFF0000))


def _unpack_bf16_pairs(u):
    lo = pltpu.bitcast(u << 16, F32).astype(BF16)
    hi = pltpu.bitcast(u & jnp.uint32(0xFFFF0000), F32).astype(BF16)
    return lo, hi


def _router_kernel(x_ref, g_ref, sc_ref, sh_ref, rw_ref, rb_ref, f_ref, e_ref, gt_ref, m_ref):
    f = _rms(x_ref[...]) * g_ref[...] * (1.0 + sc_ref[0]) + sh_ref[0]
    f_hi = f.astype(BF16)
    f_ref[...] = f_hi
    f_lo = (f - f_hi.astype(F32)).astype(BF16)
    part = _bdot(f_hi, rw_ref[...])
    logits = part[:, :LANES] + part[:, LANES:] + _bdot(f_lo, rw_ref[:, :LANES]) + rb_ref[...]
    lane = lax.broadcasted_iota(jnp.int32, logits.shape, 1).astype(F32)
    work = logits
    e_out = jnp.zeros_like(logits)
    mask = jnp.zeros_like(logits)
    vals = []
    for k in range(TOP_K):
        mx = jnp.max(work, axis=-1, keepdims=True)
        idx = jnp.min(jnp.where(work == mx, lane, float(LANES)), axis=-1, keepdims=True)
        sel = lane == idx
        e_out = jnp.where(lane == float(k), idx, e_out)
        mask = jnp.where(sel, 1.0, mask)
        work = jnp.where(sel, -jnp.inf, work)
        vals.append(mx)
    ex = [jnp.exp(v - vals[0]) for v in vals]
    den = ex[0]
    for v in ex[1:]:
        den = den + v
    g_out = jnp.zeros_like(logits)
    for k in range(TOP_K):
        g_out = jnp.where(lane == float(k), ex[k] / den, g_out)
    e_ref[...] = e_out.astype(jnp.int32)
    gt_ref[...] = g_out
    m_ref[...] = mask


def _router_call(x, g, sc, sh, rw, rb, seg_of_tile, n_rows):
    d = x.shape[1]
    row = lambda width: pl.BlockSpec((TM, width), lambda i: (i, 0))
    mod = pl.BlockSpec((1, 1, d), lambda i: (seg_of_tile(i), 0, 0))
    return pl.pallas_call(
        _router_kernel,
        grid=(n_rows // TM,),
        in_specs=[row(d), pl.BlockSpec((1, d), lambda i: (0, 0)), mod, mod,
                  pl.BlockSpec((d, 2 * LANES), lambda i: (0, 0)), pl.BlockSpec((1, LANES), lambda i: (0, 0))],
        out_specs=[row(d), row(LANES), row(LANES), row(LANES)],
        out_shape=[jax.ShapeDtypeStruct((n_rows, d), BF16), jax.ShapeDtypeStruct((n_rows, LANES), jnp.int32),
                   jax.ShapeDtypeStruct((n_rows, LANES), F32), jax.ShapeDtypeStruct((n_rows, LANES), F32)],
        compiler_params=_cparams("arbitrary"),
        name="moe_router",
    )(x, g, sc, sh, rw, rb)


def _rank_kernel(m_ref, e_ref, tri_ref, utri_ref, ls_ref, meta_ref, tot_ref, carry_s):
    i = pl.program_id(0)

    @pl.when(i == 0)
    def _():
        carry_s[...] = jnp.zeros_like(carry_s)

    m = m_ref[...]
    rank = _bdot(tri_ref[...], m.astype(BF16))
    cnt = jnp.sum(m, axis=0, keepdims=True)
    seg = jnp.floor((cnt + (SEG_ROWS - 1)) * (1.0 / SEG_ROWS)) * SEG_ROWS
    seg8 = jnp.broadcast_to(seg, (SUBLANES, LANES))
    loc = _bdot(seg8.astype(BF16), utri_ref[...])[0:1, :]
    sd = rank + loc
    e = e_ref[...]
    lane = lax.broadcasted_iota(jnp.int32, sd.shape, 1)
    out = jnp.zeros_like(sd)
    for k in range(TOP_K):
        sk = jnp.sum(jnp.where(lane == e[:, k:k + 1], sd, 0.0), axis=-1, keepdims=True)
        out = jnp.where(lane == k, sk, out)
    ls_ref[...] = out.astype(jnp.int32)
    row = lax.broadcasted_iota(jnp.int32, (SUBLANES, LANES), 0)
    meta_ref[0] = jnp.where(row == 0, carry_s[...], jnp.where(row == 1, seg8, 0.0))
    carry_s[...] = carry_s[...] + seg8
    tot_ref[...] = carry_s[...]


def _rank_call(mask, e4):
    n = mask.shape[0]
    tri = jnp.asarray(np.tril(np.ones((TM, TM)), -1), BF16)
    utri = jnp.asarray(np.triu(np.ones((LANES, LANES)), 1), BF16)
    row = pl.BlockSpec((TM, LANES), lambda i: (i, 0))
    return pl.pallas_call(
        _rank_kernel,
        grid=(n // TM,),
        in_specs=[row, row, pl.BlockSpec((TM, TM), lambda i: (0, 0)), pl.BlockSpec((LANES, LANES), lambda i: (0, 0))],
        out_specs=[row, pl.BlockSpec((1, SUBLANES, LANES), lambda i: (i, 0, 0)),
                   pl.BlockSpec((SUBLANES, LANES), lambda i: (0, 0))],
        out_shape=[jax.ShapeDtypeStruct((n, LANES), jnp.int32), jax.ShapeDtypeStruct((n // TM, SUBLANES, LANES), F32),
                   jax.ShapeDtypeStruct((SUBLANES, LANES), F32)],
        scratch_shapes=[pltpu.VMEM((SUBLANES, LANES), F32)],
        compiler_params=_cparams("arbitrary"),
        name="moe_rank",
    )(mask, e4, tri, utri)


def _seg_copy(src, dst, src_row, dst_row, sem):
    return pltpu.make_async_copy(src.at[pl.ds(src_row, SEG_ROWS)], dst.at[pl.ds(dst_row, SEG_ROWS)], sem)


def _drain(copy, count):
    def body(r, c):
        copy.wait()
        return c

    lax.fori_loop(0, count, body, 0)


def _tile_segments(seg_ref, nch_ref, tile, move):
    off = jnp.int32(0)
    for e in range(N_EXPERTS):
        nc = nch_ref[tile * N_EXPERTS + e]
        g0 = seg_ref[tile * N_EXPERTS + e]

        def body(c, carry, off=off, g0=g0):
            move(pl.multiple_of(off + c * SEG_ROWS, SEG_ROWS), pl.multiple_of(g0 + c * SEG_ROWS, SEG_ROWS))
            return carry

        lax.fori_loop(0, nc, body, 0)
        off = off + nc * SEG_ROWS
    return off // SEG_ROWS


def _tile_chunks(nch_ref, tile):
    total = jnp.int32(0)
    for e in range(N_EXPERTS):
        total = total + nch_ref[tile * N_EXPERTS + e]
    return total


def _dispatch_kernel(seg_ref, nch_ref, pad_ref, f_ref, ls_ref, xs_hbm, xl_s, zero_s, sem, zsem):
    i = pl.program_id(0)
    n = pl.num_programs(0)
    rows = xl_s.shape[1]
    cur = i % 2

    def sent(tile, b):
        _drain(_seg_copy(xl_s.at[b], xs_hbm, 0, 0, sem.at[b]), _tile_chunks(nch_ref, tile))

    @pl.when(i >= 2)
    def _():
        sent(i - 2, cur)

    ls_t = ls_ref[...].astype(F32).T
    r_iota = lax.broadcasted_iota(jnp.int32, (rows, TM), 0).astype(F32)
    perm = jnp.zeros((rows, TM), F32)
    for k in range(TOP_K):
        perm = jnp.where(r_iota == ls_t[k:k + 1, :], 1.0, perm)
    xl_s[cur] = _pack_bf16_pairs(_bdot(perm.astype(BF16), f_ref[...]), is_bf16_valued=True)

    _tile_segments(seg_ref, nch_ref, i, lambda lr, gr: _seg_copy(xl_s.at[cur], xs_hbm, lr, gr, sem.at[cur]).start())

    @pl.when(i == 0)
    def _():
        zero_s[...] = jnp.zeros_like(zero_s)
        total = jnp.int32(0)
        for e in range(N_EXPERTS):
            nc = pad_ref[N_EXPERTS + e]
            g0 = pad_ref[e]

            def body(c, carry, g0=g0):
                _seg_copy(zero_s, xs_hbm, 0, pl.multiple_of(g0 + c * SEG_ROWS, SEG_ROWS), zsem).start()
                return carry

            lax.fori_loop(0, nc, body, 0)
            total = total + nc
        _drain(_seg_copy(zero_s, xs_hbm, 0, 0, zsem), total)

        def block_copy(b):
            return pltpu.make_async_copy(zero_s, xs_hbm.at[pl.ds(pl.multiple_of(b * MOE_BLOCK, MOE_BLOCK), MOE_BLOCK)], zsem)

        n_used = pad_ref[2 * N_EXPERTS]
        n_blocks = xs_hbm.shape[0] // MOE_BLOCK

        def start_block(b, carry):
            block_copy(b).start()
            return carry

        def wait_block(b, carry):
            block_copy(b).wait()
            return carry

        lax.fori_loop(n_used, n_blocks, start_block, 0)
        lax.fori_loop(n_used, n_blocks, wait_block, 0)

    @pl.when(i == n - 1)
    def _():
        @pl.when(i >= 1)
        def _():
            sent(i - 1, 1 - cur)

        sent(i, cur)


def _dispatch_call(seg, nch, pad, f, ls, n_rows):
    n, d = f.shape
    return pl.pallas_call(
        _dispatch_kernel,
        grid_spec=pltpu.PrefetchScalarGridSpec(
            num_scalar_prefetch=3,
            grid=(n // TM,),
            in_specs=[pl.BlockSpec((TM, d), lambda i, *_: (i, 0)), pl.BlockSpec((TM, LANES), lambda i, *_: (i, 0))],
            out_specs=pl.BlockSpec(memory_space=pl.ANY),
            scratch_shapes=[pltpu.VMEM((2, LOCAL_ROWS, d // 2), jnp.uint32), pltpu.VMEM((MOE_BLOCK, d // 2), jnp.uint32),
                            pltpu.SemaphoreType.DMA((2,)), pltpu.SemaphoreType.DMA(())],
        ),
        out_shape=jax.ShapeDtypeStruct((n_rows, d // 2), jnp.uint32),
        compiler_params=_cparams("arbitrary", row_dma=True),
        name="moe_dispatch",
    )(seg, nch, pad, f, ls)


def _expert_kernel(be_ref, nu_ref, x_ref, w1_ref, b1_ref, w2_ref, b2_ref, o_ref, w1_s, w2_s):
    i = pl.program_id(0)
    d_e = w2_ref.shape[2]
    half = x_ref.shape[1]
    live = i < nu_ref[0]
    new_expert = jnp.logical_or(i == 0, be_ref[i] != be_ref[jnp.maximum(i - 1, 0)])

    @pl.when(jnp.logical_and(live, new_expert))
    def _():
        w1_s[...] = w1_ref[0, 0].astype(BF16)
        w2_s[...] = w2_ref[0, 0].astype(BF16)

    @pl.when(live)
    def _():
        lo, hi = _unpack_bf16_pairs(x_ref[...])
        gu = _bdot(lo, w1_s[:half, :]) + _bdot(hi, w1_s[half:, :]) + b1_ref[0, 0]
        glu = jnp.minimum(gu[:, :d_e], SWIGLU_LIMIT)
        lin = jnp.clip(gu[:, d_e:], -SWIGLU_LIMIT, SWIGLU_LIMIT)
        act = glu * _sigmoid(SWIGLU_ALPHA * glu) * (lin + 1.0)
        o_ref[...] = _pack_bf16_pairs(_bdot(act.astype(BF16), w2_s[...]) + b2_ref[0, 0])

    @pl.when(i >= nu_ref[0])
    def _():
        o_ref[...] = jnp.zeros_like(o_ref)


def _expert_call(block_e, n_used, xs, w1, b1, w2, b2, layer):
    n_rows, half = xs.shape
    depth, n_e, d, d_gu = w1.shape
    d_e = w2.shape[2]
    used = lambda i, nu: jnp.minimum(i, nu[0] - 1)
    return pl.pallas_call(
        _expert_kernel,
        grid_spec=pltpu.PrefetchScalarGridSpec(
            num_scalar_prefetch=2,
            grid=(n_rows // MOE_BLOCK,),
            in_specs=[
                pl.BlockSpec((MOE_BLOCK, half), lambda i, be, nu: (used(i, nu), 0)),
                pl.BlockSpec((1, 1, d, d_gu), lambda i, be, nu: (layer, be[i], 0, 0)),
                pl.BlockSpec((1, 1, 1, d_gu), lambda i, be, nu: (layer, be[i], 0, 0)),
                pl.BlockSpec((1, 1, d_e, d), lambda i, be, nu: (layer, be[i], 0, 0)),
                pl.BlockSpec((1, 1, 1, d), lambda i, be, nu: (layer, be[i], 0, 0)),
            ],
            out_specs=pl.BlockSpec((MOE_BLOCK, half), lambda i, be, nu: (i, 0)),
            scratch_shapes=[pltpu.VMEM((d, d_gu), BF16), pltpu.VMEM((d_e, d), BF16)],
        ),
        out_shape=jax.ShapeDtypeStruct((n_rows, half), jnp.uint32),
        compiler_params=_cparams("arbitrary"),
        name="moe_experts",
    )(block_e, n_used, xs, w1, b1.reshape(depth, n_e, 1, d_gu), w2, b2.reshape(depth, n_e, 1, d))


def _combine_kernel(seg_ref, nch_ref, x_ref, ls_ref, g_ref, gate_ref, ys_hbm, o_ref, yl_s, sem):
    i = pl.program_id(0)
    n = pl.num_programs(0)
    rows = yl_s.shape[1]

    def fetch(tile, b):
        return _tile_segments(seg_ref, nch_ref, tile,
                              lambda lr, gr: _seg_copy(ys_hbm, yl_s.at[b], gr, lr, sem.at[b]).start())

    @pl.when(i == 0)
    def _():
        yl_s[...] = jnp.zeros_like(yl_s)
        fetch(0, 0)

    @pl.when(i + 1 < n)
    def _():
        fetch(i + 1, (i + 1) % 2)

    cur = i % 2
    total = jnp.int32(0)
    for e in range(N_EXPERTS):
        total = total + nch_ref[i * N_EXPERTS + e]
    _drain(_seg_copy(ys_hbm, yl_s.at[cur], 0, 0, sem.at[cur]), total)

    ls = ls_ref[...]
    g = g_ref[...]
    r_iota = lax.broadcasted_iota(jnp.int32, (TM, rows), 1)
    wsel = jnp.zeros((TM, rows), F32)
    for k in range(TOP_K):
        wsel = jnp.where(r_iota == ls[:, k:k + 1], g[:, k:k + 1], wsel)
    wsel = wsel.astype(BF16)
    lo, hi = _unpack_bf16_pairs(yl_s[cur])
    acc = jnp.concatenate([_bdot(wsel, lo), _bdot(wsel, hi)], axis=-1)
    o_ref[...] = x_ref[...] + gate_ref[0] * acc


def _combine_call(seg, nch, x, ls, g4, gate, ys, seg_of_tile, n_tok):
    d = x.shape[1]
    row = lambda width: pl.BlockSpec((TM, width), lambda i, *_: (i, 0))
    return pl.pallas_call(
        _combine_kernel,
        grid_spec=pltpu.PrefetchScalarGridSpec(
            num_scalar_prefetch=2,
            grid=(n_tok // TM,),
            in_specs=[row(d), row(LANES), row(LANES),
                      pl.BlockSpec((1, 1, d), lambda i, *_: (seg_of_tile(i), 0, 0)),
                      pl.BlockSpec(memory_space=pl.ANY)],
            out_specs=row(d),
            scratch_shapes=[pltpu.VMEM((2, LOCAL_ROWS, d // 2), jnp.uint32), pltpu.SemaphoreType.DMA((2,))],
        ),
        out_shape=jax.ShapeDtypeStruct((n_tok, d), F32),
        compiler_params=_cparams("arbitrary", row_dma=True),
        name="moe_combine",
    )(seg, nch, x, ls, g4, gate, ys)


def _moe(x, norm_g, sc, sh, gate, rw, rb, w1, b1, w2, b2, layer, seg_of_row_tile, n_tok):
    n_tiles = n_tok // TM
    f, e4, g4, mask = _router_call(x, norm_g, sc, sh, rw, rb, seg_of_row_tile(TM), n_tok)
    ls, meta, tot = _rank_call(mask, e4)
    rows_e = tot[0, :N_EXPERTS].astype(jnp.int32)
    region = (rows_e + MOE_BLOCK - 1) // MOE_BLOCK * MOE_BLOCK
    pend = jnp.cumsum(region)
    pstart = pend - region
    max_rows = n_tok * TOP_K + n_tiles * N_EXPERTS * (SEG_ROWS - 1) + N_EXPERTS * (MOE_BLOCK - 1)
    n_blocks = -(-max_rows // MOE_BLOCK)
    first_row = jnp.arange(n_blocks, dtype=jnp.int32) * MOE_BLOCK
    block_e = jnp.minimum(jnp.sum((pend[None, :] <= first_row[:, None]).astype(jnp.int32), axis=1), N_EXPERTS - 1)
    n_used = (pend[-1:] // MOE_BLOCK).astype(jnp.int32)
    seg = (pstart[None, :] + meta[:, 0, :N_EXPERTS].astype(jnp.int32)).reshape(-1)
    nch = (meta[:, 1, :N_EXPERTS].astype(jnp.int32) // SEG_ROWS).reshape(-1)
    pad = jnp.concatenate([pstart + rows_e, (region - rows_e) // SEG_ROWS, n_used])
    xs = _dispatch_call(seg, nch, pad, f, ls, n_blocks * MOE_BLOCK)
    ys = _expert_call(block_e, n_used, xs, w1, b1, w2, b2, layer)
    return _combine_call(seg, nch, x, ls, g4, gate, ys, seg_of_row_tile(TM), n_tok)


def _final_kernel(x_ref, g_ref, o_ref):
    o_ref[...] = _rms(x_ref[...]) * g_ref[...]


def _final_call(x, g, n_rows):
    d = x.shape[1]
    return pl.pallas_call(
        _final_kernel,
        grid=(n_rows // TM,),
        in_specs=[pl.BlockSpec((TM, d), lambda i: (i, 0)), pl.BlockSpec((1, d), lambda i: (0, 0))],
        out_specs=pl.BlockSpec((TM, d), lambda i: (i, 0)),
        out_shape=jax.ShapeDtypeStruct((n_rows, d), F32),
        compiler_params=_cparams("arbitrary"),
        name="final_norm",
    )(x, g)


def _blockdiag(w):
    h, hd, _ = w.shape
    return jnp.einsum('hij,hg->higj', w, jnp.eye(h, dtype=w.dtype)).reshape(h * hd, h * hd)


def kernel(x, c, ctx, c_ctx, ada_w, ada_b, norm_mix_g, w_in, lru_conv_w, lru_conv_b, lru_wa, lru_ba, lru_wx, lru_bx, lru_lam, hy_conv_w, hy_conv_b, hf_w1, hf_b1, hf_f1, hf_w2, hf_b2, hf_f2, hf_w3, hf_b3, hy_skip, sg_ln_g, sg_ln_b, sg_w, sg_b, grp_norm_g, w_out, norm_ffn_g, router_w, router_b, moe_w1, moe_b1, moe_w2, moe_b2, final_norm_g):
    batch, n_lat, d = x.shape
    n_ctx = ctx.shape[1]
    depth = ada_w.shape[0]
    d_lru = lru_conv_w.shape[2]
    d_hy = hy_skip.shape[2]
    d_sg = sg_ln_g.shape[1]
    off_hy = 2 * d_lru
    off_sg = off_hy + 3 * d_hy
    assert batch == 2 and n_ctx == TM and n_lat % (TM * 2) == 0 and batch + 1 <= SUBLANES
    n_lat_total = batch * n_lat
    n_all = n_lat_total + batch * n_ctx

    def seg_of_row_tile(rows):
        per_seq = n_lat // rows
        return lambda i: jnp.minimum(i // per_seq, batch)

    cond = jnp.zeros((SUBLANES, d), F32).at[:batch].set(c).at[batch].set(c_ctx)
    mods = _ada_call(cond, ada_w, ada_b)

    mats = _dft_tables(n_lat)
    ctx_mats = _ctx_dft_mats(n_ctx)
    n1 = mats["n1"]
    sg_bias_shape = (SG_CHUNK, d_sg)

    xt = jnp.concatenate([x.reshape(n_lat_total, d), ctx.reshape(batch * n_ctx, d)], axis=0)
    for l in range(depth):
        ctx_out = l < depth - 1
        m = mods[l, :batch + 1].reshape(batch + 1, N_MOD, d)
        mod = [m[:, j][:, None, :] for j in range(N_MOD)]

        p = _inproj_call(xt, norm_mix_g[l][None], mod[1], mod[0], w_in[l].astype(BF16), seg_of_row_tile(TM))

        hs = []
        for dr, rev in enumerate((False, True)):
            wg = jnp.concatenate([_blockdiag(lru_wa[l, dr]), _blockdiag(lru_wx[l, dr])], axis=1).astype(BF16)
            bg = jnp.concatenate([lru_ba[l, dr], lru_bx[l, dr]])[None]
            hs.append(_lru_call(p, lru_conv_w[l], lru_conv_b[l][None], wg, bg, lru_lam[l, dr][None],
                                reverse=rev, batch=batch, n_lat=n_lat, n_ctx=n_ctx))

        sg_bias = jnp.broadcast_to(sg_b[l].T[:, :, None], (SG_CHUNK, SG_HEADS, d_sg // SG_HEADS)).reshape(sg_bias_shape)
        sg = _sg_call(p, off_sg, sg_ln_g[l][None], sg_ln_b[l][None], sg_w[l].astype(BF16), sg_bias)

        filt = (hf_w1[l], hf_b1[l], hf_f1[l], hf_w2[l], hf_b2[l], hf_f2[l], hf_w3[l], hf_b3[l])
        kt, ss = _filter_call(n_lat, *filt, d_hy)
        ka = _dft_outer_call(kt.reshape(n1, DFT_INNER, HY_ORDER * d_hy), None, mats["outer_real"], d_hy)
        kf = _filter_spectrum_call(ka, mats["inner"])
        vxx = _hyconv_call(p, off_hy, hy_conv_w[l], hy_conv_b[l][None], batch=batch, n_lat=n_lat)
        vxx = vxx.reshape(3, batch, n1 // 2, DFT_INNER, d_hy)
        zs, z_lead = vxx, 0
        for o in range(HY_ORDER):
            a = _dft_outer_call(zs, z_lead, mats["outer"], d_hy)
            cp = _spectral_call(a, kf, o, mats["inner"], mats["inner_inv"])
            zs = _dft_outer_inv_call(cp, mats["outer_inv"], zs, z_lead, vxx, 1 + o, hy_skip[l, o][None],
                                     ss[0:1, o * d_hy:(o + 1) * d_hy])
            z_lead = 0
        hy = zs.reshape(n_lat_total, d_hy)
        if ctx_out:
            ktc, ssc = _filter_call(n_ctx, *filt, d_hy)
            hyc = _hyctx_call(p, off_hy, hy_conv_w[l], hy_conv_b[l][None], ktc, ssc, hy_skip[l], ctx_mats,
                              batch=batch, n_lat=n_lat, n_ctx=n_ctx)
            hy = jnp.concatenate([hy, hyc], axis=0)

        n_tok = n_all if ctx_out else n_lat_total
        xt = _merge_call(xt, hs[0], hs[1], p, hy, sg, grp_norm_g[l][None], w_out[l].astype(BF16), mod[2],
                         seg_of_row_tile(TM), n_tok)

        rw = jnp.zeros((d, LANES), F32).at[:, :N_EXPERTS].set(router_w[l])
        rw_hi = rw.astype(BF16)
        rw = jnp.concatenate([rw_hi, (rw - rw_hi.astype(F32)).astype(BF16)], axis=1)
        rb = jnp.full((1, LANES), -1e30, F32).at[0, :N_EXPERTS].set(router_b[l])
        xt = _moe(xt, norm_ffn_g[l][None], mod[4], mod[3], mod[5], rw, rb, moe_w1, moe_b1, moe_w2, moe_b2,
                  l, seg_of_row_tile, n_tok)

    return _final_call(xt, final_norm_g[None], n_lat_total).reshape(batch, n_lat, d)
```
